```python
import math
import jax, jax.numpy as jnp
from jax import lax
import numpy as np

D_MODEL = 1024
BATCH = 4
SEQ = 4096
DEPTH = 2
DEC_BATCH = 32
DEC_SEQ = 1
PAST_LEN = 8192
PAGE_SIZE = 128

N_A = DEPTH // 2
N_B = DEPTH - N_A
LRU_WIDTH = D_MODEL
N_LRU_BLOCKS = 4
LRU_BLOCK = LRU_WIDTH // N_LRU_BLOCKS
CONV_A = 4
LRU_C = 8.0
D_FF = 3 * D_MODEL
CONV_F = 3
HEAD_DIM = 64
HEADS_PER_GROUP = 8
GROUPS = ((128, 1), (512, 4), (2048, 16))
N_GROUPS = len(GROUPS)
Q_WIDTH = N_GROUPS * HEADS_PER_GROUP * HEAD_DIM
ATT_OUT = HEADS_PER_GROUP * HEAD_DIM
EPS = 1e-6

kernel_name = 'yoco_hawk_dilated_swa_step'


def rmsnorm(x, g):
    xf = x.astype(jnp.float32)
    y = xf * lax.rsqrt(jnp.mean(xf * xf, axis=-1, keepdims=True) + EPS)
    return (y * g.astype(jnp.float32)).astype(x.dtype)


def causal_dwconv(x, buf, w, b):
    K = w.shape[0]
    T = x.shape[1]
    xp = jnp.concatenate([buf.astype(x.dtype), x], axis=1)
    y = b + sum(w[k] * xp[:, k:k + T] for k in range(K))
    return y, xp[:, T:]


def block_diag(x, w, b):
    B_, T = x.shape[:2]
    xb = x.reshape(B_, T, N_LRU_BLOCKS, LRU_BLOCK)
    return jnp.einsum('btni,nij->btnj', xb, w.astype(jnp.float32)).reshape(B_, T, LRU_WIDTH) + b.astype(jnp.float32)


def rg_lru(x, h0, w_a, b_a, w_x, b_x, lam):
    xf = x.astype(jnp.float32)
    r = jax.nn.sigmoid(block_diag(xf, w_a, b_a))
    i = jax.nn.sigmoid(block_diag(xf, w_x, b_x))
    log_a = LRU_C * r * jax.nn.log_sigmoid(lam.astype(jnp.float32))
    a = jnp.exp(log_a)
    bt = jnp.sqrt(-jnp.expm1(2.0 * log_a)) * (i * xf)
    bt = bt.at[:, 0].add(a[:, 0] * h0.astype(jnp.float32))

    def combine(left, right):
        a1, b1 = left
        a2, b2 = right
        return a1 * a2, a2 * b1 + b2

    _, h = lax.associative_scan(combine, (a, bt), axis=1)
    return h.astype(x.dtype), h[:, -1].astype(x.dtype)


def recurrent_block(x, h0, conv_buf, w_in, conv_w, conv_b, w_a, b_a, w_x, b_x, lam, w_out):
    u = x @ w_in
    gate = jax.nn.gelu(u[..., :LRU_WIDTH])
    xc, new_buf = causal_dwconv(u[..., LRU_WIDTH:], conv_buf, conv_w, conv_b)
    h, h_last = rg_lru(xc, h0, w_a, b_a, w_x, b_x, lam)
    return (h * gate) @ w_out, h_last, new_buf


def conv_ffn(x, buf, w_up, conv_w, conv_b, w_down):
    u = x @ w_up
    g, new_buf = causal_dwconv(u[..., :D_FF], buf, conv_w, conv_b)
    return (jax.nn.gelu(g) * u[..., D_FF:]) @ w_down, new_buf


def dilated_window_prompt(q, k, v, window, dilation):
    B_, T, H, Dh = q.shape
    blk = window // dilation
    n = T // dilation
    nb = -(-n // blk)
    pad = nb * blk - n

    def to_blocks(t):
        t = t.reshape(B_, n, dilation, H, Dh).transpose(0, 2, 1, 3, 4)
        t = jnp.pad(t, ((0, 0), (0, 0), (0, pad), (0, 0), (0, 0)))
        return t.reshape(B_, dilation, nb, blk, H, Dh)

    def with_prev(t):
        prev = jnp.pad(t[:, :, :-1], ((0, 0), (0, 0), (1, 0), (0, 0), (0, 0), (0, 0)))
        return jnp.concatenate([prev, t], axis=3)

    qb = to_blocks(q).astype(jnp.float32)
    kk = with_prev(to_blocks(k)).astype(jnp.float32)
    vv = with_prev(to_blocks(v)).astype(jnp.float32)
    s = jnp.einsum('brnqhd,brnkhd->brnhqk', qb, kk) * (Dh ** -0.5)
    qi = jnp.arange(blk)[:, None]
    ki = jnp.arange(2 * blk)[None, :]
    dist = blk + qi - ki
    key_step = (jnp.arange(nb)[:, None, None] - 1) * blk + ki[None]
    valid = (dist >= 0) & (dist <= blk) & (key_step >= 0)
    s = jnp.where(valid[None, None, :, None], s, -jnp.inf)
    m = jnp.max(s, axis=-1, keepdims=True)
    p = jnp.exp(s - m)
    den = jnp.sum(p, axis=-1, keepdims=True)
    o = jnp.einsum('brnhqk,brnkhd->brnqhd', p, vv) / jnp.swapaxes(den, 3, 4)
    lse = jnp.swapaxes((m + jnp.log(den))[..., 0], 3, 4)
    o = o.reshape(B_, dilation, nb * blk, H, Dh)[:, :, :n].transpose(0, 2, 1, 3, 4).reshape(B_, T, H, Dh)
    lse = lse.reshape(B_, dilation, nb * blk, H)[:, :, :n].transpose(0, 2, 1, 3).reshape(B_, T, H)
    return o, lse


def dilated_window_sample(q, ke, ve, buf_len, window, dilation):
    S = q.shape[1]
    Dh = q.shape[-1]
    nk = window // dilation + 1
    idx = buf_len + jnp.arange(S)[:, None] - dilation * jnp.arange(nk)[None, :]
    valid = idx >= 0
    idxc = jnp.maximum(idx, 0)
    kg = ke[:, idxc].astype(jnp.float32)
    vg = ve[:, idxc].astype(jnp.float32)
    s = jnp.einsum('bshd,bsjhd->bshj', q.astype(jnp.float32), kg) * (Dh ** -0.5)
    s = jnp.where(valid[None, :, None, :], s, -jnp.inf)
    m = jnp.max(s, axis=-1, keepdims=True)
    p = jnp.exp(s - m)
    den = jnp.sum(p, axis=-1, keepdims=True)
    o = jnp.einsum('bshj,bsjhd->bshd', p, vg) / den
    return o, (m + jnp.log(den))[..., 0]


def trunk(x, lru_h, conv_a_buf, ffn_buf, kv_bufs,
          norm_mix, a_w_in, a_conv_w, a_conv_b, a_gate_a_w, a_gate_a_b, a_gate_x_w, a_gate_x_b,
          a_lambda, a_w_out, kv_norm, w_kv, b_w_q, b_w_o,
          norm_ffn, ffn_w_up, ffn_conv_w, ffn_conv_b, ffn_w_down, final_norm):
    Bsz, T = x.shape[:2]
    h = x
    new_lru, new_conva, new_ffn, kv_state = [], [], [], []
    ks, vs, buf_lens = [], [], []
    for l in range(DEPTH):
        if l < N_A:
            a_out, h_last, cbuf = recurrent_block(
                rmsnorm(h, norm_mix[l]), lru_h[l], conv_a_buf[l], a_w_in[l], a_conv_w[l], a_conv_b[l],
                a_gate_a_w[l], a_gate_a_b[l], a_gate_x_w[l], a_gate_x_b[l], a_lambda[l], a_w_out[l])
            h = h + a_out
            new_lru.append(h_last)
            new_conva.append(cbuf)
        else:
            if l == N_A:
                kv = rmsnorm(h, kv_norm) @ w_kv
                k_all = kv[..., :Q_WIDTH].reshape(Bsz, T, N_GROUPS, HEADS_PER_GROUP, HEAD_DIM)
                v_all = kv[..., Q_WIDTH:].reshape(Bsz, T, N_GROUPS, HEADS_PER_GROUP, HEAD_DIM)
                for g, (win, dil) in enumerate(GROUPS):
                    kg, vg = k_all[:, :, g], v_all[:, :, g]
                    if kv_bufs is None:
                        keep = min(win, T)
                        ks.append(kg)
                        vs.append(vg)
                        buf_lens.append(0)
                        kv_state += [kg[:, T - keep:], vg[:, T - keep:]]
                    else:
                        kb, vb = kv_bufs[2 * g], kv_bufs[2 * g + 1]
                        ke = jnp.concatenate([kb.astype(kg.dtype), kg], axis=1)
                        ve = jnp.concatenate([vb.astype(vg.dtype), vg], axis=1)
                        keep = min(win, ke.shape[1])
                        ks.append(ke)
                        vs.append(ve)
                        buf_lens.append(kb.shape[1])
                        kv_state += [ke[:, ke.shape[1] - keep:], ve[:, ve.shape[1] - keep:]]
            j = l - N_A
            q = (rmsnorm(h, norm_mix[l]) @ b_w_q[j]).reshape(Bsz, T, N_GROUPS, HEADS_PER_GROUP, HEAD_DIM)
            outs, lses = [], []
            for g, (win, dil) in enumerate(GROUPS):
                if kv_bufs is None:
                    o, lse = dilated_window_prompt(q[:, :, g], ks[g], vs[g], win, dil)
                else:
                    o, lse = dilated_window_sample(q[:, :, g], ks[g], vs[g], buf_lens[g], win, dil)
                outs.append(o)
                lses.append(lse)
            wgt = jax.nn.softmax(jnp.stack(lses, axis=0), axis=0)
            o = jnp.sum(wgt[..., None] * jnp.stack(outs, axis=0), axis=0)
            h = h + o.reshape(Bsz, T, ATT_OUT).astype(h.dtype) @ b_w_o[j]
        f_out, fbuf = conv_ffn(rmsnorm(h, norm_ffn[l]), ffn_buf[l], ffn_w_up[l], ffn_conv_w[l],
                               ffn_conv_b[l], ffn_w_down[l])
        h = h + f_out
        new_ffn.append(fbuf)
    y = rmsnorm(h, final_norm)
    return y, jnp.stack(new_lru), jnp.stack(new_conva), jnp.stack(new_ffn), kv_state


def setup_inputs(seed: int = 0) -> dict:
    key = jax.random.key(seed)
    ks = jax.random.split(key, 40)
    f32 = jnp.float32
    nrm = lambda k, shape, scale: jax.random.normal(k, shape, f32) * scale
    lens = [min(w, PAST_LEN) for (w, _) in GROUPS]
    u = jax.random.uniform(ks[13], (N_A, LRU_WIDTH), f32, 0.9, 0.999)
    return {
        'x_prompt': nrm(ks[0], (BATCH, SEQ, D_MODEL), 1.0),
        'x_sample': nrm(ks[1], (DEC_BATCH, DEC_SEQ, D_MODEL), 1.0),
        'state_lru_h': nrm(ks[2], (N_A, DEC_BATCH, LRU_WIDTH), 0.5),
        'state_conv_a': nrm(ks[3], (N_A, DEC_BATCH, CONV_A - 1, LRU_WIDTH), 0.5),
        'state_ffn_conv': nrm(ks[4], (DEPTH, DEC_BATCH, CONV_F - 1, D_FF), 0.5),
        'cache_k0': nrm(ks[5], (DEC_BATCH, lens[0], HEADS_PER_GROUP, HEAD_DIM), 1.0),
        'cache_v0': nrm(ks[6], (DEC_BATCH, lens[0], HEADS_PER_GROUP, HEAD_DIM), 1.0),
        'cache_k1': nrm(ks[7], (DEC_BATCH, lens[1], HEADS_PER_GROUP, HEAD_DIM), 1.0),
        'cache_v1': nrm(ks[8], (DEC_BATCH, lens[1], HEADS_PER_GROUP, HEAD_DIM), 1.0),
        'cache_k2': nrm(ks[9], (DEC_BATCH, lens[2], HEADS_PER_GROUP, HEAD_DIM), 1.0),
        'cache_v2': nrm(ks[10], (DEC_BATCH, lens[2], HEADS_PER_GROUP, HEAD_DIM), 1.0),
        'norm_mix': 1.0 + nrm(ks[11], (DEPTH, D_MODEL), 0.01),
        'a_w_in': nrm(ks[12], (N_A, D_MODEL, 2 * LRU_WIDTH), D_MODEL ** -0.5),
        'a_conv_w': nrm(ks[14], (N_A, CONV_A, LRU_WIDTH), CONV_A ** -0.5),
        'a_conv_b': nrm(ks[15], (N_A, LRU_WIDTH), 0.01),
        'a_gate_a_w': nrm(ks[16], (N_A, N_LRU_BLOCKS, LRU_BLOCK, LRU_BLOCK), LRU_BLOCK ** -0.5),
        'a_gate_a_b': nrm(ks[17], (N_A, LRU_WIDTH), 0.01),
        'a_gate_x_w': nrm(ks[18], (N_A, N_LRU_BLOCKS, LRU_BLOCK, LRU_BLOCK), LRU_BLOCK ** -0.5),
        'a_gate_x_b': nrm(ks[19], (N_A, LRU_WIDTH), 0.01),
        'a_lambda': jnp.log(u) - jnp.log1p(-u),
        'a_w_out': nrm(ks[20], (N_A, LRU_WIDTH, D_MODEL), LRU_WIDTH ** -0.5),
        'kv_norm': 1.0 + nrm(ks[21], (D_MODEL,), 0.01),
        'w_kv': nrm(ks[22], (D_MODEL, 2 * Q_WIDTH), D_MODEL ** -0.5),
        'b_w_q': nrm(ks[23], (N_B, D_MODEL, Q_WIDTH), D_MODEL ** -0.5),
        'b_w_o': nrm(ks[24], (N_B, ATT_OUT, D_MODEL), ATT_OUT ** -0.5),
        'norm_ffn': 1.0 + nrm(ks[25], (DEPTH, D_MODEL), 0.01),
        'ffn_w_up': nrm(ks[26], (DEPTH, D_MODEL, 2 * D_FF), D_MODEL ** -0.5),
        'ffn_conv_w': nrm(ks[27], (DEPTH, CONV_F, D_FF), CONV_F ** -0.5),
        'ffn_conv_b': nrm(ks[28], (DEPTH, D_FF), 0.01),
        'ffn_w_down': nrm(ks[29], (DEPTH, D_FF, D_MODEL), D_FF ** -0.5),
        'final_norm': 1.0 + nrm(ks[30], (D_MODEL,), 0.01),
    }


def reference(x_prompt, x_sample, state_lru_h, state_conv_a, state_ffn_conv,
              cache_k0, cache_v0, cache_k1, cache_v1, cache_k2, cache_v2,
              norm_mix, a_w_in, a_conv_w, a_conv_b, a_gate_a_w, a_gate_a_b, a_gate_x_w, a_gate_x_b,
              a_lambda, a_w_out, kv_norm, w_kv, b_w_q, b_w_o,
              norm_ffn, ffn_w_up, ffn_conv_w, ffn_conv_b, ffn_w_down, final_norm):
    dt = x_prompt.dtype
    zero_h = jnp.zeros((N_A, BATCH, LRU_WIDTH), dt)
    zero_ca = jnp.zeros((N_A, BATCH, CONV_A - 1, LRU_WIDTH), dt)
    zero_cf = jnp.zeros((DEPTH, BATCH, CONV_F - 1, D_FF), dt)
    y_prompt, p_lru_h, p_conv_a, p_ffn, p_kv = trunk(
        x_prompt, zero_h, zero_ca, zero_cf, None,
        norm_mix, a_w_in, a_conv_w, a_conv_b, a_gate_a_w, a_gate_a_b, a_gate_x_w, a_gate_x_b,
        a_lambda, a_w_out, kv_norm, w_kv, b_w_q, b_w_o,
        norm_ffn, ffn_w_up, ffn_conv_w, ffn_conv_b, ffn_w_down, final_norm)
    y_sample, s_lru_h, s_conv_a, s_ffn, s_kv = trunk(
        x_sample, state_lru_h, state_conv_a, state_ffn_conv,
        (cache_k0, cache_v0, cache_k1, cache_v1, cache_k2, cache_v2),
        norm_mix, a_w_in, a_conv_w, a_conv_b, a_gate_a_w, a_gate_a_b, a_gate_x_w, a_gate_x_b,
        a_lambda, a_w_out, kv_norm, w_kv, b_w_q, b_w_o,
        norm_ffn, ffn_w_up, ffn_conv_w, ffn_conv_b, ffn_w_down, final_norm)
    p_k0, p_v0, p_k1, p_v1, p_k2, p_v2 = p_kv
    s_k0, s_v0, s_k1, s_v1, s_k2, s_v2 = s_kv
    return (y_prompt, y_sample, p_lru_h, s_lru_h, p_conv_a, s_conv_a, p_ffn, s_ffn,
            p_k0, p_v0, s_k0, s_v0, p_k1, p_v1, s_k1, s_v1, p_k2, p_v2, s_k2, s_v2)
```

```python
import functools

import jax
import jax.numpy as jnp
from jax import lax
from jax.experimental import pallas as pl
from jax.experimental.pallas import tpu as pltpu

F32 = jnp.float32
BF16 = jnp.bfloat16

EPS = 1e-6
LRU_C = 8.0
N_LRU_BLOCKS = 4
HEAD_DIM = 64
HEADS = 8
GROUPS = ((128, 1), (512, 4), (2048, 16))
N_GROUPS = len(GROUPS)
ATT_W = HEADS * HEAD_DIM
ATT_BLK = 128
CARRY_ROWS = 8

TQ_MIXER = 256
TM_FFN = 512
FC_FFN = 512
TM_TOKEN = 512
SAMPLE_BB = 8
VMEM_LIMIT = 52 * 1024 * 1024


def _rmsnorm(x, g):
    return x * lax.rsqrt(jnp.mean(x * x, axis=-1, keepdims=True) + EPS) * g


def _bdot(a, w):
    return jnp.dot(a.astype(BF16), w, preferred_element_type=F32)


def _shift_rows(cur, prev, s):
    rolled = pltpu.roll(cur, s, axis=0)
    prev_rolled = pltpu.roll(prev, s, axis=0)
    row = lax.broadcasted_iota(jnp.int32, prev.shape, 0)
    head = jnp.where(row < s, prev_rolled, rolled[:CARRY_ROWS])
    return jnp.concatenate([head, rolled[CARRY_ROWS:]], axis=0)


def _causal_conv_rows(cur, prev, w_ref, b):
    k = w_ref.shape[0]
    y = b + w_ref[k - 1:k, :] * cur
    for j in range(k - 1):
        y = y + w_ref[j:j + 1, :] * _shift_rows(cur, prev, k - 1 - j)
    return y


def _log_sigmoid(x):
    return jnp.minimum(x, 0.0) - jnp.log1p(jnp.exp(-jnp.abs(x)))


def _expm1(x):
    u = jnp.exp(x)
    um1 = u - 1.0
    tiny = um1 == 0.0
    ratio = x / jnp.where(tiny, 1.0, jnp.log(u))
    return jnp.where(tiny, x, jnp.where(um1 == -1.0, -1.0, um1 * ratio))


def _lru_coeffs(xc, wa_ref, ba, wx_ref, bx, lam):
    xb = xc.astype(BF16)
    blk = xc.shape[-1] // N_LRU_BLOCKS
    ra, ri = [], []
    for n in range(N_LRU_BLOCKS):
        xs = xb[:, n * blk:(n + 1) * blk]
        ra.append(jnp.dot(xs, wa_ref[n], preferred_element_type=F32))
        ri.append(jnp.dot(xs, wx_ref[n], preferred_element_type=F32))
    r = jax.nn.sigmoid(jnp.concatenate(ra, axis=-1) + ba)
    i = jax.nn.sigmoid(jnp.concatenate(ri, axis=-1) + bx)
    log_a = LRU_C * r * _log_sigmoid(lam)
    a = jnp.exp(log_a)
    bt = jnp.sqrt(-_expm1(2.0 * log_a)) * (i * xc)
    return a, bt


def _scan_rows(a, b):
    n = a.shape[0]
    row = lax.broadcasted_iota(jnp.int32, a.shape, 0)
    s = 1
    while s < n:
        keep = row >= s
        a_sh = jnp.where(keep, pltpu.roll(a, s, axis=0), 1.0)
        b_sh = jnp.where(keep, pltpu.roll(b, s, axis=0), 0.0)
        b = a * b_sh + b
        a = a * a_sh
        s *= 2
    return a, b


def _mixer_prompt_kernel(x_ref, g_ref, win_ref, cw_ref, cb_ref, wa_ref, ba_ref,
                         wx_ref, bx_ref, lam_ref, wout_ref,
                         y_ref, hlast_ref, cbuf_ref, ucarry, hcarry):
    t = pl.program_id(1)
    tq, d = x_ref.shape

    @pl.when(t == 0)
    def _():
        ucarry[...] = jnp.zeros_like(ucarry)
        hcarry[...] = jnp.zeros_like(hcarry)

    x = x_ref[...]
    u = _bdot(_rmsnorm(x, g_ref[...]), win_ref[...])
    gate = jax.nn.gelu(u[:, :d])
    ur = u[:, d:]
    xc = _causal_conv_rows(ur, ucarry[...], cw_ref, cb_ref[...])
    ucarry[...] = ur[tq - CARRY_ROWS:, :]
    a, bt = _lru_coeffs(xc, wa_ref, ba_ref[...], wx_ref, bx_ref[...], lam_ref[...])
    a_cum, b_cum = _scan_rows(a, bt)
    h = a_cum * hcarry[...] + b_cum
    hcarry[...] = h[tq - 1:tq, :]
    y_ref[...] = x + _bdot(h * gate, wout_ref[...])

    @pl.when(t == pl.num_programs(1) - 1)
    def _():
        hlast_ref[...] = h[tq - 1:tq, :]
        k = cw_ref.shape[0]
        cbuf_ref[...] = ur[tq - (k - 1):, :]


def _mixer_sample_kernel(x_ref, h0_ref, buf_ref, g_ref, win_ref, cw_ref, cb_ref,
                         wa_ref, ba_ref, wx_ref, bx_ref, lam_ref, wout_ref,
                         y_ref, hnew_ref, ur_ref):
    d = x_ref.shape[-1]
    k = cw_ref.shape[0]
    x = x_ref[...]
    u = _bdot(_rmsnorm(x, g_ref[...]), win_ref[...])
    gate = jax.nn.gelu(u[:, :d])
    ur = u[:, d:]
    xc = cb_ref[...] + cw_ref[k - 1:k, :] * ur
    for j in range(k - 1):
        xc = xc + cw_ref[j:j + 1, :] * buf_ref[j]
    a, bt = _lru_coeffs(xc, wa_ref, ba_ref[...], wx_ref, bx_ref[...], lam_ref[...])
    h = a * h0_ref[...] + bt
    hnew_ref[...] = h
    ur_ref[...] = ur
    y_ref[...] = x + _bdot(h * gate, wout_ref[...])


def _full_spec(shape):
    zeros = (0,) * len(shape)
    return pl.BlockSpec(shape, lambda *_: zeros)


def _weight_spec(w):
    zeros = (0,) * w.ndim
    return pl.BlockSpec(w.shape, lambda *_: zeros, pipeline_mode=pl.Buffered(1))


def _weight_specs(weights):
    return [_weight_spec(w) for w in weights]


def _mixer_prompt(x, weights):
    b, t, d = x.shape
    tq = TQ_MIXER
    k = weights[2].shape[0]
    row = lambda bi, ti: (bi, ti, 0)
    per_seq = lambda bi, ti: (bi, 0, 0)
    return pl.pallas_call(
        _mixer_prompt_kernel,
        grid=(b, t // tq),
        in_specs=[pl.BlockSpec((None, tq, d), row)] + _weight_specs(weights),
        out_specs=[pl.BlockSpec((None, tq, d), row),
                   pl.BlockSpec((None, 1, d), per_seq),
                   pl.BlockSpec((None, k - 1, d), per_seq)],
        out_shape=[jax.ShapeDtypeStruct((b, t, d), F32),
                   jax.ShapeDtypeStruct((b, 1, d), F32),
                   jax.ShapeDtypeStruct((b, k - 1, d), F32)],
        scratch_shapes=[pltpu.VMEM((CARRY_ROWS, d), F32), pltpu.VMEM((1, d), F32)],
        compiler_params=pltpu.CompilerParams(
            dimension_semantics=("arbitrary", "arbitrary"),
            vmem_limit_bytes=VMEM_LIMIT),
        name="mixer_prompt",
    )(x, *weights)


def _mixer_sample(x, h0, buf, weights):
    n, d = x.shape
    k = weights[2].shape[0]
    return pl.pallas_call(
        _mixer_sample_kernel,
        grid=(1,),
        in_specs=[_full_spec((n, d)), _full_spec((n, d)), _full_spec((k - 1, n, d))]
        + _weight_specs(weights),
        out_specs=[_full_spec((n, d))] * 3,
        out_shape=[jax.ShapeDtypeStruct((n, d), F32)] * 3,
        compiler_params=pltpu.CompilerParams(
            dimension_semantics=("arbitrary",), vmem_limit_bytes=VMEM_LIMIT),
        name="mixer_sample",
    )(x, h0, buf, *weights)


def _ffn_chunks(xb, x, wup_ref, cw_ref, cb_ref, wdown_ref, conv_gate):
    f = wdown_ref.shape[0]
    acc = x
    for lo in range(0, f, FC_FFN):
        hi = lo + FC_FFN
        ug = jnp.dot(xb, wup_ref[:, lo:hi], preferred_element_type=F32)
        uv = jnp.dot(xb, wup_ref[:, f + lo:f + hi], preferred_element_type=F32)
        gc = conv_gate(ug, lo, hi)
        acc = acc + _bdot(jax.nn.gelu(gc) * uv, wdown_ref[lo:hi, :])
    return acc


def _ffn_prompt_kernel(*refs, final_norm):
    if final_norm:
        (x_ref, g_ref, wup_ref, cw_ref, cb_ref, wdown_ref, fn_ref,
         y_ref, fbuf_ref, gcarry) = refs
    else:
        (x_ref, g_ref, wup_ref, cw_ref, cb_ref, wdown_ref,
         y_ref, fbuf_ref, gcarry) = refs
    t = pl.program_id(1)
    tm = x_ref.shape[0]
    k = cw_ref.shape[0]

    @pl.when(t == 0)
    def _():
        gcarry[...] = jnp.zeros_like(gcarry)

    def conv_gate(ug, lo, hi):
        gc = _causal_conv_rows(ug, gcarry[:, lo:hi], cw_ref.at[:, lo:hi], cb_ref[:, lo:hi])
        gcarry[:, lo:hi] = ug[tm - CARRY_ROWS:, :]
        return gc

    x = x_ref[...]
    xb = _rmsnorm(x, g_ref[...]).astype(BF16)
    y = _ffn_chunks(xb, x, wup_ref, cw_ref, cb_ref, wdown_ref, conv_gate)
    if final_norm:
        y = _rmsnorm(y, fn_ref[...])
    y_ref[...] = y

    @pl.when(t == pl.num_programs(1) - 1)
    def _():
        fbuf_ref[...] = gcarry[CARRY_ROWS - (k - 1):, :]


def _ffn_sample_kernel(*refs, final_norm):
    if final_norm:
        (x_ref, buf_ref, g_ref, wup_ref, cw_ref, cb_ref, wdown_ref, fn_ref,
         y_ref, ug_ref) = refs
    else:
        (x_ref, buf_ref, g_ref, wup_ref, cw_ref, cb_ref, wdown_ref,
         y_ref, ug_ref) = refs
    k = cw_ref.shape[0]

    def conv_gate(ug, lo, hi):
        ug_ref[:, lo:hi] = ug
        gc = cb_ref[:, lo:hi] + cw_ref[k - 1:k, lo:hi] * ug
        for j in range(k - 1):
            gc = gc + cw_ref[j:j + 1, lo:hi] * buf_ref[j, :, lo:hi]
        return gc

    x = x_ref[...]
    xb = _rmsnorm(x, g_ref[...]).astype(BF16)
    y = _ffn_chunks(xb, x, wup_ref, cw_ref, cb_ref, wdown_ref, conv_gate)
    if final_norm:
        y = _rmsnorm(y, fn_ref[...])
    y_ref[...] = y


def _ffn_prompt(x, weights, final_norm):
    b, t, d = x.shape
    f = weights[4].shape[0]
    k = weights[2].shape[0]
    tm = TM_FFN
    row = lambda bi, ti: (bi, ti, 0)
    per_seq = lambda bi, ti: (bi, 0, 0)
    return pl.pallas_call(
        functools.partial(_ffn_prompt_kernel, final_norm=final_norm),
        grid=(b, t // tm),
        in_specs=[pl.BlockSpec((None, tm, d), row)] + _weight_specs(weights),
        out_specs=[pl.BlockSpec((None, tm, d), row),
                   pl.BlockSpec((None, k - 1, f), per_seq)],
        out_shape=[jax.ShapeDtypeStruct((b, t, d), F32),
                   jax.ShapeDtypeStruct((b, k - 1, f), F32)],
        scratch_shapes=[pltpu.VMEM((CARRY_ROWS, f), F32)],
        compiler_params=pltpu.CompilerParams(
            dimension_semantics=("arbitrary", "arbitrary"),
            vmem_limit_bytes=VMEM_LIMIT),
        name="ffn_prompt_final" if final_norm else "ffn_prompt",
    )(x, *weights)


def _ffn_sample(x, buf, weights, final_norm):
    n, d = x.shape
    f = weights[4].shape[0]
    k = weights[2].shape[0]
    return pl.pallas_call(
        functools.partial(_ffn_sample_kernel, final_norm=final_norm),
        grid=(1,),
        in_specs=[_full_spec((n, d)), _full_spec((k - 1, n, f))]
        + _weight_specs(weights),
        out_specs=[_full_spec((n, d)), _full_spec((n, f))],
        out_shape=[jax.ShapeDtypeStruct((n, d), F32), jax.ShapeDtypeStruct((n, f), F32)],
        compiler_params=pltpu.CompilerParams(
            dimension_semantics=("arbitrary",), vmem_limit_bytes=VMEM_LIMIT),
        name="ffn_sample_final" if final_norm else "ffn_sample",
    )(x, buf, *weights)


def _qkv_kernel(x_ref, gq_ref, gkv_ref, wq_ref, wkv_ref, q_ref, kv_ref):
    x = x_ref[...]
    inv = lax.rsqrt(jnp.mean(x * x, axis=-1, keepdims=True) + EPS)
    y = x * inv
    q_ref[...] = _bdot(y * gq_ref[...], wq_ref[...])
    kv_ref[...] = _bdot(y * gkv_ref[...], wkv_ref[...])


def _qkv(x, gq, gkv, wq, wkv, tm):
    m, d = x.shape
    nq, nkv = wq.shape[1], wkv.shape[1]
    row = lambda i: (i, 0)
    return pl.pallas_call(
        _qkv_kernel,
        grid=(m // tm,),
        in_specs=[pl.BlockSpec((tm, d), row)] + _weight_specs((gq, gkv, wq, wkv)),
        out_specs=[pl.BlockSpec((tm, nq), row), pl.BlockSpec((tm, nkv), row)],
        out_shape=[jax.ShapeDtypeStruct((m, nq), F32), jax.ShapeDtypeStruct((m, nkv), F32)],
        compiler_params=pltpu.CompilerParams(
            dimension_semantics=("arbitrary",), vmem_limit_bytes=VMEM_LIMIT),
        name="qkv_proj",
    )(x, gq, gkv, wq, wkv)


def _attn_prompt_kernel(q_ref, kp_ref, kc_ref, vp_ref, vc_ref, o_ref, lse_ref):
    i = pl.program_id(2)
    blk = q_ref.shape[0]
    qi = lax.broadcasted_iota(jnp.int32, (blk, blk), 0)
    ki = lax.broadcasted_iota(jnp.int32, (blk, blk), 1)
    mask_prev = jnp.logical_and(ki >= qi, i > 0)
    mask_cur = ki <= qi
    contract_last = (((1,), (1,)), ((), ()))
    scale = HEAD_DIM ** -0.5
    for h in range(HEADS):
        cols = slice(h * HEAD_DIM, (h + 1) * HEAD_DIM)
        qh = (q_ref[:, cols] * scale).astype(BF16)
        sp = lax.dot_general(qh, kp_ref[:, cols].astype(BF16), contract_last,
                             preferred_element_type=F32)
        sc = lax.dot_general(qh, kc_ref[:, cols].astype(BF16), contract_last,
                             preferred_element_type=F32)
        sp = jnp.where(mask_prev, sp, -jnp.inf)
        sc = jnp.where(mask_cur, sc, -jnp.inf)
        m = jnp.maximum(jnp.max(sp, axis=-1, keepdims=True),
                        jnp.max(sc, axis=-1, keepdims=True))
        pp = jnp.exp(sp - m)
        pc = jnp.exp(sc - m)
        den = jnp.sum(pp, axis=-1, keepdims=True) + jnp.sum(pc, axis=-1, keepdims=True)
        o = _bdot(pp, vp_ref[:, cols].astype(BF16)) + _bdot(pc, vc_ref[:, cols].astype(BF16))
        o_ref[:, cols] = o / den
        lse_ref[:, cols] = jnp.broadcast_to(m + jnp.log(den), (blk, HEAD_DIM))


def _attn_prompt_group(q, kv, g, dil):
    b, t, _ = q.shape
    n = t // dil
    nb = n // ATT_BLK
    qv = q.reshape(b, n, dil * q.shape[-1])
    kvv = kv.reshape(b, n, dil * kv.shape[-1])
    blk = (None, ATT_BLK, ATT_W)
    q_spec = pl.BlockSpec(blk, lambda bi, r, i: (bi, i, r * N_GROUPS + g))
    kc_spec = pl.BlockSpec(blk, lambda bi, r, i: (bi, i, r * 2 * N_GROUPS + g))
    kp_spec = pl.BlockSpec(blk, lambda bi, r, i: (bi, jnp.maximum(i - 1, 0), r * 2 * N_GROUPS + g))
    vc_spec = pl.BlockSpec(blk, lambda bi, r, i: (bi, i, r * 2 * N_GROUPS + N_GROUPS + g))
    vp_spec = pl.BlockSpec(
        blk, lambda bi, r, i: (bi, jnp.maximum(i - 1, 0), r * 2 * N_GROUPS + N_GROUPS + g))
    o_spec = pl.BlockSpec(blk, lambda bi, r, i: (bi, i, r))
    o, lse = pl.pallas_call(
        _attn_prompt_kernel,
        grid=(b, dil, nb),
        in_specs=[q_spec, kp_spec, kc_spec, vp_spec, vc_spec],
        out_specs=[o_spec, o_spec],
        out_shape=[jax.ShapeDtypeStruct((b, n, dil * ATT_W), F32)] * 2,
        compiler_params=pltpu.CompilerParams(
            dimension_semantics=("arbitrary", "arbitrary", "arbitrary"),
            vmem_limit_bytes=VMEM_LIMIT),
        name=f"attn_prompt_g{g}",
    )(qv, kvv, kvv, kvv, kvv)
    return o.reshape(b, t, ATT_W), lse.reshape(b, t, ATT_W)


def _attn_sample_kernel(q_ref, kv_ref, seg_ref, k0_ref, v0_ref, k1_ref, v1_ref,
                        k2_ref, v2_ref, o_ref, lse_ref):
    bb = q_ref.shape[0]
    rows = k0_ref.shape[1]
    seg = seg_ref[...]
    scale = HEAD_DIM ** -0.5
    caches = ((k0_ref, v0_ref), (k1_ref, v1_ref), (k2_ref, v2_ref))

    def head_sums(prod):
        return jnp.dot(prod, seg, preferred_element_type=F32,
                       precision=lax.Precision.HIGHEST)

    for g, (k_ref, v_ref) in enumerate(caches):
        q = q_ref[:, :, g * ATT_W:(g + 1) * ATT_W]
        k_new = kv_ref[:, :, g * ATT_W:(g + 1) * ATT_W]
        v_new = kv_ref[:, :, (N_GROUPS + g) * ATT_W:(N_GROUPS + g + 1) * ATT_W]
        s_buf = head_sums((k_ref[...] * q).reshape(bb * rows, ATT_W)) * scale
        s_buf = s_buf.reshape(bb, rows, ATT_W)
        s_new = head_sums((k_new * q).reshape(bb, ATT_W)) * scale
        s_new = s_new.reshape(bb, 1, ATT_W)
        m = jnp.maximum(jnp.max(s_buf, axis=1, keepdims=True), s_new)
        p_buf = jnp.exp(s_buf - m)
        p_new = jnp.exp(s_new - m)
        den = jnp.sum(p_buf, axis=1, keepdims=True) + p_new
        o = (jnp.sum(p_buf * v_ref[...], axis=1, keepdims=True) + p_new * v_new) / den
        o_ref[:, :, g * ATT_W:(g + 1) * ATT_W] = o
        lse_ref[:, :, g * ATT_W:(g + 1) * ATT_W] = m + jnp.log(den)


def _attn_sample(q, kv, caches):
    n = q.shape[0]
    bb = SAMPLE_BB
    lane = jnp.arange(ATT_W) // HEAD_DIM
    seg = (lane[:, None] == lane[None, :]).astype(F32)
    cache_specs, cache_views = [], []
    for g, (win, dil) in enumerate(GROUPS):
        for c in caches[2 * g:2 * g + 2]:
            cache_views.append(c.reshape(n, win // dil, dil * ATT_W))
            cache_specs.append(pl.BlockSpec((bb, win // dil, ATT_W), lambda i: (i, 0, 0)))
    row = lambda i: (i, 0, 0)
    return pl.pallas_call(
        _attn_sample_kernel,
        grid=(n // bb,),
        in_specs=[pl.BlockSpec((bb, 1, q.shape[-1]), row),
                  pl.BlockSpec((bb, 1, kv.shape[-1]), row),
                  _full_spec((ATT_W, ATT_W))] + cache_specs,
        out_specs=[pl.BlockSpec((bb, 1, q.shape[-1]), row)] * 2,
        out_shape=[jax.ShapeDtypeStruct(q.shape, F32)] * 2,
        compiler_params=pltpu.CompilerParams(
            dimension_semantics=("arbitrary",), vmem_limit_bytes=VMEM_LIMIT),
        name="attn_sample",
    )(q, kv, seg, *cache_views)


def _merge_kernel(x_ref, o0_ref, o1_ref, o2_ref, l0_ref, l1_ref, l2_ref, wo_ref, y_ref):
    l0, l1, l2 = l0_ref[...], l1_ref[...], l2_ref[...]
    m = jnp.maximum(jnp.maximum(l0, l1), l2)
    e0, e1, e2 = jnp.exp(l0 - m), jnp.exp(l1 - m), jnp.exp(l2 - m)
    den = e0 + e1 + e2
    o = (e0 / den) * o0_ref[...] + (e1 / den) * o1_ref[...] + (e2 / den) * o2_ref[...]
    y_ref[...] = x_ref[...] + _bdot(o, wo_ref[...])


def _merge(x, outs, lses, wo, tm):
    m, d = x.shape
    w = wo.shape[0]
    row = lambda i: (i, 0)
    part = pl.BlockSpec((tm, w), row)
    return pl.pallas_call(
        _merge_kernel,
        grid=(m // tm,),
        in_specs=[pl.BlockSpec((tm, d), row)] + [part] * 6 + [_weight_spec(wo)],
        out_specs=pl.BlockSpec((tm, d), row),
        out_shape=jax.ShapeDtypeStruct((m, d), F32),
        compiler_params=pltpu.CompilerParams(
            dimension_semantics=("arbitrary",), vmem_limit_bytes=VMEM_LIMIT),
        name="attn_merge",
    )(x, *outs, *lses, wo)


def kernel(x_prompt, x_sample, state_lru_h, state_conv_a, state_ffn_conv, cache_k0, cache_v0, cache_k1, cache_v1, cache_k2, cache_v2, norm_mix, a_w_in, a_conv_w, a_conv_b, a_gate_a_w, a_gate_a_b, a_gate_x_w, a_gate_x_b, a_lambda, a_w_out, kv_norm, w_kv, b_w_q, b_w_o, norm_ffn, ffn_w_up, ffn_conv_w, ffn_conv_b, ffn_w_down, final_norm):
    b, t, d = x_prompt.shape
    n = x_sample.shape[0]
    assert x_sample.shape[1] == 1 and norm_mix.shape[0] == 2
    caches = (cache_k0, cache_v0, cache_k1, cache_v1, cache_k2, cache_v2)
    row = lambda v: v.reshape(1, -1)

    mixer_w = (row(norm_mix[0]), a_w_in[0].astype(BF16), a_conv_w[0], row(a_conv_b[0]),
               a_gate_a_w[0].astype(BF16), row(a_gate_a_b[0]),
               a_gate_x_w[0].astype(BF16), row(a_gate_x_b[0]),
               row(a_lambda[0]), a_w_out[0].astype(BF16))
    ffn_w = [(row(norm_ffn[l]), ffn_w_up[l].astype(BF16), ffn_conv_w[l],
              row(ffn_conv_b[l]), ffn_w_down[l].astype(BF16)) for l in range(2)]
    ffn_w[1] = ffn_w[1] + (row(final_norm),)
    gq, gkv = row(norm_mix[1]), row(kv_norm)
    wq, wkv, wo = b_w_q[0].astype(BF16), w_kv.astype(BF16), b_w_o[0].astype(BF16)

    h, p_h, p_ca = _mixer_prompt(x_prompt, mixer_w)
    h, p_f0 = _ffn_prompt(h, ffn_w[0], final_norm=False)
    hf = h.reshape(b * t, d)
    q, kv = _qkv(hf, gq, gkv, wq, wkv, TM_TOKEN)
    q3, kv3 = q.reshape(b, t, -1), kv.reshape(b, t, -1)
    outs, lses = [], []
    for g, (_, dil) in enumerate(GROUPS):
        o, lse = _attn_prompt_group(q3, kv3, g, dil)
        outs.append(o.reshape(b * t, ATT_W))
        lses.append(lse.reshape(b * t, ATT_W))
    hf = _merge(hf, outs, lses, wo, TM_TOKEN)
    y_prompt, p_f1 = _ffn_prompt(hf.reshape(b, t, d), ffn_w[1], final_norm=True)
    p_kv = []
    for g, (win, _) in enumerate(GROUPS):
        keep = min(win, t)
        for off in (g * ATT_W, (N_GROUPS + g) * ATT_W):
            p_kv.append(kv3[:, t - keep:, off:off + ATT_W].reshape(b, keep, HEADS, HEAD_DIM))

    xs = x_sample.reshape(n, d)
    conv_buf = jnp.swapaxes(state_conv_a[0], 0, 1)
    hs, s_h, s_ur = _mixer_sample(xs, state_lru_h[0], conv_buf, mixer_w)
    s_ca = jnp.concatenate([state_conv_a[0][:, 1:], s_ur[:, None]], axis=1)
    hs, s_ug0 = _ffn_sample(hs, jnp.swapaxes(state_ffn_conv[0], 0, 1), ffn_w[0], final_norm=False)
    qs, kvs = _qkv(hs, gq, gkv, wq, wkv, n)
    os_, lses_s = _attn_sample(qs.reshape(n, 1, -1), kvs.reshape(n, 1, -1), caches)
    os_, lses_s = os_.reshape(n, -1), lses_s.reshape(n, -1)
    hs = _merge(hs,
                [os_[:, g * ATT_W:(g + 1) * ATT_W] for g in range(N_GROUPS)],
                [lses_s[:, g * ATT_W:(g + 1) * ATT_W] for g in range(N_GROUPS)], wo, n)
    y_sample, s_ug1 = _ffn_sample(hs, jnp.swapaxes(state_ffn_conv[1], 0, 1), ffn_w[1], final_norm=True)
    s_ffn = jnp.stack([
        jnp.concatenate([state_ffn_conv[l][:, 1:], ug[:, None]], axis=1)
        for l, ug in enumerate((s_ug0, s_ug1))])
    s_kv = []
    for g, (win, _) in enumerate(GROUPS):
        for c, off in zip(caches[2 * g:2 * g + 2], (g * ATT_W, (N_GROUPS + g) * ATT_W)):
            new = kvs[:, off:off + ATT_W].reshape(n, 1, HEADS, HEAD_DIM)
            ext = jnp.concatenate([c, new], axis=1)
            s_kv.append(ext[:, ext.shape[1] - min(win, ext.shape[1]):])

    return (y_prompt, y_sample.reshape(n, 1, d),
            p_h.reshape(1, b, d), s_h.reshape(1, n, d),
            p_ca.reshape(1, b, -1, d), s_ca.reshape(1, n, -1, d),
            jnp.stack([p_f0, p_f1]), s_ffn,
            p_kv[0], p_kv[1], s_kv[0], s_kv[1],
            p_kv[2], p_kv[3], s_kv[2], s_kv[3],
            p_kv[4], p_kv[5], s_kv[4], s_kv[5])
```

```python
import functools

import jax
import jax.numpy as jnp
from jax import lax
from jax.experimental import pallas as pl
from jax.experimental.pallas import tpu as pltpu

F32 = jnp.float32
BF16 = jnp.bfloat16

EPS = 1e-6
LRU_C = 8.0
N_LRU_BLOCKS = 4
HEAD_DIM = 64
HEADS = 8
GROUPS = ((128, 1), (512, 4), (2048, 16))
N_GROUPS = len(GROUPS)
ATT_W = HEADS * HEAD_DIM
ATT_BLK = 128
CARRY_ROWS = 8

TQ_MIXER = 256
TM_FFN = 512
FC_FFN = 512
TM_TOKEN = 512
VMEM_LIMIT = 52 * 1024 * 1024


def _rmsnorm(x, g):
    return x * lax.rsqrt(jnp.mean(x * x, axis=-1, keepdims=True) + EPS) * g


def _bdot(a, w):
    return jnp.dot(a.astype(BF16), w, preferred_element_type=F32)


def _shift_rows(cur, prev, s):
    rolled = pltpu.roll(cur, s, axis=0)
    prev_rolled = pltpu.roll(prev, s, axis=0)
    row = lax.broadcasted_iota(jnp.int32, prev.shape, 0)
    head = jnp.where(row < s, prev_rolled, rolled[:CARRY_ROWS])
    return jnp.concatenate([head, rolled[CARRY_ROWS:]], axis=0)


def _causal_conv_rows(cur, prev, w_ref, b):
    k = w_ref.shape[0]
    y = b + w_ref[k - 1:k, :] * cur
    for j in range(k - 1):
        y = y + w_ref[j:j + 1, :] * _shift_rows(cur, prev, k - 1 - j)
    return y


def _log_sigmoid(x):
    return jnp.minimum(x, 0.0) - jnp.log1p(jnp.exp(-jnp.abs(x)))


def _expm1(x):
    u = jnp.exp(x)
    um1 = u - 1.0
    tiny = um1 == 0.0
    ratio = x / jnp.where(tiny, 1.0, jnp.log(u))
    return jnp.where(tiny, x, jnp.where(um1 == -1.0, -1.0, um1 * ratio))


def _lru_coeffs(xc, wa_ref, ba, wx_ref, bx, lam):
    xb = xc.astype(BF16)
    blk = xc.shape[-1] // N_LRU_BLOCKS
    ra, ri = [], []
    for n in range(N_LRU_BLOCKS):
        xs = xb[:, n * blk:(n + 1) * blk]
        ra.append(jnp.dot(xs, wa_ref[n], preferred_element_type=F32))
        ri.append(jnp.dot(xs, wx_ref[n], preferred_element_type=F32))
    r = jax.nn.sigmoid(jnp.concatenate(ra, axis=-1) + ba)
    i = jax.nn.sigmoid(jnp.concatenate(ri, axis=-1) + bx)
    log_a = LRU_C * r * _log_sigmoid(lam)
    a = jnp.exp(log_a)
    bt = jnp.sqrt(-_expm1(2.0 * log_a)) * (i * xc)
    return a, bt


def _scan_rows(a, b):
    n = a.shape[0]
    row = lax.broadcasted_iota(jnp.int32, a.shape, 0)
    s = 1
    while s < n:
        keep = row >= s
        a_sh = jnp.where(keep, pltpu.roll(a, s, axis=0), 1.0)
        b_sh = jnp.where(keep, pltpu.roll(b, s, axis=0), 0.0)
        b = a * b_sh + b
        a = a * a_sh
        s *= 2
    return a, b


def _mixer_prompt_kernel(x_ref, g_ref, win_ref, cw_ref, cb_ref, wa_ref, ba_ref,
                         wx_ref, bx_ref, lam_ref, wout_ref,
                         y_ref, hlast_ref, cbuf_ref, ucarry, hcarry):
    t = pl.program_id(1)
    tq, d = x_ref.shape

    @pl.when(t == 0)
    def _():
        ucarry[...] = jnp.zeros_like(ucarry)
        hcarry[...] = jnp.zeros_like(hcarry)

    x = x_ref[...]
    u = _bdot(_rmsnorm(x, g_ref[...]), win_ref[...])
    gate = jax.nn.gelu(u[:, :d])
    ur = u[:, d:]
    xc = _causal_conv_rows(ur, ucarry[...], cw_ref, cb_ref[...])
    ucarry[...] = ur[tq - CARRY_ROWS:, :]
    a, bt = _lru_coeffs(xc, wa_ref, ba_ref[...], wx_ref, bx_ref[...], lam_ref[...])
    a_cum, b_cum = _scan_rows(a, bt)
    h = a_cum * hcarry[...] + b_cum
    hcarry[...] = h[tq - 1:tq, :]
    y_ref[...] = x + _bdot(h * gate, wout_ref[...])

    @pl.when(t == pl.num_programs(1) - 1)
    def _():
        hlast_ref[...] = h[tq - 1:tq, :]
        k = cw_ref.shape[0]
        cbuf_ref[...] = ur[tq - (k - 1):, :]


def _mixer_sample_kernel(x_ref, h0_ref, buf_ref, g_ref, win_ref, cw_ref, cb_ref,
                         wa_ref, ba_ref, wx_ref, bx_ref, lam_ref, wout_ref,
                         y_ref, hnew_ref, ur_ref):
    d = x_ref.shape[-1]
    k = cw_ref.shape[0]
    x = x_ref[...]
    u = _bdot(_rmsnorm(x, g_ref[...]), win_ref[...])
    gate = jax.nn.gelu(u[:, :d])
    ur = u[:, d:]
    xc = cb_ref[...] + cw_ref[k - 1:k, :] * ur
    for j in range(k - 1):
        xc = xc + cw_ref[j:j + 1, :] * buf_ref[j]
    a, bt = _lru_coeffs(xc, wa_ref, ba_ref[...], wx_ref, bx_ref[...], lam_ref[...])
    h = a * h0_ref[...] + bt
    hnew_ref[...] = h
    ur_ref[...] = ur
    y_ref[...] = x + _bdot(h * gate, wout_ref[...])


def _full_spec(shape):
    zeros = (0,) * len(shape)
    return pl.BlockSpec(shape, lambda *_: zeros)


def _weight_spec(w):
    zeros = (0,) * w.ndim
    return pl.BlockSpec(w.shape, lambda *_: zeros, pipeline_mode=pl.Buffered(1))


def _weight_specs(weights):
    return [_weight_spec(w) for w in weights]


def _mixer_prompt(x, weights):
    b, t, d = x.shape
    tq = TQ_MIXER
    k = weights[2].shape[0]
    row = lambda bi, ti: (bi, ti, 0)
    per_seq = lambda bi, ti: (bi, 0, 0)
    return pl.pallas_call(
        _mixer_prompt_kernel,
        grid=(b, t // tq),
        in_specs=[pl.BlockSpec((None, tq, d), row)] + _weight_specs(weights),
        out_specs=[pl.BlockSpec((None, tq, d), row),
                   pl.BlockSpec((None, 1, d), per_seq),
                   pl.BlockSpec((None, k - 1, d), per_seq)],
        out_shape=[jax.ShapeDtypeStruct((b, t, d), F32),
                   jax.ShapeDtypeStruct((b, 1, d), F32),
                   jax.ShapeDtypeStruct((b, k - 1, d), F32)],
        scratch_shapes=[pltpu.VMEM((CARRY_ROWS, d), F32), pltpu.VMEM((1, d), F32)],
        compiler_params=pltpu.CompilerParams(
            dimension_semantics=("arbitrary", "arbitrary"),
            vmem_limit_bytes=VMEM_LIMIT),
        name="mixer_prompt",
    )(x, *weights)


def _mixer_sample(x, h0, buf, weights):
    n, d = x.shape
    k = weights[2].shape[0]
    return pl.pallas_call(
        _mixer_sample_kernel,
        grid=(1,),
        in_specs=[_full_spec((n, d)), _full_spec((n, d)), _full_spec((k - 1, n, d))]
        + _weight_specs(weights),
        out_specs=[_full_spec((n, d))] * 3,
        out_shape=[jax.ShapeDtypeStruct((n, d), F32)] * 3,
        compiler_params=pltpu.CompilerParams(
            dimension_semantics=("arbitrary",), vmem_limit_bytes=VMEM_LIMIT),
        name="mixer_sample",
    )(x, h0, buf, *weights)


def _ffn_chunks(xb, x, wup_ref, cw_ref, cb_ref, wdown_ref, conv_gate):
    f = wdown_ref.shape[0]
    acc = x
    for lo in range(0, f, FC_FFN):
        hi = lo + FC_FFN
        ug = jnp.dot(xb, wup_ref[:, lo:hi], preferred_element_type=F32)
        uv = jnp.dot(xb, wup_ref[:, f + lo:f + hi], preferred_element_type=F32)
        gc = conv_gate(ug, lo, hi)
        acc = acc + _bdot(jax.nn.gelu(gc) * uv, wdown_ref[lo:hi, :])
    return acc


def _ffn_prompt_kernel(*refs, final_norm):
    if final_norm:
        (x_ref, g_ref, wup_ref, cw_ref, cb_ref, wdown_ref, fn_ref,
         y_ref, fbuf_ref, gcarry) = refs
    else:
        (x_ref, g_ref, wup_ref, cw_ref, cb_ref, wdown_ref,
         y_ref, fbuf_ref, gcarry) = refs
    t = pl.program_id(1)
    tm = x_ref.shape[0]
    k = cw_ref.shape[0]

    @pl.when(t == 0)
    def _():
        gcarry[...] = jnp.zeros_like(gcarry)

    def conv_gate(ug, lo, hi):
        gc = _causal_conv_rows(ug, gcarry[:, lo:hi], cw_ref.at[:, lo:hi], cb_ref[:, lo:hi])
        gcarry[:, lo:hi] = ug[tm - CARRY_ROWS:, :]
        return gc

    x = x_ref[...]
    xb = _rmsnorm(x, g_ref[...]).astype(BF16)
    y = _ffn_chunks(xb, x, wup_ref, cw_ref, cb_ref, wdown_ref, conv_gate)
    if final_norm:
        y = _rmsnorm(y, fn_ref[...])
    y_ref[...] = y

    @pl.when(t == pl.num_programs(1) - 1)
    def _():
        fbuf_ref[...] = gcarry[CARRY_ROWS - (k - 1):, :]


def _ffn_sample_kernel(*refs, final_norm):
    if final_norm:
        (x_ref, buf_ref, g_ref, wup_ref, cw_ref, cb_ref, wdown_ref, fn_ref,
         y_ref, ug_ref) = refs
    else:
        (x_ref, buf_ref, g_ref, wup_ref, cw_ref, cb_ref, wdown_ref,
         y_ref, ug_ref) = refs
    k = cw_ref.shape[0]

    def conv_gate(ug, lo, hi):
        ug_ref[:, lo:hi] = ug
        gc = cb_ref[:, lo:hi] + cw_ref[k - 1:k, lo:hi] * ug
        for j in range(k - 1):
            gc = gc + cw_ref[j:j + 1, lo:hi] * buf_ref[j, :, lo:hi]
        return gc

    x = x_ref[...]
    xb = _rmsnorm(x, g_ref[...]).astype(BF16)
    y = _ffn_chunks(xb, x, wup_ref, cw_ref, cb_ref, wdown_ref, conv_gate)
    if final_norm:
        y = _rmsnorm(y, fn_ref[...])
    y_ref[...] = y


def _ffn_prompt(x, weights, final_norm):
    b, t, d = x.shape
    f = weights[4].shape[0]
    k = weights[2].shape[0]
    tm = TM_FFN
    row = lambda bi, ti: (bi, ti, 0)
    per_seq = lambda bi, ti: (bi, 0, 0)
    return pl.pallas_call(
        functools.partial(_ffn_prompt_kernel, final_norm=final_norm),
        grid=(b, t // tm),
        in_specs=[pl.BlockSpec((None, tm, d), row)] + _weight_specs(weights),
        out_specs=[pl.BlockSpec((None, tm, d), row),
                   pl.BlockSpec((None, k - 1, f), per_seq)],
        out_shape=[jax.ShapeDtypeStruct((b, t, d), F32),
                   jax.ShapeDtypeStruct((b, k - 1, f), F32)],
        scratch_shapes=[pltpu.VMEM((CARRY_ROWS, f), F32)],
        compiler_params=pltpu.CompilerParams(
            dimension_semantics=("arbitrary", "arbitrary"),
            vmem_limit_bytes=VMEM_LIMIT),
        name="ffn_prompt_final" if final_norm else "ffn_prompt",
    )(x, *weights)


def _ffn_sample(x, buf, weights, final_norm):
    n, d = x.shape
    f = weights[4].shape[0]
    k = weights[2].shape[0]
    return pl.pallas_call(
        functools.partial(_ffn_sample_kernel, final_norm=final_norm),
        grid=(1,),
        in_specs=[_full_spec((n, d)), _full_spec((k - 1, n, f))]
        + _weight_specs(weights),
        out_specs=[_full_spec((n, d)), _full_spec((n, f))],
        out_shape=[jax.ShapeDtypeStruct((n, d), F32), jax.ShapeDtypeStruct((n, f), F32)],
        compiler_params=pltpu.CompilerParams(
            dimension_semantics=("arbitrary",), vmem_limit_bytes=VMEM_LIMIT),
        name="ffn_sample_final" if final_norm else "ffn_sample",
    )(x, buf, *weights)


def _qkv_kernel(x_ref, gq_ref, gkv_ref, wq_ref, wkv_ref, q_ref, kv_ref):
    x = x_ref[...]
    inv = lax.rsqrt(jnp.mean(x * x, axis=-1, keepdims=True) + EPS)
    y = x * inv
    q_ref[...] = _bdot(y * gq_ref[...], wq_ref[...])
    kv_ref[...] = _bdot(y * gkv_ref[...], wkv_ref[...])


def _qkv(x, gq, gkv, wq, wkv, tm):
    m, d = x.shape
    nq, nkv = wq.shape[1], wkv.shape[1]
    row = lambda i: (i, 0)
    return pl.pallas_call(
        _qkv_kernel,
        grid=(m // tm,),
        in_specs=[pl.BlockSpec((tm, d), row)] + _weight_specs((gq, gkv, wq, wkv)),
        out_specs=[pl.BlockSpec((tm, nq), row), pl.BlockSpec((tm, nkv), row)],
        out_shape=[jax.ShapeDtypeStruct((m, nq), F32), jax.ShapeDtypeStruct((m, nkv), F32)],
        compiler_params=pltpu.CompilerParams(
            dimension_semantics=("arbitrary",), vmem_limit_bytes=VMEM_LIMIT),
        name="qkv_proj",
    )(x, gq, gkv, wq, wkv)


def _attn_prompt_kernel(q_ref, kp_ref, kc_ref, vp_ref, vc_ref, o_ref, lse_ref):
    i = pl.program_id(2)
    blk = q_ref.shape[0]
    qi = lax.broadcasted_iota(jnp.int32, (blk, blk), 0)
    ki = lax.broadcasted_iota(jnp.int32, (blk, blk), 1)
    mask_prev = jnp.logical_and(ki >= qi, i > 0)
    mask_cur = ki <= qi
    contract_last = (((1,), (1,)), ((), ()))
    scale = HEAD_DIM ** -0.5
    for h in range(HEADS):
        cols = slice(h * HEAD_DIM, (h + 1) * HEAD_DIM)
        qh = (q_ref[:, cols] * scale).astype(BF16)
        sp = lax.dot_general(qh, kp_ref[:, cols].astype(BF16), contract_last,
                             preferred_element_type=F32)
        sc = lax.dot_general(qh, kc_ref[:, cols].astype(BF16), contract_last,
                             preferred_element_type=F32)
        sp = jnp.where(mask_prev, sp, -jnp.inf)
        sc = jnp.where(mask_cur, sc, -jnp.inf)
        m = jnp.maximum(jnp.max(sp, axis=-1, keepdims=True),
                        jnp.max(sc, axis=-1, keepdims=True))
        pp = jnp.exp(sp - m)
        pc = jnp.exp(sc - m)
        den = jnp.sum(pp, axis=-1, keepdims=True) + jnp.sum(pc, axis=-1, keepdims=True)
        o = _bdot(pp, vp_ref[:, cols].astype(BF16)) + _bdot(pc, vc_ref[:, cols].astype(BF16))
        o_ref[:, cols] = o / den
        lse_ref[:, cols] = jnp.broadcast_to(m + jnp.log(den), (blk, HEAD_DIM))


def _attn_prompt_group(q, kv, g, dil):
    b, t, _ = q.shape
    n = t // dil
    nb = n // ATT_BLK
    qv = q.reshape(b, n, dil * q.shape[-1])
    kvv = kv.reshape(b, n, dil * kv.shape[-1])
    blk = (None, ATT_BLK, ATT_W)
    q_spec = pl.BlockSpec(blk, lambda bi, r, i: (bi, i, r * N_GROUPS + g))
    kc_spec = pl.BlockSpec(blk, lambda bi, r, i: (bi, i, r * 2 * N_GROUPS + g))
    kp_spec = pl.BlockSpec(blk, lambda bi, r, i: (bi, jnp.maximum(i - 1, 0), r * 2 * N_GROUPS + g))
    vc_spec = pl.BlockSpec(blk, lambda bi, r, i: (bi, i, r * 2 * N_GROUPS + N_GROUPS + g))
    vp_spec = pl.BlockSpec(
        blk, lambda bi, r, i: (bi, jnp.maximum(i - 1, 0), r * 2 * N_GROUPS + N_GROUPS + g))
    o_spec = pl.BlockSpec(blk, lambda bi, r, i: (bi, i, r))
    o, lse = pl.pallas_call(
        _attn_prompt_kernel,
        grid=(b, dil, nb),
        in_specs=[q_spec, kp_spec, kc_spec, vp_spec, vc_spec],
        out_specs=[o_spec, o_spec],
        out_shape=[jax.ShapeDtypeStruct((b, n, dil * ATT_W), F32)] * 2,
        compiler_params=pltpu.CompilerParams(
            dimension_semantics=("arbitrary", "arbitrary", "arbitrary"),
            vmem_limit_bytes=VMEM_LIMIT),
        name=f"attn_prompt_g{g}",
    )(qv, kvv, kvv, kvv, kvv)
    return o.reshape(b, t, ATT_W), lse.reshape(b, t, ATT_W)


def _exact_transpose(x):
    c = x.shape[1]
    eye = (lax.broadcasted_iota(jnp.int32, (c, c), 0)
           == lax.broadcasted_iota(jnp.int32, (c, c), 1)).astype(F32)
    return lax.dot_general(eye, x, (((1,), (1,)), ((), ())),
                           precision=lax.Precision.HIGHEST, preferred_element_type=F32)


def _attn_sample_kernel(q_ref, kn_ref, vn_ref, k_ref, v_ref,
                        o_ref, lse_ref, ko_ref, vo_ref,
                        qc_ref, kc_ref, vc_ref, oc_ref, lc_ref, *, dil):
    step = pl.program_id(0)
    bb, heads, dh, w = k_ref.shape
    npad = q_ref.shape[0]

    @pl.when(step == 0)
    def _():
        qc_ref[...] = _exact_transpose(q_ref[...]) * (HEAD_DIM ** -0.5)
        kc_ref[...] = _exact_transpose(kn_ref[...])
        vc_ref[...] = _exact_transpose(vn_ref[...])
        oc_ref[...] = jnp.zeros_like(oc_ref)
        lc_ref[...] = jnp.zeros_like(lc_ref)

    pos = lax.broadcasted_iota(jnp.int32, (1, w), 1)
    attended = (pos & (dil - 1)) == 0
    newest = pos == w - 1
    seq_lane = lax.broadcasted_iota(jnp.int32, (1, npad), 1)

    for i in range(bb):
        mine = seq_lane == step * bb + i

        def column(ref, rows):
            return jnp.sum(jnp.where(mine, ref[rows, :], 0.0), axis=-1, keepdims=True)

        def head(h, carry):
            rows = pl.ds(pl.multiple_of(h * dh, dh), dh)
            qc, kc, vc = column(qc_ref, rows), column(kc_ref, rows), column(vc_ref, rows)
            kt = k_ref[i, h]
            vt = v_ref[i, h]
            s = jnp.where(attended, jnp.sum(kt * qc, axis=0, keepdims=True), -jnp.inf)
            s_new = jnp.sum(kc * qc, axis=0, keepdims=True)
            m = jnp.maximum(jnp.max(s, axis=-1, keepdims=True), s_new)
            p = jnp.exp(s - m)
            p_new = jnp.exp(s_new - m)
            den = jnp.sum(p, axis=-1, keepdims=True) + p_new
            o = (jnp.sum(vt * p, axis=-1, keepdims=True) + p_new * vc) / den
            lse = m + jnp.log(den)
            oc_ref[rows, :] = jnp.where(mine, o, oc_ref[rows, :])
            lc_ref[rows, :] = jnp.where(mine, lse, lc_ref[rows, :])
            ko_ref[i, h] = jnp.where(newest, kc, pltpu.roll(kt, w - 1, axis=1))
            vo_ref[i, h] = jnp.where(newest, vc, pltpu.roll(vt, w - 1, axis=1))
            return carry

        lax.fori_loop(0, heads, head, 0)

    @pl.when(step == pl.num_programs(0) - 1)
    def _():
        o_ref[...] = _exact_transpose(oc_ref[...])
        lse_ref[...] = _exact_transpose(lc_ref[...])


SAMPLE_PAD = 128
SAMPLE_BB = {128: 8, 512: 4, 2048: 1}


def _attn_sample_group(q, k_new, v_new, k_cache, v_cache, dil):
    n, w = k_cache.shape[:2]
    assert w % dil == 0 and w // dil == ATT_BLK
    bb = SAMPLE_BB[w]
    pad = lambda x: jnp.pad(x, ((0, SAMPLE_PAD - n), (0, 0)))
    to_lanes = lambda c: jnp.transpose(c, (0, 2, 3, 1))
    cache_spec = pl.BlockSpec((bb, HEADS, HEAD_DIM, w), lambda i: (i, 0, 0, 0))
    vec_spec = _full_spec((SAMPLE_PAD, ATT_W))
    cache_shape = jax.ShapeDtypeStruct((n, HEADS, HEAD_DIM, w), F32)
    vec_shape = jax.ShapeDtypeStruct((SAMPLE_PAD, ATT_W), F32)
    o, lse, k_out, v_out = pl.pallas_call(
        functools.partial(_attn_sample_kernel, dil=dil),
        grid=(n // bb,),
        in_specs=[vec_spec] * 3 + [cache_spec] * 2,
        out_specs=[vec_spec] * 2 + [cache_spec] * 2,
        out_shape=[vec_shape] * 2 + [cache_shape] * 2,
        scratch_shapes=[pltpu.VMEM((ATT_W, SAMPLE_PAD), F32)] * 5,
        compiler_params=pltpu.CompilerParams(
            dimension_semantics=("arbitrary",), vmem_limit_bytes=VMEM_LIMIT),
        name=f"attn_sample_d{dil}",
    )(pad(q), pad(k_new), pad(v_new), to_lanes(k_cache), to_lanes(v_cache))
    to_rows = lambda c: jnp.transpose(c, (0, 3, 1, 2))
    return o[:n], lse[:n], to_rows(k_out), to_rows(v_out)


def _merge_kernel(x_ref, o0_ref, o1_ref, o2_ref, l0_ref, l1_ref, l2_ref, wo_ref, y_ref):
    l0, l1, l2 = l0_ref[...], l1_ref[...], l2_ref[...]
    m = jnp.maximum(jnp.maximum(l0, l1), l2)
    e0, e1, e2 = jnp.exp(l0 - m), jnp.exp(l1 - m), jnp.exp(l2 - m)
    den = e0 + e1 + e2
    o = (e0 / den) * o0_ref[...] + (e1 / den) * o1_ref[...] + (e2 / den) * o2_ref[...]
    y_ref[...] = x_ref[...] + _bdot(o, wo_ref[...])


def _merge(x, outs, lses, wo, tm):
    m, d = x.shape
    w = wo.shape[0]
    row = lambda i: (i, 0)
    part = pl.BlockSpec((tm, w), row)
    return pl.pallas_call(
        _merge_kernel,
        grid=(m // tm,),
        in_specs=[pl.BlockSpec((tm, d), row)] + [part] * 6 + [_weight_spec(wo)],
        out_specs=pl.BlockSpec((tm, d), row),
        out_shape=jax.ShapeDtypeStruct((m, d), F32),
        compiler_params=pltpu.CompilerParams(
            dimension_semantics=("arbitrary",), vmem_limit_bytes=VMEM_LIMIT),
        name="attn_merge",
    )(x, *outs, *lses, wo)


def kernel(x_prompt, x_sample, state_lru_h, state_conv_a, state_ffn_conv, cache_k0, cache_v0, cache_k1, cache_v1, cache_k2, cache_v2, norm_mix, a_w_in, a_conv_w, a_conv_b, a_gate_a_w, a_gate_a_b, a_gate_x_w, a_gate_x_b, a_lambda, a_w_out, kv_norm, w_kv, b_w_q, b_w_o, norm_ffn, ffn_w_up, ffn_conv_w, ffn_conv_b, ffn_w_down, final_norm):
    b, t, d = x_prompt.shape
    n = x_sample.shape[0]
    assert x_sample.shape[1] == 1 and norm_mix.shape[0] == 2
    caches = (cache_k0, cache_v0, cache_k1, cache_v1, cache_k2, cache_v2)
    row = lambda v: v.reshape(1, -1)

    mixer_w = (row(norm_mix[0]), a_w_in[0].astype(BF16), a_conv_w[0], row(a_conv_b[0]),
               a_gate_a_w[0].astype(BF16), row(a_gate_a_b[0]),
               a_gate_x_w[0].astype(BF16), row(a_gate_x_b[0]),
               row(a_lambda[0]), a_w_out[0].astype(BF16))
    ffn_w = [(row(norm_ffn[l]), ffn_w_up[l].astype(BF16), ffn_conv_w[l],
              row(ffn_conv_b[l]), ffn_w_down[l].astype(BF16)) for l in range(2)]
    ffn_w[1] = ffn_w[1] + (row(final_norm),)
    gq, gkv = row(norm_mix[1]), row(kv_norm)
    wq, wkv, wo = b_w_q[0].astype(BF16), w_kv.astype(BF16), b_w_o[0].astype(BF16)

    h, p_h, p_ca = _mixer_prompt(x_prompt, mixer_w)
    h, p_f0 = _ffn_prompt(h, ffn_w[0], final_norm=False)
    hf = h.reshape(b * t, d)
    q, kv = _qkv(hf, gq, gkv, wq, wkv, TM_TOKEN)
    q3, kv3 = q.reshape(b, t, -1), kv.reshape(b, t, -1)
    outs, lses = [], []
    for g, (_, dil) in enumerate(GROUPS):
        o, lse = _attn_prompt_group(q3, kv3, g, dil)
        outs.append(o.reshape(b * t, ATT_W))
        lses.append(lse.reshape(b * t, ATT_W))
    hf = _merge(hf, outs, lses, wo, TM_TOKEN)
    y_prompt, p_f1 = _ffn_prompt(hf.reshape(b, t, d), ffn_w[1], final_norm=True)
    p_kv = []
    for g, (win, _) in enumerate(GROUPS):
        keep = min(win, t)
        for off in (g * ATT_W, (N_GROUPS + g) * ATT_W):
            p_kv.append(kv3[:, t - keep:, off:off + ATT_W].reshape(b, keep, HEADS, HEAD_DIM))

    xs = x_sample.reshape(n, d)
    conv_buf = jnp.swapaxes(state_conv_a[0], 0, 1)
    hs, s_h, s_ur = _mixer_sample(xs, state_lru_h[0], conv_buf, mixer_w)
    s_ca = jnp.concatenate([state_conv_a[0][:, 1:], s_ur[:, None]], axis=1)
    hs, s_ug0 = _ffn_sample(hs, jnp.swapaxes(state_ffn_conv[0], 0, 1), ffn_w[0], final_norm=False)
    qs, kvs = _qkv(hs, gq, gkv, wq, wkv, n)
    outs_s, lses_s, s_kv = [], [], []
    for g, (_, dil) in enumerate(GROUPS):
        col = lambda a, j: a[:, j * ATT_W:(j + 1) * ATT_W]
        o, lse, k_out, v_out = _attn_sample_group(
            col(qs, g), col(kvs, g), col(kvs, N_GROUPS + g), caches[2 * g], caches[2 * g + 1], dil)
        outs_s.append(o)
        lses_s.append(lse)
        s_kv += [k_out, v_out]
    hs = _merge(hs, outs_s, lses_s, wo, n)
    y_sample, s_ug1 = _ffn_sample(hs, jnp.swapaxes(state_ffn_conv[1], 0, 1), ffn_w[1], final_norm=True)
    s_ffn = jnp.stack([
        jnp.concatenate([state_ffn_conv[l][:, 1:], ug[:, None]], axis=1)
        for l, ug in enumerate((s_ug0, s_ug1))])
    return (y_prompt, y_sample.reshape(n, 1, d),
            p_h.reshape(1, b, d), s_h.reshape(1, n, d),
            p_ca.reshape(1, b, -1, d), s_ca.reshape(1, n, -1, d),
            jnp.stack([p_f0, p_f1]), s_ffn,
            p_kv[0], p_kv[1], s_kv[0], s_kv[1],
            p_kv[2], p_kv[3], s_kv[2], s_kv[3],
            p_kv[4], p_kv[5], s_kv[4], s_kv[5])
```

```python
import functools

import jax
import jax.numpy as jnp
from jax import lax
from jax.experimental import pallas as pl
from jax.experimental.pallas import tpu as pltpu

F32 = jnp.float32
BF16 = jnp.bfloat16

EPS = 1e-6
LRU_C = 8.0
N_LRU_BLOCKS = 4
HEAD_DIM = 64
HEADS = 8
GROUPS = ((128, 1), (512, 4), (2048, 16))
N_GROUPS = len(GROUPS)
ATT_W = HEADS * HEAD_DIM
ATT_BLK = 128
LANES = 128
CARRY_ROWS = 8

TQ_MIXER = 256
TM_FFN = 512
FC_FFN = 512
TM_TOKEN = 512
VMEM_LIMIT = 52 * 1024 * 1024


def _rmsnorm(x, g):
    return x * lax.rsqrt(jnp.mean(x * x, axis=-1, keepdims=True) + EPS) * g


def _bdot(a, w):
    return jnp.dot(a.astype(BF16), w, preferred_element_type=F32)


def _shift_rows(cur, prev, s):
    rolled = pltpu.roll(cur, s, axis=0)
    prev_rolled = pltpu.roll(prev, s, axis=0)
    row = lax.broadcasted_iota(jnp.int32, prev.shape, 0)
    head = jnp.where(row < s, prev_rolled, rolled[:CARRY_ROWS])
    return jnp.concatenate([head, rolled[CARRY_ROWS:]], axis=0)


def _causal_conv_rows(cur, prev, w_ref, b):
    k = w_ref.shape[0]
    y = b + w_ref[k - 1:k, :] * cur
    for j in range(k - 1):
        y = y + w_ref[j:j + 1, :] * _shift_rows(cur, prev, k - 1 - j)
    return y


def _log_sigmoid(x):
    return jnp.minimum(x, 0.0) - jnp.log1p(jnp.exp(-jnp.abs(x)))


def _expm1(x):
    u = jnp.exp(x)
    um1 = u - 1.0
    tiny = um1 == 0.0
    ratio = x / jnp.where(tiny, 1.0, jnp.log(u))
    return jnp.where(tiny, x, jnp.where(um1 == -1.0, -1.0, um1 * ratio))


def _lru_coeffs(xc, wa_ref, ba, wx_ref, bx, lam):
    xb = xc.astype(BF16)
    blk = xc.shape[-1] // N_LRU_BLOCKS
    ra, ri = [], []
    for n in range(N_LRU_BLOCKS):
        xs = xb[:, n * blk:(n + 1) * blk]
        ra.append(jnp.dot(xs, wa_ref[n], preferred_element_type=F32))
        ri.append(jnp.dot(xs, wx_ref[n], preferred_element_type=F32))
    r = jax.nn.sigmoid(jnp.concatenate(ra, axis=-1) + ba)
    i = jax.nn.sigmoid(jnp.concatenate(ri, axis=-1) + bx)
    log_a = LRU_C * r * _log_sigmoid(lam)
    a = jnp.exp(log_a)
    bt = jnp.sqrt(-_expm1(2.0 * log_a)) * (i * xc)
    return a, bt


def _scan_rows(a, b):
    n = a.shape[0]
    row = lax.broadcasted_iota(jnp.int32, a.shape, 0)
    s = 1
    while s < n:
        keep = row >= s
        a_sh = jnp.where(keep, pltpu.roll(a, s, axis=0), 1.0)
        b_sh = jnp.where(keep, pltpu.roll(b, s, axis=0), 0.0)
        b = a * b_sh + b
        a = a * a_sh
        s *= 2
    return a, b


def _mixer_prompt_kernel(x_ref, g_ref, win_ref, cw_ref, cb_ref, wa_ref, ba_ref,
                         wx_ref, bx_ref, lam_ref, wout_ref,
                         y_ref, hlast_ref, cbuf_ref, ucarry, hcarry):
    t = pl.program_id(1)
    tq, d = x_ref.shape

    @pl.when(t == 0)
    def _():
        ucarry[...] = jnp.zeros_like(ucarry)
        hcarry[...] = jnp.zeros_like(hcarry)

    x = x_ref[...]
    u = _bdot(_rmsnorm(x, g_ref[...]), win_ref[...])
    gate = jax.nn.gelu(u[:, :d])
    ur = u[:, d:]
    xc = _causal_conv_rows(ur, ucarry[...], cw_ref, cb_ref[...])
    ucarry[...] = ur[tq - CARRY_ROWS:, :]
    a, bt = _lru_coeffs(xc, wa_ref, ba_ref[...], wx_ref, bx_ref[...], lam_ref[...])
    a_cum, b_cum = _scan_rows(a, bt)
    h = a_cum * hcarry[...] + b_cum
    hcarry[...] = h[tq - 1:tq, :]
    y_ref[...] = x + _bdot(h * gate, wout_ref[...])

    @pl.when(t == pl.num_programs(1) - 1)
    def _():
        hlast_ref[...] = h[tq - 1:tq, :]
        k = cw_ref.shape[0]
        cbuf_ref[...] = ur[tq - (k - 1):, :]


def _mixer_sample_kernel(x_ref, h0_ref, buf_ref, g_ref, win_ref, cw_ref, cb_ref,
                         wa_ref, ba_ref, wx_ref, bx_ref, lam_ref, wout_ref,
                         y_ref, hnew_ref, ur_ref):
    d = x_ref.shape[-1]
    k = cw_ref.shape[0]
    x = x_ref[...]
    u = _bdot(_rmsnorm(x, g_ref[...]), win_ref[...])
    gate = jax.nn.gelu(u[:, :d])
    ur = u[:, d:]
    xc = cb_ref[...] + cw_ref[k - 1:k, :] * ur
    for j in range(k - 1):
        xc = xc + cw_ref[j:j + 1, :] * buf_ref[j]
    a, bt = _lru_coeffs(xc, wa_ref, ba_ref[...], wx_ref, bx_ref[...], lam_ref[...])
    h = a * h0_ref[...] + bt
    hnew_ref[...] = h
    ur_ref[...] = ur
    y_ref[...] = x + _bdot(h * gate, wout_ref[...])


def _full_spec(shape):
    zeros = (0,) * len(shape)
    return pl.BlockSpec(shape, lambda *_: zeros)


def _weight_spec(w):
    zeros = (0,) * w.ndim
    return pl.BlockSpec(w.shape, lambda *_: zeros, pipeline_mode=pl.Buffered(1))


def _weight_specs(weights):
    return [_weight_spec(w) for w in weights]


def _mixer_prompt(x, weights):
    b, t, d = x.shape
    tq = TQ_MIXER
    k = weights[2].shape[0]
    row = lambda bi, ti: (bi, ti, 0)
    per_seq = lambda bi, ti: (bi, 0, 0)
    return pl.pallas_call(
        _mixer_prompt_kernel,
        grid=(b, t // tq),
        in_specs=[pl.BlockSpec((None, tq, d), row)] + _weight_specs(weights),
        out_specs=[pl.BlockSpec((None, tq, d), row),
                   pl.BlockSpec((None, 1, d), per_seq),
                   pl.BlockSpec((None, k - 1, d), per_seq)],
        out_shape=[jax.ShapeDtypeStruct((b, t, d), F32),
                   jax.ShapeDtypeStruct((b, 1, d), F32),
                   jax.ShapeDtypeStruct((b, k - 1, d), F32)],
        scratch_shapes=[pltpu.VMEM((CARRY_ROWS, d), F32), pltpu.VMEM((1, d), F32)],
        compiler_params=pltpu.CompilerParams(
            dimension_semantics=("arbitrary", "arbitrary"),
            vmem_limit_bytes=VMEM_LIMIT),
        name="mixer_prompt",
    )(x, *weights)


def _mixer_sample(x, h0, buf, weights):
    n, d = x.shape
    k = weights[2].shape[0]
    return pl.pallas_call(
        _mixer_sample_kernel,
        grid=(1,),
        in_specs=[_full_spec((n, d)), _full_spec((n, d)), _full_spec((k - 1, n, d))]
        + _weight_specs(weights),
        out_specs=[_full_spec((n, d))] * 3,
        out_shape=[jax.ShapeDtypeStruct((n, d), F32)] * 3,
        compiler_params=pltpu.CompilerParams(
            dimension_semantics=("arbitrary",), vmem_limit_bytes=VMEM_LIMIT),
        name="mixer_sample",
    )(x, h0, buf, *weights)


def _ffn_chunks(xb, x, wup_ref, cw_ref, cb_ref, wdown_ref, conv_gate):
    f = wdown_ref.shape[0]
    acc = x
    for lo in range(0, f, FC_FFN):
        hi = lo + FC_FFN
        ug = jnp.dot(xb, wup_ref[:, lo:hi], preferred_element_type=F32)
        uv = jnp.dot(xb, wup_ref[:, f + lo:f + hi], preferred_element_type=F32)
        gc = conv_gate(ug, lo, hi)
        acc = acc + _bdot(jax.nn.gelu(gc) * uv, wdown_ref[lo:hi, :])
    return acc


def _ffn_prompt_kernel(*refs, final_norm):
    if final_norm:
        (x_ref, g_ref, wup_ref, cw_ref, cb_ref, wdown_ref, fn_ref,
         y_ref, fbuf_ref, gcarry) = refs
    else:
        (x_ref, g_ref, wup_ref, cw_ref, cb_ref, wdown_ref,
         y_ref, fbuf_ref, gcarry) = refs
    t = pl.program_id(1)
    tm = x_ref.shape[0]
    k = cw_ref.shape[0]

    @pl.when(t == 0)
    def _():
        gcarry[...] = jnp.zeros_like(gcarry)

    def conv_gate(ug, lo, hi):
        gc = _causal_conv_rows(ug, gcarry[:, lo:hi], cw_ref.at[:, lo:hi], cb_ref[:, lo:hi])
        gcarry[:, lo:hi] = ug[tm - CARRY_ROWS:, :]
        return gc

    x = x_ref[...]
    xb = _rmsnorm(x, g_ref[...]).astype(BF16)
    y = _ffn_chunks(xb, x, wup_ref, cw_ref, cb_ref, wdown_ref, conv_gate)
    if final_norm:
        y = _rmsnorm(y, fn_ref[...])
    y_ref[...] = y

    @pl.when(t == pl.num_programs(1) - 1)
    def _():
        fbuf_ref[...] = gcarry[CARRY_ROWS - (k - 1):, :]


def _ffn_sample_kernel(*refs, final_norm):
    if final_norm:
        (x_ref, buf_ref, g_ref, wup_ref, cw_ref, cb_ref, wdown_ref, fn_ref,
         y_ref, ug_ref) = refs
    else:
        (x_ref, buf_ref, g_ref, wup_ref, cw_ref, cb_ref, wdown_ref,
         y_ref, ug_ref) = refs
    k = cw_ref.shape[0]

    def conv_gate(ug, lo, hi):
        ug_ref[:, lo:hi] = ug
        gc = cb_ref[:, lo:hi] + cw_ref[k - 1:k, lo:hi] * ug
        for j in range(k - 1):
            gc = gc + cw_ref[j:j + 1, lo:hi] * buf_ref[j, :, lo:hi]
        return gc

    x = x_ref[...]
    xb = _rmsnorm(x, g_ref[...]).astype(BF16)
    y = _ffn_chunks(xb, x, wup_ref, cw_ref, cb_ref, wdown_ref, conv_gate)
    if final_norm:
        y = _rmsnorm(y, fn_ref[...])
    y_ref[...] = y


def _ffn_prompt(x, weights, final_norm):
    b, t, d = x.shape
    f = weights[4].shape[0]
    k = weights[2].shape[0]
    tm = TM_FFN
    row = lambda bi, ti: (bi, ti, 0)
    per_seq = lambda bi, ti: (bi, 0, 0)
    return pl.pallas_call(
        functools.partial(_ffn_prompt_kernel, final_norm=final_norm),
        grid=(b, t // tm),
        in_specs=[pl.BlockSpec((None, tm, d), row)] + _weight_specs(weights),
        out_specs=[pl.BlockSpec((None, tm, d), row),
                   pl.BlockSpec((None, k - 1, f), per_seq)],
        out_shape=[jax.ShapeDtypeStruct((b, t, d), F32),
                   jax.ShapeDtypeStruct((b, k - 1, f), F32)],
        scratch_shapes=[pltpu.VMEM((CARRY_ROWS, f), F32)],
        compiler_params=pltpu.CompilerParams(
            dimension_semantics=("arbitrary", "arbitrary"),
            vmem_limit_bytes=VMEM_LIMIT),
        name="ffn_prompt_final" if final_norm else "ffn_prompt",
    )(x, *weights)


def _ffn_sample(x, buf, weights, final_norm):
    n, d = x.shape
    f = weights[4].shape[0]
    k = weights[2].shape[0]
    return pl.pallas_call(
        functools.partial(_ffn_sample_kernel, final_norm=final_norm),
        grid=(1,),
        in_specs=[_full_spec((n, d)), _full_spec((k - 1, n, f))]
        + _weight_specs(weights),
        out_specs=[_full_spec((n, d)), _full_spec((n, f))],
        out_shape=[jax.ShapeDtypeStruct((n, d), F32), jax.ShapeDtypeStruct((n, f), F32)],
        compiler_params=pltpu.CompilerParams(
            dimension_semantics=("arbitrary",), vmem_limit_bytes=VMEM_LIMIT),
        name="ffn_sample_final" if final_norm else "ffn_sample",
    )(x, buf, *weights)


def _qkv_kernel(x_ref, gq_ref, gkv_ref, wq_ref, wkv_ref, q_ref, kv_ref):
    x = x_ref[...]
    inv = lax.rsqrt(jnp.mean(x * x, axis=-1, keepdims=True) + EPS)
    y = x * inv
    q_ref[...] = _bdot(y * gq_ref[...], wq_ref[...])
    kv_ref[...] = _bdot(y * gkv_ref[...], wkv_ref[...])


def _qkv(x, gq, gkv, wq, wkv, tm):
    m, d = x.shape
    nq, nkv = wq.shape[1], wkv.shape[1]
    row = lambda i: (i, 0)
    return pl.pallas_call(
        _qkv_kernel,
        grid=(m // tm,),
        in_specs=[pl.BlockSpec((tm, d), row)] + _weight_specs((gq, gkv, wq, wkv)),
        out_specs=[pl.BlockSpec((tm, nq), row), pl.BlockSpec((tm, nkv), row)],
        out_shape=[jax.ShapeDtypeStruct((m, nq), F32), jax.ShapeDtypeStruct((m, nkv), F32)],
        compiler_params=pltpu.CompilerParams(
            dimension_semantics=("arbitrary",), vmem_limit_bytes=VMEM_LIMIT),
        name="qkv_proj",
    )(x, gq, gkv, wq, wkv)


LSE_REP = ATT_BLK // HEADS


_NT = (((1,), (1,)), ((), ()))


def _qkv_prompt_kernel(x_ref, gq_ref, gkv_ref, wq_ref, wv_ref, wkt_ref,
                       q_ref, kt_ref, v_ref, ybuf):
    dil, per = q_ref.shape[0], q_ref.shape[1]
    x = x_ref[...]
    y = x * lax.rsqrt(jnp.mean(x * x, axis=-1, keepdims=True) + EPS)
    if dil > 1:
        chunks = ybuf.shape[0]
        for c in range(chunks):
            ybuf[c] = y[:, c * LANES:(c + 1) * LANES]
        y = jnp.concatenate(
            [jnp.concatenate([ybuf[c, pl.ds(r, per, stride=dil), :] for c in range(chunks)], axis=1)
             for r in range(dil)], axis=0)
    xq = (y * gq_ref[...]).astype(BF16)
    xkv = (y * gkv_ref[...]).astype(BF16)
    q = jnp.dot(xq, wq_ref[...], preferred_element_type=F32) * (HEAD_DIM ** -0.5)
    q_ref[...] = q.astype(BF16).reshape(dil, per, -1)
    v = jnp.dot(xkv, wv_ref[...], preferred_element_type=F32)
    v_ref[...] = v.astype(BF16).reshape(dil, per, -1)
    kt = lax.dot_general(wkt_ref[...], xkv, _NT, preferred_element_type=F32).astype(BF16)
    for r in range(dil):
        kt_ref[r] = kt[:, r * per:(r + 1) * per]


def _qkv_prompt_group(x, gq, gkv, wq, wv, wkt, dil):
    b, t, d = x.shape
    tm = max(ATT_BLK * dil, TM_TOKEN)
    per = tm // dil
    assert t % tm == 0
    rows = pl.BlockSpec((None, dil, per, ATT_W), lambda bi, ti: (bi, 0, ti, 0))
    cols = pl.BlockSpec((None, dil, ATT_W, per), lambda bi, ti: (bi, 0, 0, ti))
    weights = (gq, gkv, wq, wv, wkt)
    return pl.pallas_call(
        _qkv_prompt_kernel,
        grid=(b, t // tm),
        in_specs=[pl.BlockSpec((None, tm, d), lambda bi, ti: (bi, ti, 0))] + _weight_specs(weights),
        out_specs=[rows, cols, rows],
        out_shape=[jax.ShapeDtypeStruct((b, dil, t // dil, ATT_W), BF16),
                   jax.ShapeDtypeStruct((b, dil, ATT_W, t // dil), BF16),
                   jax.ShapeDtypeStruct((b, dil, t // dil, ATT_W), BF16)],
        scratch_shapes=[pltpu.VMEM((d // LANES, tm, LANES), F32)],
        compiler_params=pltpu.CompilerParams(
            dimension_semantics=("arbitrary", "arbitrary"), vmem_limit_bytes=VMEM_LIMIT),
        name=f"qkv_prompt_d{dil}",
    )(x, *weights)


def _kv_tail_kernel(x_ref, gkv_ref, wkt_ref, wvt_ref, kt_ref, vt_ref):
    x = x_ref[...]
    y = x * lax.rsqrt(jnp.mean(x * x, axis=-1, keepdims=True) + EPS)
    xkv = (y * gkv_ref[...]).astype(BF16)
    kt_ref[...] = lax.dot_general(wkt_ref[...], xkv, _NT, preferred_element_type=F32)
    vt_ref[...] = lax.dot_general(wvt_ref[...], xkv, _NT, preferred_element_type=F32)


def _kv_tail(x, gkv, wkt, wvt, win):
    b, t, d = x.shape
    assert t % win == 0
    out = pl.BlockSpec((None, ATT_W, win), lambda bi: (bi, 0, 0))
    weights = (gkv, wkt, wvt)
    return pl.pallas_call(
        _kv_tail_kernel,
        grid=(b,),
        in_specs=[pl.BlockSpec((None, win, d), lambda bi: (bi, t // win - 1, 0))]
        + _weight_specs(weights),
        out_specs=[out, out],
        out_shape=[jax.ShapeDtypeStruct((b, ATT_W, win), F32)] * 2,
        compiler_params=pltpu.CompilerParams(
            dimension_semantics=("arbitrary",), vmem_limit_bytes=VMEM_LIMIT),
        name=f"kv_tail_w{win}",
    )(x, *weights)


def _attn_prompt_kernel(q_ref, ktp_ref, ktc_ref, vp_ref, vc_ref, o_ref, lse_ref):
    i = pl.program_id(2)
    blk = q_ref.shape[0]
    q = q_ref[...]
    kt = jnp.concatenate([ktp_ref[...], ktc_ref[...]], axis=1)
    v = jnp.concatenate([vp_ref[...], vc_ref[...]], axis=0)
    qi = lax.broadcasted_iota(jnp.int32, (blk, 2 * blk), 0)
    kj = lax.broadcasted_iota(jnp.int32, (blk, 2 * blk), 1)
    keep_prev = jnp.logical_and(jnp.logical_and(kj < blk, kj >= qi), i > 0)
    keep = jnp.logical_or(keep_prev, jnp.logical_and(kj >= blk, kj - blk <= qi))
    lane = lax.broadcasted_iota(jnp.int32, (blk, blk), 1)
    low_half = lane < HEAD_DIM
    head_of_lane = lane // LSE_REP
    lse_tile = jnp.zeros((blk, blk), F32)
    zero = jnp.zeros((), BF16)
    for pair in range(HEADS // 2):
        cols = slice(pair * blk, (pair + 1) * blk)
        q2, kt2, v2 = q[:, cols], kt[cols, :], v[:, cols]
        outs = []
        for half in range(2):
            mine = low_half if half == 0 else jnp.logical_not(low_half)
            s = jnp.dot(jnp.where(mine, q2, zero), kt2, preferred_element_type=F32)
            s = jnp.where(keep, s, -jnp.inf)
            m = jnp.max(s, axis=-1, keepdims=True)
            p = jnp.exp(s - m)
            den = jnp.sum(p, axis=-1, keepdims=True)
            outs.append(jnp.dot(p.astype(BF16), v2, preferred_element_type=F32) / den)
            lse_tile = jnp.where(head_of_lane == 2 * pair + half, m + jnp.log(den), lse_tile)
        o_ref[:, cols] = jnp.where(low_half, outs[0], outs[1])
    lse_ref[...] = lse_tile


def _attn_prompt_group(q, kt, v, g):
    b, dil, n, _ = q.shape
    nb = n // ATT_BLK
    prev = lambda i: jnp.maximum(i - 1, 0)
    rows = (None, None, ATT_BLK, ATT_W)
    cols = (None, None, ATT_W, ATT_BLK)
    return pl.pallas_call(
        _attn_prompt_kernel,
        grid=(b, dil, nb),
        in_specs=[pl.BlockSpec(rows, lambda bi, r, i: (bi, r, i, 0)),
                  pl.BlockSpec(cols, lambda bi, r, i: (bi, r, 0, prev(i))),
                  pl.BlockSpec(cols, lambda bi, r, i: (bi, r, 0, i)),
                  pl.BlockSpec(rows, lambda bi, r, i: (bi, r, prev(i), 0)),
                  pl.BlockSpec(rows, lambda bi, r, i: (bi, r, i, 0))],
        out_specs=[pl.BlockSpec(rows, lambda bi, r, i: (bi, r, i, 0)),
                   pl.BlockSpec((None, None, ATT_BLK, ATT_BLK), lambda bi, r, i: (bi, r, i, 0))],
        out_shape=[jax.ShapeDtypeStruct((b, dil, n, ATT_W), F32),
                   jax.ShapeDtypeStruct((b, dil, n, ATT_BLK), F32)],
        compiler_params=pltpu.CompilerParams(
            dimension_semantics=("arbitrary", "arbitrary", "arbitrary"),
            vmem_limit_bytes=VMEM_LIMIT),
        name=f"attn_prompt_g{g}",
    )(q, kt, kt, v, v)


def _merge_prompt_kernel(x_ref, o0_ref, o1_ref, o2_ref, l0_ref, l1_ref, l2_ref, sel_ref, wo_ref,
                         y_ref, obuf, lbuf):
    def token_order(ref, buf):
        dil, per, width = ref.shape
        if dil == 1:
            return ref[0]
        chunks = width // LANES
        for r in range(dil):
            rows = ref[r]
            for c in range(chunks):
                buf[c, pl.ds(r, per, stride=dil), :] = rows[:, c * LANES:(c + 1) * LANES]
        return jnp.concatenate([buf[c] for c in range(chunks)], axis=1)

    lses = []
    for l_ref in (l0_ref, l1_ref, l2_ref):
        lses.append(token_order(l_ref, lbuf))
    m = jnp.maximum(jnp.maximum(lses[0], lses[1]), lses[2])
    es = [jnp.exp(l - m) for l in lses]
    den = es[0] + es[1] + es[2]
    comb = None
    for e, o_ref in zip(es, (o0_ref, o1_ref, o2_ref)):
        w = e / den
        hi = w.astype(BF16)
        lo = (w - hi.astype(F32)).astype(BF16)
        wide = (jnp.dot(hi, sel_ref[...], preferred_element_type=F32)
                + jnp.dot(lo, sel_ref[...], preferred_element_type=F32))
        term = wide * token_order(o_ref, obuf)
        comb = term if comb is None else comb + term
    y_ref[...] = x_ref[...] + _bdot(comb, wo_ref[...])


def _merge_prompt(x, outs, lses, wo):
    b, t, d = x.shape
    tm = TM_TOKEN
    head_of_lane = jnp.arange(ATT_BLK) // LSE_REP
    first_copy = jnp.arange(ATT_BLK) % LSE_REP == 0
    sel = ((head_of_lane[:, None] == (jnp.arange(ATT_W) // HEAD_DIM)[None, :])
           & first_copy[:, None]).astype(BF16)
    part_specs = []
    for width, parts in ((ATT_W, outs), (ATT_BLK, lses)):
        for p in parts:
            dil = p.shape[1]
            part_specs.append(pl.BlockSpec((None, dil, tm // dil, width),
                                           lambda bi, ti: (bi, 0, ti, 0)))
    row = lambda bi, ti: (bi, ti, 0)
    return pl.pallas_call(
        _merge_prompt_kernel,
        grid=(b, t // tm),
        in_specs=[pl.BlockSpec((None, tm, d), row)] + part_specs + _weight_specs((sel, wo)),
        out_specs=pl.BlockSpec((None, tm, d), row),
        out_shape=jax.ShapeDtypeStruct((b, t, d), F32),
        scratch_shapes=[pltpu.VMEM((ATT_W // LANES, tm, LANES), F32),
                        pltpu.VMEM((ATT_BLK // LANES, tm, LANES), F32)],
        compiler_params=pltpu.CompilerParams(
            dimension_semantics=("arbitrary", "arbitrary"), vmem_limit_bytes=VMEM_LIMIT),
        name="attn_merge_prompt",
    )(x, *outs, *lses, sel, wo)


def _exact_transpose(x):
    c = x.shape[1]
    eye = (lax.broadcasted_iota(jnp.int32, (c, c), 0)
           == lax.broadcasted_iota(jnp.int32, (c, c), 1)).astype(F32)
    return lax.dot_general(eye, x, (((1,), (1,)), ((), ())),
                           precision=lax.Precision.HIGHEST, preferred_element_type=F32)


def _attn_sample_kernel(q_ref, kn_ref, vn_ref, k_ref, v_ref,
                        o_ref, lse_ref, ko_ref, vo_ref,
                        qc_ref, kc_ref, vc_ref, oc_ref, lc_ref, *, dil):
    step = pl.program_id(0)
    bb, heads, dh, w = k_ref.shape
    npad = q_ref.shape[0]

    @pl.when(step == 0)
    def _():
        qc_ref[...] = _exact_transpose(q_ref[...]) * (HEAD_DIM ** -0.5)
        kc_ref[...] = _exact_transpose(kn_ref[...])
        vc_ref[...] = _exact_transpose(vn_ref[...])
        oc_ref[...] = jnp.zeros_like(oc_ref)
        lc_ref[...] = jnp.zeros_like(lc_ref)

    pos = lax.broadcasted_iota(jnp.int32, (1, w), 1)
    attended = (pos & (dil - 1)) == 0
    newest = pos == w - 1
    seq_lane = lax.broadcasted_iota(jnp.int32, (1, npad), 1)

    for i in range(bb):
        mine = seq_lane == step * bb + i

        def column(ref, rows):
            return jnp.sum(jnp.where(mine, ref[rows, :], 0.0), axis=-1, keepdims=True)

        def head(h, carry):
            rows = pl.ds(pl.multiple_of(h * dh, dh), dh)
            qc, kc, vc = column(qc_ref, rows), column(kc_ref, rows), column(vc_ref, rows)
            kt = k_ref[i, h]
            vt = v_ref[i, h]
            s = jnp.where(attended, jnp.sum(kt * qc, axis=0, keepdims=True), -jnp.inf)
            s_new = jnp.sum(kc * qc, axis=0, keepdims=True)
            m = jnp.maximum(jnp.max(s, axis=-1, keepdims=True), s_new)
            p = jnp.exp(s - m)
            p_new = jnp.exp(s_new - m)
            den = jnp.sum(p, axis=-1, keepdims=True) + p_new
            o = (jnp.sum(vt * p, axis=-1, keepdims=True) + p_new * vc) / den
            lse = m + jnp.log(den)
            oc_ref[rows, :] = jnp.where(mine, o, oc_ref[rows, :])
            lc_ref[rows, :] = jnp.where(mine, lse, lc_ref[rows, :])
            ko_ref[i, h] = jnp.where(newest, kc, pltpu.roll(kt, w - 1, axis=1))
            vo_ref[i, h] = jnp.where(newest, vc, pltpu.roll(vt, w - 1, axis=1))
            return carry

        lax.fori_loop(0, heads, head, 0)

    @pl.when(step == pl.num_programs(0) - 1)
    def _():
        o_ref[...] = _exact_transpose(oc_ref[...])
        lse_ref[...] = _exact_transpose(lc_ref[...])


SAMPLE_PAD = 128
SAMPLE_BB = {128: 8, 512: 4, 2048: 1}


def _attn_sample_group(q, k_new, v_new, k_cache, v_cache, dil):
    n, w = k_cache.shape[:2]
    assert w % dil == 0 and w // dil == ATT_BLK
    bb = SAMPLE_BB[w]
    pad = lambda x: jnp.pad(x, ((0, SAMPLE_PAD - n), (0, 0)))
    to_lanes = lambda c: jnp.transpose(c, (0, 2, 3, 1))
    cache_spec = pl.BlockSpec((bb, HEADS, HEAD_DIM, w), lambda i: (i, 0, 0, 0))
    vec_spec = _full_spec((SAMPLE_PAD, ATT_W))
    cache_shape = jax.ShapeDtypeStruct((n, HEADS, HEAD_DIM, w), F32)
    vec_shape = jax.ShapeDtypeStruct((SAMPLE_PAD, ATT_W), F32)
    o, lse, k_out, v_out = pl.pallas_call(
        functools.partial(_attn_sample_kernel, dil=dil),
        grid=(n // bb,),
        in_specs=[vec_spec] * 3 + [cache_spec] * 2,
        out_specs=[vec_spec] * 2 + [cache_spec] * 2,
        out_shape=[vec_shape] * 2 + [cache_shape] * 2,
        scratch_shapes=[pltpu.VMEM((ATT_W, SAMPLE_PAD), F32)] * 5,
        compiler_params=pltpu.CompilerParams(
            dimension_semantics=("arbitrary",), vmem_limit_bytes=VMEM_LIMIT),
        name=f"attn_sample_d{dil}",
    )(pad(q), pad(k_new), pad(v_new), to_lanes(k_cache), to_lanes(v_cache))
    to_rows = lambda c: jnp.transpose(c, (0, 3, 1, 2))
    return o[:n], lse[:n], to_rows(k_out), to_rows(v_out)


def _merge_kernel(x_ref, o0_ref, o1_ref, o2_ref, l0_ref, l1_ref, l2_ref, wo_ref, y_ref):
    l0, l1, l2 = l0_ref[...], l1_ref[...], l2_ref[...]
    m = jnp.maximum(jnp.maximum(l0, l1), l2)
    e0, e1, e2 = jnp.exp(l0 - m), jnp.exp(l1 - m), jnp.exp(l2 - m)
    den = e0 + e1 + e2
    o = (e0 / den) * o0_ref[...] + (e1 / den) * o1_ref[...] + (e2 / den) * o2_ref[...]
    y_ref[...] = x_ref[...] + _bdot(o, wo_ref[...])


def _merge(x, outs, lses, wo, tm):
    m, d = x.shape
    w = wo.shape[0]
    row = lambda i: (i, 0)
    part = pl.BlockSpec((tm, w), row)
    return pl.pallas_call(
        _merge_kernel,
        grid=(m // tm,),
        in_specs=[pl.BlockSpec((tm, d), row)] + [part] * 6 + [_weight_spec(wo)],
        out_specs=pl.BlockSpec((tm, d), row),
        out_shape=jax.ShapeDtypeStruct((m, d), F32),
        compiler_params=pltpu.CompilerParams(
            dimension_semantics=("arbitrary",), vmem_limit_bytes=VMEM_LIMIT),
        name="attn_merge",
    )(x, *outs, *lses, wo)


def kernel(x_prompt, x_sample, state_lru_h, state_conv_a, state_ffn_conv, cache_k0, cache_v0, cache_k1, cache_v1, cache_k2, cache_v2, norm_mix, a_w_in, a_conv_w, a_conv_b, a_gate_a_w, a_gate_a_b, a_gate_x_w, a_gate_x_b, a_lambda, a_w_out, kv_norm, w_kv, b_w_q, b_w_o, norm_ffn, ffn_w_up, ffn_conv_w, ffn_conv_b, ffn_w_down, final_norm):
    b, t, d = x_prompt.shape
    n = x_sample.shape[0]
    assert x_sample.shape[1] == 1 and norm_mix.shape[0] == 2
    caches = (cache_k0, cache_v0, cache_k1, cache_v1, cache_k2, cache_v2)
    row = lambda v: v.reshape(1, -1)

    mixer_w = (row(norm_mix[0]), a_w_in[0].astype(BF16), a_conv_w[0], row(a_conv_b[0]),
               a_gate_a_w[0].astype(BF16), row(a_gate_a_b[0]),
               a_gate_x_w[0].astype(BF16), row(a_gate_x_b[0]),
               row(a_lambda[0]), a_w_out[0].astype(BF16))
    ffn_w = [(row(norm_ffn[l]), ffn_w_up[l].astype(BF16), ffn_conv_w[l],
              row(ffn_conv_b[l]), ffn_w_down[l].astype(BF16)) for l in range(2)]
    ffn_w[1] = ffn_w[1] + (row(final_norm),)
    gq, gkv = row(norm_mix[1]), row(kv_norm)
    wq, wkv, wo = b_w_q[0].astype(BF16), w_kv.astype(BF16), b_w_o[0].astype(BF16)

    h, p_h, p_ca = _mixer_prompt(x_prompt, mixer_w)
    h, p_f0 = _ffn_prompt(h, ffn_w[0], final_norm=False)
    outs, lses, p_kv = [], [], []
    for g, (win, dil) in enumerate(GROUPS):
        col = lambda w_, j: w_[:, j * ATT_W:(j + 1) * ATT_W]
        wk_t, wv_t = col(wkv, g).T, col(wkv, N_GROUPS + g).T
        q, kt, v = _qkv_prompt_group(h, gq, gkv, col(wq, g), col(wkv, N_GROUPS + g), wk_t, dil)
        o, lse = _attn_prompt_group(q, kt, v, g)
        outs.append(o)
        lses.append(lse)
        for tail in _kv_tail(h, gkv, wk_t, wv_t, min(win, t)):
            p_kv.append(jnp.transpose(tail.reshape(b, HEADS, HEAD_DIM, -1), (0, 3, 1, 2)))
    h = _merge_prompt(h, outs, lses, wo)
    y_prompt, p_f1 = _ffn_prompt(h, ffn_w[1], final_norm=True)

    xs = x_sample.reshape(n, d)
    conv_buf = jnp.swapaxes(state_conv_a[0], 0, 1)
    hs, s_h, s_ur = _mixer_sample(xs, state_lru_h[0], conv_buf, mixer_w)
    s_ca = jnp.concatenate([state_conv_a[0][:, 1:], s_ur[:, None]], axis=1)
    hs, s_ug0 = _ffn_sample(hs, jnp.swapaxes(state_ffn_conv[0], 0, 1), ffn_w[0], final_norm=False)
    qs, kvs = _qkv(hs, gq, gkv, wq, wkv, n)
    outs_s, lses_s, s_kv = [], [], []
    for g, (_, dil) in enumerate(GROUPS):
        col = lambda a, j: a[:, j * ATT_W:(j + 1) * ATT_W]
        o, lse, k_out, v_out = _attn_sample_group(
            col(qs, g), col(kvs, g), col(kvs, N_GROUPS + g), caches[2 * g], caches[2 * g + 1], dil)
        outs_s.append(o)
        lses_s.append(lse)
        s_kv += [k_out, v_out]
    hs = _merge(hs, outs_s, lses_s, wo, n)
    y_sample, s_ug1 = _ffn_sample(hs, jnp.swapaxes(state_ffn_conv[1], 0, 1), ffn_w[1], final_norm=True)
    s_ffn = jnp.stack([
        jnp.concatenate([state_ffn_conv[l][:, 1:], ug[:, None]], axis=1)
        for l, ug in enumerate((s_ug0, s_ug1))])
    return (y_prompt, y_sample.reshape(n, 1, d),
            p_h.reshape(1, b, d), s_h.reshape(1, n, d),
            p_ca.reshape(1, b, -1, d), s_ca.reshape(1, n, -1, d),
            jnp.stack([p_f0, p_f1]), s_ffn,
            p_kv[0], p_kv[1], s_kv[0], s_kv[1],
            p_kv[2], p_kv[3], s_kv[2], s_kv[3],
            p_kv[4], p_kv[5], s_kv[4], s_kv[5])
```

```python
import functools

import jax
import jax.numpy as jnp
from jax import lax
from jax.experimental import pallas as pl
from jax.experimental.pallas import tpu as pltpu

F32 = jnp.float32
BF16 = jnp.bfloat16

EPS = 1e-6
LRU_C = 8.0
N_LRU_BLOCKS = 4
HEAD_DIM = 64
HEADS = 8
GROUPS = ((128, 1), (512, 4), (2048, 16))
N_GROUPS = len(GROUPS)
ATT_W = HEADS * HEAD_DIM
ATT_BLK = 128
LANES = 128
CARRY_ROWS = 8

TQ_MIXER = 256
TM_FFN = 512
FC_FFN = 512
TM_TOKEN = 512
VMEM_LIMIT = 52 * 1024 * 1024


def _rmsnorm(x, g):
    return x * lax.rsqrt(jnp.mean(x * x, axis=-1, keepdims=True) + EPS) * g


def _bdot(a, w):
    return jnp.dot(a.astype(BF16), w, preferred_element_type=F32)


def _shift_rows(cur, prev, s):
    rolled = pltpu.roll(cur, s, axis=0)
    prev_rolled = pltpu.roll(prev, s, axis=0)
    row = lax.broadcasted_iota(jnp.int32, prev.shape, 0)
    head = jnp.where(row < s, prev_rolled, rolled[:CARRY_ROWS])
    return jnp.concatenate([head, rolled[CARRY_ROWS:]], axis=0)


def _causal_conv_rows(cur, prev, w_ref, b):
    k = w_ref.shape[0]
    y = b + w_ref[k - 1:k, :] * cur
    for j in range(k - 1):
        y = y + w_ref[j:j + 1, :] * _shift_rows(cur, prev, k - 1 - j)
    return y


def _log_sigmoid(x):
    return jnp.minimum(x, 0.0) - jnp.log1p(jnp.exp(-jnp.abs(x)))


def _lru_coeffs(xc, wa_ref, ba, wx_ref, bx, lam):
    xb = xc.astype(BF16)
    blk = xc.shape[-1] // N_LRU_BLOCKS
    ra, ri = [], []
    for n in range(N_LRU_BLOCKS):
        xs = xb[:, n * blk:(n + 1) * blk]
        ra.append(jnp.dot(xs, wa_ref[n], preferred_element_type=F32))
        ri.append(jnp.dot(xs, wx_ref[n], preferred_element_type=F32))
    r = jax.nn.sigmoid(jnp.concatenate(ra, axis=-1) + ba)
    i = jax.nn.sigmoid(jnp.concatenate(ri, axis=-1) + bx)
    log_a = LRU_C * r * _log_sigmoid(lam)
    a = jnp.exp(log_a)
    one_minus_a2 = (1.0 + a * a) * jnp.tanh(-log_a)
    bt = jnp.sqrt(one_minus_a2) * (i * xc)
    return a, bt


def _scan_rows(a, b, h_init):
    n, d = a.shape
    groups = n // CARRY_ROWS
    a = a.reshape(groups, CARRY_ROWS, d)
    b = b.reshape(groups, CARRY_ROWS, d)
    row = lax.broadcasted_iota(jnp.int32, a.shape, 1)
    s = 1
    while s < CARRY_ROWS:
        keep = row >= s
        a_sh = jnp.where(keep, pltpu.roll(a, s, axis=1), 1.0)
        b_sh = jnp.where(keep, pltpu.roll(b, s, axis=1), 0.0)
        b = a * b_sh + b
        a = a * a_sh
        s *= 2
    h = h_init
    out = []
    for k in range(groups):
        hk = a[k] * h + b[k]
        out.append(hk)
        h = hk[CARRY_ROWS - 1:, :]
    return jnp.concatenate(out, axis=0)


def _mixer_prompt_kernel(x_ref, g_ref, win_ref, cw_ref, cb_ref, wa_ref, ba_ref,
                         wx_ref, bx_ref, lam_ref, wout_ref,
                         y_ref, hlast_ref, cbuf_ref, ucarry, hcarry):
    t = pl.program_id(1)
    tq, d = x_ref.shape

    @pl.when(t == 0)
    def _():
        ucarry[...] = jnp.zeros_like(ucarry)
        hcarry[...] = jnp.zeros_like(hcarry)

    x = x_ref[...]
    u = _bdot(_rmsnorm(x, g_ref[...]), win_ref[...])
    gate = jax.nn.gelu(u[:, :d])
    ur = u[:, d:]
    xc = _causal_conv_rows(ur, ucarry[...], cw_ref, cb_ref[...])
    ucarry[...] = ur[tq - CARRY_ROWS:, :]
    a, bt = _lru_coeffs(xc, wa_ref, ba_ref[...], wx_ref, bx_ref[...], lam_ref[...])
    h = _scan_rows(a, bt, hcarry[...])
    hcarry[...] = h[tq - 1:tq, :]
    y_ref[...] = x + _bdot(h * gate, wout_ref[...])

    @pl.when(t == pl.num_programs(1) - 1)
    def _():
        hlast_ref[...] = h[tq - 1:tq, :]
        k = cw_ref.shape[0]
        cbuf_ref[...] = ur[tq - (k - 1):, :]


def _mixer_sample_kernel(x_ref, h0_ref, buf_ref, g_ref, win_ref, cw_ref, cb_ref,
                         wa_ref, ba_ref, wx_ref, bx_ref, lam_ref, wout_ref,
                         y_ref, hnew_ref, ur_ref):
    d = x_ref.shape[-1]
    k = cw_ref.shape[0]
    x = x_ref[...]
    u = _bdot(_rmsnorm(x, g_ref[...]), win_ref[...])
    gate = jax.nn.gelu(u[:, :d])
    ur = u[:, d:]
    xc = cb_ref[...] + cw_ref[k - 1:k, :] * ur
    for j in range(k - 1):
        xc = xc + cw_ref[j:j + 1, :] * buf_ref[j]
    a, bt = _lru_coeffs(xc, wa_ref, ba_ref[...], wx_ref, bx_ref[...], lam_ref[...])
    h = a * h0_ref[...] + bt
    hnew_ref[...] = h
    ur_ref[...] = ur
    y_ref[...] = x + _bdot(h * gate, wout_ref[...])


def _full_spec(shape):
    zeros = (0,) * len(shape)
    return pl.BlockSpec(shape, lambda *_: zeros)


def _weight_spec(w):
    zeros = (0,) * w.ndim
    return pl.BlockSpec(w.shape, lambda *_: zeros, pipeline_mode=pl.Buffered(1))


def _weight_specs(weights):
    return [_weight_spec(w) for w in weights]


def _mixer_prompt(x, weights):
    b, t, d = x.shape
    tq = TQ_MIXER
    k = weights[2].shape[0]
    row = lambda bi, ti: (bi, ti, 0)
    per_seq = lambda bi, ti: (bi, 0, 0)
    return pl.pallas_call(
        _mixer_prompt_kernel,
        grid=(b, t // tq),
        in_specs=[pl.BlockSpec((None, tq, d), row)] + _weight_specs(weights),
        out_specs=[pl.BlockSpec((None, tq, d), row),
                   pl.BlockSpec((None, 1, d), per_seq),
                   pl.BlockSpec((None, k - 1, d), per_seq)],
        out_shape=[jax.ShapeDtypeStruct((b, t, d), F32),
                   jax.ShapeDtypeStruct((b, 1, d), F32),
                   jax.ShapeDtypeStruct((b, k - 1, d), F32)],
        scratch_shapes=[pltpu.VMEM((CARRY_ROWS, d), F32), pltpu.VMEM((1, d), F32)],
        compiler_params=pltpu.CompilerParams(
            dimension_semantics=("arbitrary", "arbitrary"),
            vmem_limit_bytes=VMEM_LIMIT),
        name="mixer_prompt",
    )(x, *weights)


def _mixer_sample(x, h0, buf, weights):
    n, d = x.shape
    k = weights[2].shape[0]
    return pl.pallas_call(
        _mixer_sample_kernel,
        grid=(1,),
        in_specs=[_full_spec((n, d)), _full_spec((n, d)), _full_spec((k - 1, n, d))]
        + _weight_specs(weights),
        out_specs=[_full_spec((n, d))] * 3,
        out_shape=[jax.ShapeDtypeStruct((n, d), F32)] * 3,
        compiler_params=pltpu.CompilerParams(
            dimension_semantics=("arbitrary",), vmem_limit_bytes=VMEM_LIMIT),
        name="mixer_sample",
    )(x, h0, buf, *weights)


def _ffn_chunks(xb, x, wup_ref, cw_ref, cb_ref, wdown_ref, conv_gate):
    f = wdown_ref.shape[0]
    acc = x
    for lo in range(0, f, FC_FFN):
        hi = lo + FC_FFN
        ug = jnp.dot(xb, wup_ref[:, lo:hi], preferred_element_type=F32)
        uv = jnp.dot(xb, wup_ref[:, f + lo:f + hi], preferred_element_type=F32)
        gc = conv_gate(ug, lo, hi)
        acc = acc + _bdot(jax.nn.gelu(gc) * uv, wdown_ref[lo:hi, :])
    return acc


def _ffn_prompt_kernel(*refs, final_norm):
    if final_norm:
        (x_ref, g_ref, wup_ref, cw_ref, cb_ref, wdown_ref, fn_ref,
         y_ref, fbuf_ref, gcarry) = refs
    else:
        (x_ref, g_ref, wup_ref, cw_ref, cb_ref, wdown_ref,
         y_ref, fbuf_ref, gcarry) = refs
    t = pl.program_id(1)
    tm = x_ref.shape[0]
    k = cw_ref.shape[0]

    @pl.when(t == 0)
    def _():
        gcarry[...] = jnp.zeros_like(gcarry)

    def conv_gate(ug, lo, hi):
        gc = _causal_conv_rows(ug, gcarry[:, lo:hi], cw_ref.at[:, lo:hi], cb_ref[:, lo:hi])
        gcarry[:, lo:hi] = ug[tm - CARRY_ROWS:, :]
        return gc

    x = x_ref[...]
    xb = _rmsnorm(x, g_ref[...]).astype(BF16)
    y = _ffn_chunks(xb, x, wup_ref, cw_ref, cb_ref, wdown_ref, conv_gate)
    if final_norm:
        y = _rmsnorm(y, fn_ref[...])
    y_ref[...] = y

    @pl.when(t == pl.num_programs(1) - 1)
    def _():
        fbuf_ref[...] = gcarry[CARRY_ROWS - (k - 1):, :]


def _ffn_sample_kernel(*refs, final_norm):
    if final_norm:
        (x_ref, buf_ref, g_ref, wup_ref, cw_ref, cb_ref, wdown_ref, fn_ref,
         y_ref, ug_ref) = refs
    else:
        (x_ref, buf_ref, g_ref, wup_ref, cw_ref, cb_ref, wdown_ref,
         y_ref, ug_ref) = refs
    k = cw_ref.shape[0]

    def conv_gate(ug, lo, hi):
        ug_ref[:, lo:hi] = ug
        gc = cb_ref[:, lo:hi] + cw_ref[k - 1:k, lo:hi] * ug
        for j in range(k - 1):
            gc = gc + cw_ref[j:j + 1, lo:hi] * buf_ref[j, :, lo:hi]
        return gc

    x = x_ref[...]
    xb = _rmsnorm(x, g_ref[...]).astype(BF16)
    y = _ffn_chunks(xb, x, wup_ref, cw_ref, cb_ref, wdown_ref, conv_gate)
    if final_norm:
        y = _rmsnorm(y, fn_ref[...])
    y_ref[...] = y


def _ffn_prompt(x, weights, final_norm):
    b, t, d = x.shape
    f = weights[4].shape[0]
    k = weights[2].shape[0]
    tm = TM_FFN
    row = lambda bi, ti: (bi, ti, 0)
    per_seq = lambda bi, ti: (bi, 0, 0)
    return pl.pallas_call(
        functools.partial(_ffn_prompt_kernel, final_norm=final_norm),
        grid=(b, t // tm),
        in_specs=[pl.BlockSpec((None, tm, d), row)] + _weight_specs(weights),
        out_specs=[pl.BlockSpec((None, tm, d), row),
                   pl.BlockSpec((None, k - 1, f), per_seq)],
        out_shape=[jax.ShapeDtypeStruct((b, t, d), F32),
                   jax.ShapeDtypeStruct((b, k - 1, f), F32)],
        scratch_shapes=[pltpu.VMEM((CARRY_ROWS, f), F32)],
        compiler_params=pltpu.CompilerParams(
            dimension_semantics=("arbitrary", "arbitrary"),
            vmem_limit_bytes=VMEM_LIMIT),
        name="ffn_prompt_final" if final_norm else "ffn_prompt",
    )(x, *weights)


def _ffn_sample(x, buf, weights, final_norm):
    n, d = x.shape
    f = weights[4].shape[0]
    k = weights[2].shape[0]
    return pl.pallas_call(
        functools.partial(_ffn_sample_kernel, final_norm=final_norm),
        grid=(1,),
        in_specs=[_full_spec((n, d)), _full_spec((k - 1, n, f))]
        + _weight_specs(weights),
        out_specs=[_full_spec((n, d)), _full_spec((n, f))],
        out_shape=[jax.ShapeDtypeStruct((n, d), F32), jax.ShapeDtypeStruct((n, f), F32)],
        compiler_params=pltpu.CompilerParams(
            dimension_semantics=("arbitrary",), vmem_limit_bytes=VMEM_LIMIT),
        name="ffn_sample_final" if final_norm else "ffn_sample",
    )(x, buf, *weights)


def _qkv_kernel(x_ref, gq_ref, gkv_ref, wq_ref, wkv_ref, q_ref, kv_ref):
    x = x_ref[...]
    inv = lax.rsqrt(jnp.mean(x * x, axis=-1, keepdims=True) + EPS)
    y = x * inv
    q_ref[...] = _bdot(y * gq_ref[...], wq_ref[...])
    kv_ref[...] = _bdot(y * gkv_ref[...], wkv_ref[...])


def _qkv(x, gq, gkv, wq, wkv, tm):
    m, d = x.shape
    nq, nkv = wq.shape[1], wkv.shape[1]
    row = lambda i: (i, 0)
    return pl.pallas_call(
        _qkv_kernel,
        grid=(m // tm,),
        in_specs=[pl.BlockSpec((tm, d), row)] + _weight_specs((gq, gkv, wq, wkv)),
        out_specs=[pl.BlockSpec((tm, nq), row), pl.BlockSpec((tm, nkv), row)],
        out_shape=[jax.ShapeDtypeStruct((m, nq), F32), jax.ShapeDtypeStruct((m, nkv), F32)],
        compiler_params=pltpu.CompilerParams(
            dimension_semantics=("arbitrary",), vmem_limit_bytes=VMEM_LIMIT),
        name="qkv_proj",
    )(x, gq, gkv, wq, wkv)


LSE_REP = ATT_BLK // HEADS


_NT = (((1,), (1,)), ((), ()))


def _qkv_prompt_kernel(x_ref, gq_ref, gkv_ref, wq_ref, wv_ref, wkt_ref,
                       q_ref, kt_ref, v_ref, ybuf):
    dil, per = q_ref.shape[0], q_ref.shape[1]
    x = x_ref[...]
    y = x * lax.rsqrt(jnp.mean(x * x, axis=-1, keepdims=True) + EPS)
    if dil > 1:
        chunks = ybuf.shape[0]
        for c in range(chunks):
            ybuf[c] = y[:, c * LANES:(c + 1) * LANES]
        y = jnp.concatenate(
            [jnp.concatenate([ybuf[c, pl.ds(r, per, stride=dil), :] for c in range(chunks)], axis=1)
             for r in range(dil)], axis=0)
    xq = (y * gq_ref[...]).astype(BF16)
    xkv = (y * gkv_ref[...]).astype(BF16)
    q = jnp.dot(xq, wq_ref[...], preferred_element_type=F32) * (HEAD_DIM ** -0.5)
    q_ref[...] = q.astype(BF16).reshape(dil, per, -1)
    v = jnp.dot(xkv, wv_ref[...], preferred_element_type=F32)
    v_ref[...] = v.astype(BF16).reshape(dil, per, -1)
    kt = lax.dot_general(wkt_ref[...], xkv, _NT, preferred_element_type=F32).astype(BF16)
    for r in range(dil):
        kt_ref[r] = kt[:, r * per:(r + 1) * per]


def _qkv_prompt_group(x, gq, gkv, wq, wv, wkt, dil):
    b, t, d = x.shape
    tm = max(ATT_BLK * dil, TM_TOKEN)
    per = tm // dil
    assert t % tm == 0
    rows = pl.BlockSpec((None, dil, per, ATT_W), lambda bi, ti: (bi, 0, ti, 0))
    cols = pl.BlockSpec((None, dil, ATT_W, per), lambda bi, ti: (bi, 0, 0, ti))
    weights = (gq, gkv, wq, wv, wkt)
    return pl.pallas_call(
        _qkv_prompt_kernel,
        grid=(b, t // tm),
        in_specs=[pl.BlockSpec((None, tm, d), lambda bi, ti: (bi, ti, 0))] + _weight_specs(weights),
        out_specs=[rows, cols, rows],
        out_shape=[jax.ShapeDtypeStruct((b, dil, t // dil, ATT_W), BF16),
                   jax.ShapeDtypeStruct((b, dil, ATT_W, t // dil), BF16),
                   jax.ShapeDtypeStruct((b, dil, t // dil, ATT_W), BF16)],
        scratch_shapes=[pltpu.VMEM((d // LANES, tm, LANES), F32)],
        compiler_params=pltpu.CompilerParams(
            dimension_semantics=("arbitrary", "arbitrary"), vmem_limit_bytes=VMEM_LIMIT),
        name=f"qkv_prompt_d{dil}",
    )(x, *weights)


def _kv_tail_kernel(x_ref, gkv_ref, wkt_ref, wvt_ref, kt_ref, vt_ref):
    x = x_ref[...]
    y = x * lax.rsqrt(jnp.mean(x * x, axis=-1, keepdims=True) + EPS)
    xkv = (y * gkv_ref[...]).astype(BF16)
    kt_ref[...] = lax.dot_general(wkt_ref[...], xkv, _NT, preferred_element_type=F32)
    vt_ref[...] = lax.dot_general(wvt_ref[...], xkv, _NT, preferred_element_type=F32)


def _kv_tail(x, gkv, wkt, wvt, win):
    b, t, d = x.shape
    assert t % win == 0
    out = pl.BlockSpec((None, ATT_W, win), lambda bi: (bi, 0, 0))
    weights = (gkv, wkt, wvt)
    return pl.pallas_call(
        _kv_tail_kernel,
        grid=(b,),
        in_specs=[pl.BlockSpec((None, win, d), lambda bi: (bi, t // win - 1, 0))]
        + _weight_specs(weights),
        out_specs=[out, out],
        out_shape=[jax.ShapeDtypeStruct((b, ATT_W, win), F32)] * 2,
        compiler_params=pltpu.CompilerParams(
            dimension_semantics=("arbitrary",), vmem_limit_bytes=VMEM_LIMIT),
        name=f"kv_tail_w{win}",
    )(x, *weights)


def _attn_prompt_kernel(q_ref, ktp_ref, kt_ref, vp_ref, v_ref, o_ref, lse_ref):
    blk = ktp_ref.shape[1]
    nblk = q_ref.shape[0] // blk
    qi = lax.broadcasted_iota(jnp.int32, (blk, 2 * blk), 0)
    kj = lax.broadcasted_iota(jnp.int32, (blk, 2 * blk), 1)
    band = jnp.logical_or(jnp.logical_and(kj < blk, kj >= qi),
                          jnp.logical_and(kj >= blk, kj - blk <= qi))
    band_first = jnp.logical_and(band, jnp.logical_or(kj >= blk, pl.program_id(2) > 0))
    lane = lax.broadcasted_iota(jnp.int32, (blk, blk), 1)
    low_half = lane < HEAD_DIM
    head_of_lane = lane // LSE_REP
    zero = jnp.zeros((), BF16)
    for j in range(nblk):
        rows = slice(j * blk, (j + 1) * blk)
        q = q_ref[rows, :]
        if j == 0:
            keep = band_first
            kt = jnp.concatenate([ktp_ref[...], kt_ref[:, :blk]], axis=1)
            v = jnp.concatenate([vp_ref[...], v_ref[:blk, :]], axis=0)
        else:
            keep = band
            kt = kt_ref[:, (j - 1) * blk:(j + 1) * blk]
            v = v_ref[(j - 1) * blk:(j + 1) * blk, :]
        lse_tile = jnp.zeros((blk, blk), F32)
        for pair in range(HEADS // 2):
            cols = slice(pair * blk, (pair + 1) * blk)
            q2, kt2, v2 = q[:, cols], kt[cols, :], v[:, cols]
            outs = []
            for half in range(2):
                mine = low_half if half == 0 else jnp.logical_not(low_half)
                s = jnp.dot(jnp.where(mine, q2, zero), kt2, preferred_element_type=F32)
                s = jnp.where(keep, s, -jnp.inf)
                m = jnp.max(s, axis=-1, keepdims=True)
                p = jnp.exp(s - m)
                den = jnp.sum(p, axis=-1, keepdims=True)
                outs.append(jnp.dot(p.astype(BF16), v2, preferred_element_type=F32) / den)
                lse_tile = jnp.where(head_of_lane == 2 * pair + half, m + jnp.log(den), lse_tile)
            o_ref[rows, cols] = jnp.where(low_half, outs[0], outs[1])
        lse_ref[rows, :] = lse_tile


ATT_NBLK = 8


def _attn_prompt_group(q, kt, v, g):
    b, dil, n, _ = q.shape
    nblk = min(ATT_NBLK, n // ATT_BLK)
    tq = nblk * ATT_BLK
    assert n % tq == 0
    prev = lambda i: jnp.maximum(i * nblk - 1, 0)
    rows = lambda size, width: pl.BlockSpec((None, None, size, width), lambda bi, r, i: (bi, r, i, 0))
    return pl.pallas_call(
        _attn_prompt_kernel,
        grid=(b, dil, n // tq),
        in_specs=[rows(tq, ATT_W),
                  pl.BlockSpec((None, None, ATT_W, ATT_BLK), lambda bi, r, i: (bi, r, 0, prev(i))),
                  pl.BlockSpec((None, None, ATT_W, tq), lambda bi, r, i: (bi, r, 0, i)),
                  pl.BlockSpec((None, None, ATT_BLK, ATT_W), lambda bi, r, i: (bi, r, prev(i), 0)),
                  rows(tq, ATT_W)],
        out_specs=[rows(tq, ATT_W), rows(tq, ATT_BLK)],
        out_shape=[jax.ShapeDtypeStruct((b, dil, n, ATT_W), F32),
                   jax.ShapeDtypeStruct((b, dil, n, ATT_BLK), F32)],
        compiler_params=pltpu.CompilerParams(
            dimension_semantics=("arbitrary", "arbitrary", "arbitrary"),
            vmem_limit_bytes=VMEM_LIMIT),
        name=f"attn_prompt_g{g}",
    )(q, kt, kt, v, v)


def _merge_prompt_kernel(x_ref, o0_ref, o1_ref, o2_ref, l0_ref, l1_ref, l2_ref, sel_ref, wo_ref,
                         y_ref, obuf, lbuf):
    def token_order(ref, buf):
        dil, per, width = ref.shape
        if dil == 1:
            return ref[0]
        chunks = width // LANES
        for r in range(dil):
            rows = ref[r]
            for c in range(chunks):
                buf[c, pl.ds(r, per, stride=dil), :] = rows[:, c * LANES:(c + 1) * LANES]
        return jnp.concatenate([buf[c] for c in range(chunks)], axis=1)

    lses = []
    for l_ref in (l0_ref, l1_ref, l2_ref):
        lses.append(token_order(l_ref, lbuf))
    m = jnp.maximum(jnp.maximum(lses[0], lses[1]), lses[2])
    es = [jnp.exp(l - m) for l in lses]
    den = es[0] + es[1] + es[2]
    comb = None
    for e, o_ref in zip(es, (o0_ref, o1_ref, o2_ref)):
        w = e / den
        hi = w.astype(BF16)
        lo = (w - hi.astype(F32)).astype(BF16)
        wide = (jnp.dot(hi, sel_ref[...], preferred_element_type=F32)
                + jnp.dot(lo, sel_ref[...], preferred_element_type=F32))
        term = wide * token_order(o_ref, obuf)
        comb = term if comb is None else comb + term
    y_ref[...] = x_ref[...] + _bdot(comb, wo_ref[...])


def _merge_prompt(x, outs, lses, wo):
    b, t, d = x.shape
    tm = TM_TOKEN
    head_of_lane = jnp.arange(ATT_BLK) // LSE_REP
    first_copy = jnp.arange(ATT_BLK) % LSE_REP == 0
    sel = ((head_of_lane[:, None] == (jnp.arange(ATT_W) // HEAD_DIM)[None, :])
           & first_copy[:, None]).astype(BF16)
    part_specs = []
    for width, parts in ((ATT_W, outs), (ATT_BLK, lses)):
        for p in parts:
            dil = p.shape[1]
            part_specs.append(pl.BlockSpec((None, dil, tm // dil, width),
                                           lambda bi, ti: (bi, 0, ti, 0)))
    row = lambda bi, ti: (bi, ti, 0)
    return pl.pallas_call(
        _merge_prompt_kernel,
        grid=(b, t // tm),
        in_specs=[pl.BlockSpec((None, tm, d), row)] + part_specs + _weight_specs((sel, wo)),
        out_specs=pl.BlockSpec((None, tm, d), row),
        out_shape=jax.ShapeDtypeStruct((b, t, d), F32),
        scratch_shapes=[pltpu.VMEM((ATT_W // LANES, tm, LANES), F32),
                        pltpu.VMEM((ATT_BLK // LANES, tm, LANES), F32)],
        compiler_params=pltpu.CompilerParams(
            dimension_semantics=("arbitrary", "arbitrary"), vmem_limit_bytes=VMEM_LIMIT),
        name="attn_merge_prompt",
    )(x, *outs, *lses, sel, wo)


def _exact_transpose(x):
    c = x.shape[1]
    eye = (lax.broadcasted_iota(jnp.int32, (c, c), 0)
           == lax.broadcasted_iota(jnp.int32, (c, c), 1)).astype(F32)
    return lax.dot_general(eye, x, (((1,), (1,)), ((), ())),
                           precision=lax.Precision.HIGHEST, preferred_element_type=F32)


def _attn_sample_kernel(q_ref, kn_ref, vn_ref, k_ref, v_ref,
                        o_ref, lse_ref, ko_ref, vo_ref,
                        qc_ref, kc_ref, vc_ref, oc_ref, lc_ref, *, dil, hb):
    step = pl.program_id(0)
    bb, heads, dh, w = k_ref.shape
    npad = q_ref.shape[0]

    @pl.when(step == 0)
    def _():
        qc_ref[...] = _exact_transpose(q_ref[...]) * (HEAD_DIM ** -0.5)
        kc_ref[...] = _exact_transpose(kn_ref[...])
        vc_ref[...] = _exact_transpose(vn_ref[...])
        oc_ref[...] = jnp.zeros_like(oc_ref)
        lc_ref[...] = jnp.zeros_like(lc_ref)

    pos = lax.broadcasted_iota(jnp.int32, (1, w), 1)
    attended = (pos & (dil - 1)) == 0
    newest = pos == w - 1
    seq_lane = lax.broadcasted_iota(jnp.int32, (1, npad), 1)

    def shifted(tile, new_col):
        flat = pltpu.roll(tile.reshape(hb * dh, w), w - 1, axis=1)
        return jnp.where(newest, new_col, flat.reshape(hb, dh, w))

    for i in range(bb):
        mine = seq_lane == step * bb + i

        def column(ref, rows):
            col = jnp.sum(jnp.where(mine, ref[rows, :], 0.0), axis=-1, keepdims=True)
            return col.reshape(hb, dh, 1)

        def head_group(grp, carry):
            if hb == heads:
                rows, hs = slice(None), slice(None)
            else:
                rows = pl.ds(pl.multiple_of(grp * (hb * dh), hb * dh), hb * dh)
                hs = pl.ds(grp * hb, hb)
            qc, kc, vc = column(qc_ref, rows), column(kc_ref, rows), column(vc_ref, rows)
            kt = k_ref[i, hs]
            vt = v_ref[i, hs]
            s = jnp.where(attended, jnp.sum(kt * qc, axis=1, keepdims=True), -jnp.inf)
            s_new = jnp.sum(kc * qc, axis=1, keepdims=True)
            m = jnp.maximum(jnp.max(s, axis=-1, keepdims=True), s_new)
            p = jnp.exp(s - m)
            p_new = jnp.exp(s_new - m)
            den = jnp.sum(p, axis=-1, keepdims=True) + p_new
            o = (jnp.sum(vt * p, axis=-1, keepdims=True) + p_new * vc) / den
            lse = jnp.broadcast_to(m + jnp.log(den), (hb, dh, 1))
            oc_ref[rows, :] = jnp.where(mine, o.reshape(hb * dh, 1), oc_ref[rows, :])
            lc_ref[rows, :] = jnp.where(mine, lse.reshape(hb * dh, 1), lc_ref[rows, :])
            ko_ref[i, hs] = shifted(kt, kc)
            vo_ref[i, hs] = shifted(vt, vc)
            return carry

        if hb == heads:
            head_group(0, 0)
        else:
            lax.fori_loop(0, heads // hb, head_group, 0)

    @pl.when(step == pl.num_programs(0) - 1)
    def _():
        o_ref[...] = _exact_transpose(oc_ref[...])
        lse_ref[...] = _exact_transpose(lc_ref[...])


SAMPLE_PAD = 128
SAMPLE_BB = {128: 8, 512: 4, 2048: 1}
SAMPLE_HB = {128: 8, 512: 8, 2048: 1}


def _attn_sample_group(q, k_new, v_new, k_cache, v_cache, dil):
    n, w = k_cache.shape[:2]
    assert w % dil == 0 and w // dil == ATT_BLK
    bb = SAMPLE_BB[w]
    pad = lambda x: jnp.pad(x, ((0, SAMPLE_PAD - n), (0, 0)))
    to_lanes = lambda c: jnp.transpose(c, (0, 2, 3, 1))
    cache_spec = pl.BlockSpec((bb, HEADS, HEAD_DIM, w), lambda i: (i, 0, 0, 0))
    vec_spec = _full_spec((SAMPLE_PAD, ATT_W))
    cache_shape = jax.ShapeDtypeStruct((n, HEADS, HEAD_DIM, w), F32)
    vec_shape = jax.ShapeDtypeStruct((SAMPLE_PAD, ATT_W), F32)
    o, lse, k_out, v_out = pl.pallas_call(
        functools.partial(_attn_sample_kernel, dil=dil, hb=SAMPLE_HB[w]),
        grid=(n // bb,),
        in_specs=[vec_spec] * 3 + [cache_spec] * 2,
        out_specs=[vec_spec] * 2 + [cache_spec] * 2,
        out_shape=[vec_shape] * 2 + [cache_shape] * 2,
        scratch_shapes=[pltpu.VMEM((ATT_W, SAMPLE_PAD), F32)] * 5,
        compiler_params=pltpu.CompilerParams(
            dimension_semantics=("arbitrary",), vmem_limit_bytes=VMEM_LIMIT),
        name=f"attn_sample_d{dil}",
    )(pad(q), pad(k_new), pad(v_new), to_lanes(k_cache), to_lanes(v_cache))
    to_rows = lambda c: jnp.transpose(c, (0, 3, 1, 2))
    return o[:n], lse[:n], to_rows(k_out), to_rows(v_out)


def _merge_kernel(x_ref, o0_ref, o1_ref, o2_ref, l0_ref, l1_ref, l2_ref, wo_ref, y_ref):
    l0, l1, l2 = l0_ref[...], l1_ref[...], l2_ref[...]
    m = jnp.maximum(jnp.maximum(l0, l1), l2)
    e0, e1, e2 = jnp.exp(l0 - m), jnp.exp(l1 - m), jnp.exp(l2 - m)
    den = e0 + e1 + e2
    o = (e0 / den) * o0_ref[...] + (e1 / den) * o1_ref[...] + (e2 / den) * o2_ref[...]
    y_ref[...] = x_ref[...] + _bdot(o, wo_ref[...])


def _merge(x, outs, lses, wo, tm):
    m, d = x.shape
    w = wo.shape[0]
    row = lambda i: (i, 0)
    part = pl.BlockSpec((tm, w), row)
    return pl.pallas_call(
        _merge_kernel,
        grid=(m // tm,),
        in_specs=[pl.BlockSpec((tm, d), row)] + [part] * 6 + [_weight_spec(wo)],
        out_specs=pl.BlockSpec((tm, d), row),
        out_shape=jax.ShapeDtypeStruct((m, d), F32),
        compiler_params=pltpu.CompilerParams(
            dimension_semantics=("arbitrary",), vmem_limit_bytes=VMEM_LIMIT),
        name="attn_merge",
    )(x, *outs, *lses, wo)


def kernel(x_prompt, x_sample, state_lru_h, state_conv_a, state_ffn_conv, cache_k0, cache_v0, cache_k1, cache_v1, cache_k2, cache_v2, norm_mix, a_w_in, a_conv_w, a_conv_b, a_gate_a_w, a_gate_a_b, a_gate_x_w, a_gate_x_b, a_lambda, a_w_out, kv_norm, w_kv, b_w_q, b_w_o, norm_ffn, ffn_w_up, ffn_conv_w, ffn_conv_b, ffn_w_down, final_norm):
    b, t, d = x_prompt.shape
    n = x_sample.shape[0]
    assert x_sample.shape[1] == 1 and norm_mix.shape[0] == 2
    caches = (cache_k0, cache_v0, cache_k1, cache_v1, cache_k2, cache_v2)
    row = lambda v: v.reshape(1, -1)

    mixer_w = (row(norm_mix[0]), a_w_in[0].astype(BF16), a_conv_w[0], row(a_conv_b[0]),
               a_gate_a_w[0].astype(BF16), row(a_gate_a_b[0]),
               a_gate_x_w[0].astype(BF16), row(a_gate_x_b[0]),
               row(a_lambda[0]), a_w_out[0].astype(BF16))
    ffn_w = [(row(norm_ffn[l]), ffn_w_up[l].astype(BF16), ffn_conv_w[l],
              row(ffn_conv_b[l]), ffn_w_down[l].astype(BF16)) for l in range(2)]
    ffn_w[1] = ffn_w[1] + (row(final_norm),)
    gq, gkv = row(norm_mix[1]), row(kv_norm)
    wq, wkv, wo = b_w_q[0].astype(BF16), w_kv.astype(BF16), b_w_o[0].astype(BF16)

    h, p_h, p_ca = _mixer_prompt(x_prompt, mixer_w)
    h, p_f0 = _ffn_prompt(h, ffn_w[0], final_norm=False)
    outs, lses, p_kv = [], [], []
    for g, (win, dil) in enumerate(GROUPS):
        col = lambda w_, j: w_[:, j * ATT_W:(j + 1) * ATT_W]
        wk_t, wv_t = col(wkv, g).T, col(wkv, N_GROUPS + g).T
        q, kt, v = _qkv_prompt_group(h, gq, gkv, col(wq, g), col(wkv, N_GROUPS + g), wk_t, dil)
        o, lse = _attn_prompt_group(q, kt, v, g)
        outs.append(o)
        lses.append(lse)
        for tail in _kv_tail(h, gkv, wk_t, wv_t, min(win, t)):
            p_kv.append(jnp.transpose(tail.reshape(b, HEADS, HEAD_DIM, -1), (0, 3, 1, 2)))
    h = _merge_prompt(h, outs, lses, wo)
    y_prompt, p_f1 = _ffn_prompt(h, ffn_w[1], final_norm=True)

    xs = x_sample.reshape(n, d)
    conv_buf = jnp.swapaxes(state_conv_a[0], 0, 1)
    hs, s_h, s_ur = _mixer_sample(xs, state_lru_h[0], conv_buf, mixer_w)
    s_ca = jnp.concatenate([state_conv_a[0][:, 1:], s_ur[:, None]], axis=1)
    hs, s_ug0 = _ffn_sample(hs, jnp.swapaxes(state_ffn_conv[0], 0, 1), ffn_w[0], final_norm=False)
    qs, kvs = _qkv(hs, gq, gkv, wq, wkv, n)
    outs_s, lses_s, s_kv = [], [], []
    for g, (_, dil) in enumerate(GROUPS):
        col = lambda a, j: a[:, j * ATT_W:(j + 1) * ATT_W]
        o, lse, k_out, v_out = _attn_sample_group(
            col(qs, g), col(kvs, g), col(kvs, N_GROUPS + g), caches[2 * g], caches[2 * g + 1], dil)
        outs_s.append(o)
        lses_s.append(lse)
        s_kv += [k_out, v_out]
    hs = _merge(hs, outs_s, lses_s, wo, n)
    y_sample, s_ug1 = _ffn_sample(hs, jnp.swapaxes(state_ffn_conv[1], 0, 1), ffn_w[1], final_norm=True)
    s_ffn = jnp.stack([
        jnp.concatenate([state_ffn_conv[l][:, 1:], ug[:, None]], axis=1)
        for l, ug in enumerate((s_ug0, s_ug1))])
    return (y_prompt, y_sample.reshape(n, 1, d),
            p_h.reshape(1, b, d), s_h.reshape(1, n, d),
            p_ca.reshape(1, b, -1, d), s_ca.reshape(1, n, -1, d),
            jnp.stack([p_f0, p_f1]), s_ffn,
            p_kv[0], p_kv[1], s_kv[0], s_kv[1],
            p_kv[2], p_kv[3], s_kv[2], s_kv[3],
            p_kv[4], p_kv[5], s_kv[4], s_kv[5])
```

```python
import functools

import jax
import jax.numpy as jnp
from jax import lax
from jax.experimental import pallas as pl
from jax.experimental.pallas import tpu as pltpu

F32 = jnp.float32
BF16 = jnp.bfloat16

EPS = 1e-6
LRU_C = 8.0
N_LRU_BLOCKS = 4
HEAD_DIM = 64
HEADS = 8
GROUPS = ((128, 1), (512, 4), (2048, 16))
N_GROUPS = len(GROUPS)
ATT_W = HEADS * HEAD_DIM
ATT_BLK = 128
LANES = 128
CARRY_ROWS = 8

TQ_MIXER = 256
TM_FFN = 512
FC_FFN = 512
TM_TOKEN = 512
VMEM_LIMIT = 52 * 1024 * 1024


def _rmsnorm(x, g):
    return x * lax.rsqrt(jnp.mean(x * x, axis=-1, keepdims=True) + EPS) * g


def _bdot(a, w):
    return jnp.dot(a.astype(BF16), w, preferred_element_type=F32)


def _shift_rows(cur, prev, s):
    rolled = pltpu.roll(cur, s, axis=0)
    prev_rolled = pltpu.roll(prev, s, axis=0)
    row = lax.broadcasted_iota(jnp.int32, prev.shape, 0)
    head = jnp.where(row < s, prev_rolled, rolled[:CARRY_ROWS])
    return jnp.concatenate([head, rolled[CARRY_ROWS:]], axis=0)


def _causal_conv_rows(cur, prev, w_ref, b):
    k = w_ref.shape[0]
    y = b + w_ref[k - 1:k, :] * cur
    for j in range(k - 1):
        y = y + w_ref[j:j + 1, :] * _shift_rows(cur, prev, k - 1 - j)
    return y


def _log_sigmoid(x):
    return jnp.minimum(x, 0.0) - jnp.log1p(jnp.exp(-jnp.abs(x)))


def _lru_coeffs(xc, wa_ref, ba, wx_ref, bx, lam):
    xb = xc.astype(BF16)
    blk = xc.shape[-1] // N_LRU_BLOCKS
    ra, ri = [], []
    for n in range(N_LRU_BLOCKS):
        xs = xb[:, n * blk:(n + 1) * blk]
        ra.append(jnp.dot(xs, wa_ref[n], preferred_element_type=F32))
        ri.append(jnp.dot(xs, wx_ref[n], preferred_element_type=F32))
    r = jax.nn.sigmoid(jnp.concatenate(ra, axis=-1) + ba)
    i = jax.nn.sigmoid(jnp.concatenate(ri, axis=-1) + bx)
    log_a = LRU_C * r * _log_sigmoid(lam)
    a = jnp.exp(log_a)
    one_minus_a2 = (1.0 + a * a) * jnp.tanh(-log_a)
    bt = jnp.sqrt(one_minus_a2) * (i * xc)
    return a, bt


def _scan_rows(a, b, h_init):
    n, d = a.shape
    groups = n // CARRY_ROWS
    a = a.reshape(groups, CARRY_ROWS, d)
    b = b.reshape(groups, CARRY_ROWS, d)
    row = lax.broadcasted_iota(jnp.int32, a.shape, 1)
    s = 1
    while s < CARRY_ROWS:
        keep = row >= s
        a_sh = jnp.where(keep, pltpu.roll(a, s, axis=1), 1.0)
        b_sh = jnp.where(keep, pltpu.roll(b, s, axis=1), 0.0)
        b = a * b_sh + b
        a = a * a_sh
        s *= 2
    h = h_init
    out = []
    for k in range(groups):
        hk = a[k] * h + b[k]
        out.append(hk)
        h = hk[CARRY_ROWS - 1:, :]
    return jnp.concatenate(out, axis=0)


def _mixer_prompt_kernel(x_ref, g_ref, win_ref, cw_ref, cb_ref, wa_ref, ba_ref,
                         wx_ref, bx_ref, lam_ref, wout_ref,
                         y_ref, hlast_ref, cbuf_ref, ucarry, hcarry):
    t = pl.program_id(1)
    tq, d = x_ref.shape

    @pl.when(t == 0)
    def _():
        ucarry[...] = jnp.zeros_like(ucarry)
        hcarry[...] = jnp.zeros_like(hcarry)

    x = x_ref[...]
    u = _bdot(_rmsnorm(x, g_ref[...]), win_ref[...])
    gate = jax.nn.gelu(u[:, :d])
    ur = u[:, d:]
    xc = _causal_conv_rows(ur, ucarry[...], cw_ref, cb_ref[...])
    ucarry[...] = ur[tq - CARRY_ROWS:, :]
    a, bt = _lru_coeffs(xc, wa_ref, ba_ref[...], wx_ref, bx_ref[...], lam_ref[...])
    h = _scan_rows(a, bt, hcarry[...])
    hcarry[...] = h[tq - 1:tq, :]
    y_ref[...] = x + _bdot(h * gate, wout_ref[...])

    @pl.when(t == pl.num_programs(1) - 1)
    def _():
        hlast_ref[...] = h[tq - 1:tq, :]
        k = cw_ref.shape[0]
        cbuf_ref[...] = ur[tq - (k - 1):, :]


def _mixer_sample_kernel(x_ref, h0_ref, buf_ref, g_ref, win_ref, cw_ref, cb_ref,
                         wa_ref, ba_ref, wx_ref, bx_ref, lam_ref, wout_ref,
                         y_ref, hnew_ref, ur_ref):
    d = x_ref.shape[-1]
    k = cw_ref.shape[0]
    x = x_ref[...]
    u = _bdot(_rmsnorm(x, g_ref[...]), win_ref[...])
    gate = jax.nn.gelu(u[:, :d])
    ur = u[:, d:]
    xc = cb_ref[...] + cw_ref[k - 1:k, :] * ur
    for j in range(k - 1):
        xc = xc + cw_ref[j:j + 1, :] * buf_ref[j]
    a, bt = _lru_coeffs(xc, wa_ref, ba_ref[...], wx_ref, bx_ref[...], lam_ref[...])
    h = a * h0_ref[...] + bt
    hnew_ref[...] = h
    ur_ref[...] = ur
    y_ref[...] = x + _bdot(h * gate, wout_ref[...])


def _full_spec(shape):
    zeros = (0,) * len(shape)
    return pl.BlockSpec(shape, lambda *_: zeros)


def _weight_spec(w):
    zeros = (0,) * w.ndim
    return pl.BlockSpec(w.shape, lambda *_: zeros, pipeline_mode=pl.Buffered(1))


def _weight_specs(weights):
    return [_weight_spec(w) for w in weights]


def _mixer_prompt(x, weights):
    b, t, d = x.shape
    tq = TQ_MIXER
    k = weights[2].shape[0]
    row = lambda bi, ti: (bi, ti, 0)
    per_seq = lambda bi, ti: (bi, 0, 0)
    return pl.pallas_call(
        _mixer_prompt_kernel,
        grid=(b, t // tq),
        in_specs=[pl.BlockSpec((None, tq, d), row)] + _weight_specs(weights),
        out_specs=[pl.BlockSpec((None, tq, d), row),
                   pl.BlockSpec((None, 1, d), per_seq),
                   pl.BlockSpec((None, k - 1, d), per_seq)],
        out_shape=[jax.ShapeDtypeStruct((b, t, d), F32),
                   jax.ShapeDtypeStruct((b, 1, d), F32),
                   jax.ShapeDtypeStruct((b, k - 1, d), F32)],
        scratch_shapes=[pltpu.VMEM((CARRY_ROWS, d), F32), pltpu.VMEM((1, d), F32)],
        compiler_params=pltpu.CompilerParams(
            dimension_semantics=("arbitrary", "arbitrary"),
            vmem_limit_bytes=VMEM_LIMIT),
        name="mixer_prompt",
    )(x, *weights)


def _mixer_sample(x, h0, buf, weights):
    n, d = x.shape
    k = weights[2].shape[0]
    return pl.pallas_call(
        _mixer_sample_kernel,
        grid=(1,),
        in_specs=[_full_spec((n, d)), _full_spec((n, d)), _full_spec((k - 1, n, d))]
        + _weight_specs(weights),
        out_specs=[_full_spec((n, d))] * 3,
        out_shape=[jax.ShapeDtypeStruct((n, d), F32)] * 3,
        compiler_params=pltpu.CompilerParams(
            dimension_semantics=("arbitrary",), vmem_limit_bytes=VMEM_LIMIT),
        name="mixer_sample",
    )(x, h0, buf, *weights)


def _ffn_chunks(xb, x, wup_ref, cw_ref, cb_ref, wdown_ref, conv_gate, act_ref=None, fc=FC_FFN):
    f = wdown_ref.shape[0]
    acc = x
    for lo in range(0, f, fc):
        hi = lo + fc
        ug = jnp.dot(xb, wup_ref[:, lo:hi], preferred_element_type=F32)
        uv = jnp.dot(xb, wup_ref[:, f + lo:f + hi], preferred_element_type=F32)
        gc = conv_gate(ug, lo, hi)
        act = (jax.nn.gelu(gc) * uv).astype(BF16)
        if act_ref is None:
            acc = acc + jnp.dot(act, wdown_ref[lo:hi, :], preferred_element_type=F32)
        else:
            act_ref[:, lo:hi] = act
    if act_ref is not None:
        acc = acc + jnp.dot(act_ref[...], wdown_ref[...], preferred_element_type=F32)
    return acc


def _ffn_prompt_kernel(*refs, final_norm, fc):
    if final_norm:
        (x_ref, g_ref, wup_ref, cw_ref, cb_ref, wdown_ref, fn_ref,
         y_ref, fbuf_ref, gcarry, act_ref) = refs
    else:
        (x_ref, g_ref, wup_ref, cw_ref, cb_ref, wdown_ref,
         y_ref, fbuf_ref, gcarry, act_ref) = refs
    t = pl.program_id(1)
    tm = x_ref.shape[0]
    k = cw_ref.shape[0]

    @pl.when(t == 0)
    def _():
        gcarry[...] = jnp.zeros_like(gcarry)

    def conv_gate(ug, lo, hi):
        gc = _causal_conv_rows(ug, gcarry[:, lo:hi], cw_ref.at[:, lo:hi], cb_ref[:, lo:hi])
        gcarry[:, lo:hi] = ug[tm - CARRY_ROWS:, :]
        return gc

    x = x_ref[...]
    xb = _rmsnorm(x, g_ref[...]).astype(BF16)
    y = _ffn_chunks(xb, x, wup_ref, cw_ref, cb_ref, wdown_ref, conv_gate, act_ref, fc)
    if final_norm:
        y = _rmsnorm(y, fn_ref[...])
    y_ref[...] = y

    @pl.when(t == pl.num_programs(1) - 1)
    def _():
        fbuf_ref[...] = gcarry[CARRY_ROWS - (k - 1):, :]


def _ffn_sample_kernel(*refs, final_norm):
    if final_norm:
        (x_ref, buf_ref, g_ref, wup_ref, cw_ref, cb_ref, wdown_ref, fn_ref,
         y_ref, ug_ref) = refs
    else:
        (x_ref, buf_ref, g_ref, wup_ref, cw_ref, cb_ref, wdown_ref,
         y_ref, ug_ref) = refs
    k = cw_ref.shape[0]

    def conv_gate(ug, lo, hi):
        ug_ref[:, lo:hi] = ug
        gc = cb_ref[:, lo:hi] + cw_ref[k - 1:k, lo:hi] * ug
        for j in range(k - 1):
            gc = gc + cw_ref[j:j + 1, lo:hi] * buf_ref[j, :, lo:hi]
        return gc

    x = x_ref[...]
    xb = _rmsnorm(x, g_ref[...]).astype(BF16)
    y = _ffn_chunks(xb, x, wup_ref, cw_ref, cb_ref, wdown_ref, conv_gate)
    if final_norm:
        y = _rmsnorm(y, fn_ref[...])
    y_ref[...] = y


def _ffn_prompt(x, weights, final_norm):
    b, t, d = x.shape
    f = weights[4].shape[0]
    k = weights[2].shape[0]
    tm = TM_FFN
    row = lambda bi, ti: (bi, ti, 0)
    per_seq = lambda bi, ti: (bi, 0, 0)
    return pl.pallas_call(
        functools.partial(_ffn_prompt_kernel, final_norm=final_norm, fc=FC_FFN),
        grid=(b, t // tm),
        in_specs=[pl.BlockSpec((None, tm, d), row)] + _weight_specs(weights),
        out_specs=[pl.BlockSpec((None, tm, d), row),
                   pl.BlockSpec((None, k - 1, f), per_seq)],
        out_shape=[jax.ShapeDtypeStruct((b, t, d), F32),
                   jax.ShapeDtypeStruct((b, k - 1, f), F32)],
        scratch_shapes=[pltpu.VMEM((CARRY_ROWS, f), F32), pltpu.VMEM((tm, f), BF16)],
        compiler_params=pltpu.CompilerParams(
            dimension_semantics=("arbitrary", "arbitrary"),
            vmem_limit_bytes=VMEM_LIMIT),
        name="ffn_prompt_final" if final_norm else "ffn_prompt",
    )(x, *weights)


def _ffn_sample(x, buf, weights, final_norm):
    n, d = x.shape
    f = weights[4].shape[0]
    k = weights[2].shape[0]
    return pl.pallas_call(
        functools.partial(_ffn_sample_kernel, final_norm=final_norm),
        grid=(1,),
        in_specs=[_full_spec((n, d)), _full_spec((k - 1, n, f))]
        + _weight_specs(weights),
        out_specs=[_full_spec((n, d)), _full_spec((n, f))],
        out_shape=[jax.ShapeDtypeStruct((n, d), F32), jax.ShapeDtypeStruct((n, f), F32)],
        compiler_params=pltpu.CompilerParams(
            dimension_semantics=("arbitrary",), vmem_limit_bytes=VMEM_LIMIT),
        name="ffn_sample_final" if final_norm else "ffn_sample",
    )(x, buf, *weights)


def _qkv_kernel(x_ref, gq_ref, gkv_ref, wq_ref, wkv_ref, q_ref, kv_ref):
    x = x_ref[...]
    inv = lax.rsqrt(jnp.mean(x * x, axis=-1, keepdims=True) + EPS)
    y = x * inv
    q_ref[...] = _bdot(y * gq_ref[...], wq_ref[...])
    kv_ref[...] = _bdot(y * gkv_ref[...], wkv_ref[...])


def _qkv(x, gq, gkv, wq, wkv, tm):
    m, d = x.shape
    nq, nkv = wq.shape[1], wkv.shape[1]
    row = lambda i: (i, 0)
    return pl.pallas_call(
        _qkv_kernel,
        grid=(m // tm,),
        in_specs=[pl.BlockSpec((tm, d), row)] + _weight_specs((gq, gkv, wq, wkv)),
        out_specs=[pl.BlockSpec((tm, nq), row), pl.BlockSpec((tm, nkv), row)],
        out_shape=[jax.ShapeDtypeStruct((m, nq), F32), jax.ShapeDtypeStruct((m, nkv), F32)],
        compiler_params=pltpu.CompilerParams(
            dimension_semantics=("arbitrary",), vmem_limit_bytes=VMEM_LIMIT),
        name="qkv_proj",
    )(x, gq, gkv, wq, wkv)


LSE_REP = ATT_BLK // HEADS


_NT = (((1,), (1,)), ((), ()))


def _qkv_prompt_kernel(x_ref, gq_ref, gkv_ref, wq_ref, wkv_ref, *out_and_scratch):
    out_refs, ybuf = out_and_scratch[:-1], out_and_scratch[-1]
    x = x_ref[...]
    y = x * lax.rsqrt(jnp.mean(x * x, axis=-1, keepdims=True) + EPS)
    chunks = ybuf.shape[0]
    for c in range(chunks):
        ybuf[c] = y[:, c * LANES:(c + 1) * LANES]
    for g in range(N_GROUPS):
        q_ref, k_ref, v_ref = out_refs[3 * g:3 * g + 3]
        dil, per = q_ref.shape[0], q_ref.shape[1]
        yg = y
        if dil > 1:
            yg = jnp.concatenate(
                [jnp.concatenate([ybuf[c, pl.ds(r, per, stride=dil), :] for c in range(chunks)], axis=1)
                 for r in range(dil)], axis=0)
        xq = (yg * gq_ref[...]).astype(BF16)
        xkv = (yg * gkv_ref[...]).astype(BF16)
        q = jnp.dot(xq, wq_ref[g], preferred_element_type=F32) * (HEAD_DIM ** -0.5)
        kv = jnp.dot(xkv, wkv_ref[g], preferred_element_type=F32)
        q_ref[...] = q.astype(BF16).reshape(dil, per, ATT_W)
        k_ref[...] = kv[:, :ATT_W].astype(BF16).reshape(dil, per, ATT_W)
        v_ref[...] = kv[:, ATT_W:].astype(BF16).reshape(dil, per, ATT_W)


def _qkv_prompt(x, gq, gkv, wq, wkv):
    b, t, d = x.shape
    tm = TM_TOKEN
    specs, shapes = [], []
    for _, dil in GROUPS:
        assert tm % (dil * 16) == 0
        specs += [pl.BlockSpec((None, dil, tm // dil, ATT_W), lambda bi, ti: (bi, 0, ti, 0))] * 3
        shapes += [jax.ShapeDtypeStruct((b, dil, t // dil, ATT_W), BF16)] * 3
    weights = (gq, gkv, wq, wkv)
    outs = pl.pallas_call(
        _qkv_prompt_kernel,
        grid=(b, t // tm),
        in_specs=[pl.BlockSpec((None, tm, d), lambda bi, ti: (bi, ti, 0))] + _weight_specs(weights),
        out_specs=specs,
        out_shape=shapes,
        scratch_shapes=[pltpu.VMEM((d // LANES, tm, LANES), F32)],
        compiler_params=pltpu.CompilerParams(
            dimension_semantics=("arbitrary", "arbitrary"), vmem_limit_bytes=VMEM_LIMIT),
        name="qkv_prompt",
    )(x, *weights)
    return [outs[3 * g:3 * g + 3] for g in range(N_GROUPS)]


def _kv_tail_kernel(x_ref, gkv_ref, wkt_ref, wvt_ref, kt_ref, vt_ref):
    x = x_ref[...]
    y = x * lax.rsqrt(jnp.mean(x * x, axis=-1, keepdims=True) + EPS)
    xkv = (y * gkv_ref[...]).astype(BF16)
    kt_ref[...] = lax.dot_general(wkt_ref[...], xkv, _NT, preferred_element_type=F32)
    vt_ref[...] = lax.dot_general(wvt_ref[...], xkv, _NT, preferred_element_type=F32)


def _kv_tail(x, gkv, wkt, wvt, win):
    b, t, d = x.shape
    assert t % win == 0
    out = pl.BlockSpec((None, ATT_W, win), lambda bi: (bi, 0, 0))
    weights = (gkv, wkt, wvt)
    return pl.pallas_call(
        _kv_tail_kernel,
        grid=(b,),
        in_specs=[pl.BlockSpec((None, win, d), lambda bi: (bi, t // win - 1, 0))]
        + _weight_specs(weights),
        out_specs=[out, out],
        out_shape=[jax.ShapeDtypeStruct((b, ATT_W, win), F32)] * 2,
        compiler_params=pltpu.CompilerParams(
            dimension_semantics=("arbitrary",), vmem_limit_bytes=VMEM_LIMIT),
        name=f"kv_tail_w{win}",
    )(x, *weights)


def _attn_prompt_kernel(q_ref, kp_ref, k_ref, vp_ref, v_ref, o_ref, lse_ref):
    blk = kp_ref.shape[0]
    nblk = q_ref.shape[0] // blk
    qi = lax.broadcasted_iota(jnp.int32, (blk, 2 * blk), 0)
    kj = lax.broadcasted_iota(jnp.int32, (blk, 2 * blk), 1)
    band = jnp.logical_or(jnp.logical_and(kj < blk, kj >= qi),
                          jnp.logical_and(kj >= blk, kj - blk <= qi))
    band_first = jnp.logical_and(band, jnp.logical_or(kj >= blk, pl.program_id(2) > 0))
    lane = lax.broadcasted_iota(jnp.int32, (blk, blk), 1)
    low_half = lane < HEAD_DIM
    head_of_lane = lane // LSE_REP
    zero = jnp.zeros((), BF16)
    for j in range(nblk):
        rows = slice(j * blk, (j + 1) * blk)
        q = q_ref[rows, :]
        if j == 0:
            keep = band_first
            k = jnp.concatenate([kp_ref[...], k_ref[:blk, :]], axis=0)
            v = jnp.concatenate([vp_ref[...], v_ref[:blk, :]], axis=0)
        else:
            keep = band
            k = k_ref[(j - 1) * blk:(j + 1) * blk, :]
            v = v_ref[(j - 1) * blk:(j + 1) * blk, :]
        lse_tile = jnp.zeros((blk, blk), F32)
        for pair in range(HEADS // 2):
            cols = slice(pair * blk, (pair + 1) * blk)
            q2, k2, v2 = q[:, cols], k[:, cols], v[:, cols]
            outs = []
            for half in range(2):
                mine = low_half if half == 0 else jnp.logical_not(low_half)
                s = lax.dot_general(jnp.where(mine, q2, zero), k2, _NT, preferred_element_type=F32)
                s = jnp.where(keep, s, -jnp.inf)
                m = jnp.max(s, axis=-1, keepdims=True)
                p = jnp.exp(s - m)
                den = jnp.sum(p, axis=-1, keepdims=True)
                outs.append(jnp.dot(p.astype(BF16), v2, preferred_element_type=F32) / den)
                lse_tile = jnp.where(head_of_lane == 2 * pair + half, m + jnp.log(den), lse_tile)
            o_ref[rows, cols] = jnp.where(low_half, outs[0], outs[1])
        lse_ref[rows, :] = lse_tile


ATT_NBLK = 8


def _attn_prompt_group(q, k, v, g):
    b, dil, n, _ = q.shape
    nblk = min(ATT_NBLK, n // ATT_BLK)
    tq = nblk * ATT_BLK
    assert n % tq == 0
    prev = lambda i: jnp.maximum(i * nblk - 1, 0)
    rows = lambda size, width: pl.BlockSpec((None, None, size, width), lambda bi, r, i: (bi, r, i, 0))
    before = pl.BlockSpec((None, None, ATT_BLK, ATT_W), lambda bi, r, i: (bi, r, prev(i), 0))
    return pl.pallas_call(
        _attn_prompt_kernel,
        grid=(b, dil, n // tq),
        in_specs=[rows(tq, ATT_W), before, rows(tq, ATT_W), before, rows(tq, ATT_W)],
        out_specs=[rows(tq, ATT_W), rows(tq, ATT_BLK)],
        out_shape=[jax.ShapeDtypeStruct((b, dil, n, ATT_W), F32),
                   jax.ShapeDtypeStruct((b, dil, n, ATT_BLK), F32)],
        compiler_params=pltpu.CompilerParams(
            dimension_semantics=("arbitrary", "arbitrary", "arbitrary"),
            vmem_limit_bytes=VMEM_LIMIT),
        name=f"attn_prompt_g{g}",
    )(q, k, k, v, v)


def _merge_prompt_kernel(x_ref, o0_ref, o1_ref, o2_ref, l0_ref, l1_ref, l2_ref, sel_ref, wo_ref,
                         y_ref, obuf, lbuf):
    def token_order(ref, buf):
        dil, per, width = ref.shape
        if dil == 1:
            return ref[0]
        chunks = width // LANES
        for r in range(dil):
            rows = ref[r]
            for c in range(chunks):
                buf[c, pl.ds(r, per, stride=dil), :] = rows[:, c * LANES:(c + 1) * LANES]
        return jnp.concatenate([buf[c] for c in range(chunks)], axis=1)

    lses = []
    for l_ref in (l0_ref, l1_ref, l2_ref):
        lses.append(token_order(l_ref, lbuf))
    m = jnp.maximum(jnp.maximum(lses[0], lses[1]), lses[2])
    es = [jnp.exp(l - m) for l in lses]
    den = es[0] + es[1] + es[2]
    comb = None
    for e, o_ref in zip(es, (o0_ref, o1_ref, o2_ref)):
        w = e / den
        hi = w.astype(BF16)
        lo = (w - hi.astype(F32)).astype(BF16)
        wide = (jnp.dot(hi, sel_ref[...], preferred_element_type=F32)
                + jnp.dot(lo, sel_ref[...], preferred_element_type=F32))
        term = wide * token_order(o_ref, obuf)
        comb = term if comb is None else comb + term
    y_ref[...] = x_ref[...] + _bdot(comb, wo_ref[...])


def _merge_prompt(x, outs, lses, wo):
    b, t, d = x.shape
    tm = TM_TOKEN
    head_of_lane = jnp.arange(ATT_BLK) // LSE_REP
    first_copy = jnp.arange(ATT_BLK) % LSE_REP == 0
    sel = ((head_of_lane[:, None] == (jnp.arange(ATT_W) // HEAD_DIM)[None, :])
           & first_copy[:, None]).astype(BF16)
    part_specs = []
    for width, parts in ((ATT_W, outs), (ATT_BLK, lses)):
        for p in parts:
            dil = p.shape[1]
            part_specs.append(pl.BlockSpec((None, dil, tm // dil, width),
                                           lambda bi, ti: (bi, 0, ti, 0)))
    row = lambda bi, ti: (bi, ti, 0)
    return pl.pallas_call(
        _merge_prompt_kernel,
        grid=(b, t // tm),
        in_specs=[pl.BlockSpec((None, tm, d), row)] + part_specs + _weight_specs((sel, wo)),
        out_specs=pl.BlockSpec((None, tm, d), row),
        out_shape=jax.ShapeDtypeStruct((b, t, d), F32),
        scratch_shapes=[pltpu.VMEM((ATT_W // LANES, tm, LANES), F32),
                        pltpu.VMEM((ATT_BLK // LANES, tm, LANES), F32)],
        compiler_params=pltpu.CompilerParams(
            dimension_semantics=("arbitrary", "arbitrary"), vmem_limit_bytes=VMEM_LIMIT),
        name="attn_merge_prompt",
    )(x, *outs, *lses, sel, wo)


def _exact_transpose(x):
    c = x.shape[1]
    eye = (lax.broadcasted_iota(jnp.int32, (c, c), 0)
           == lax.broadcasted_iota(jnp.int32, (c, c), 1)).astype(F32)
    return lax.dot_general(eye, x, (((1,), (1,)), ((), ())),
                           precision=lax.Precision.HIGHEST, preferred_element_type=F32)


def _attn_sample_kernel(q_ref, kn_ref, vn_ref, k_ref, v_ref,
                        o_ref, lse_ref, ko_ref, vo_ref,
                        qc_ref, kc_ref, vc_ref, oc_ref, lc_ref, *, dil, hb):
    step = pl.program_id(0)
    bb, heads, dh, w = k_ref.shape
    npad = q_ref.shape[0]

    @pl.when(step == 0)
    def _():
        qc_ref[...] = _exact_transpose(q_ref[...]) * (HEAD_DIM ** -0.5)
        kc_ref[...] = _exact_transpose(kn_ref[...])
        vc_ref[...] = _exact_transpose(vn_ref[...])
        oc_ref[...] = jnp.zeros_like(oc_ref)
        lc_ref[...] = jnp.zeros_like(lc_ref)

    pos = lax.broadcasted_iota(jnp.int32, (1, w), 1)
    attended = (pos & (dil - 1)) == 0
    newest = pos == w - 1
    seq_lane = lax.broadcasted_iota(jnp.int32, (1, npad), 1)

    def shifted(tile, new_col):
        flat = pltpu.roll(tile.reshape(hb * dh, w), w - 1, axis=1)
        return jnp.where(newest, new_col, flat.reshape(hb, dh, w))

    for i in range(bb):
        mine = seq_lane == step * bb + i

        def column(ref, rows):
            col = jnp.sum(jnp.where(mine, ref[rows, :], 0.0), axis=-1, keepdims=True)
            return col.reshape(hb, dh, 1)

        def head_group(grp, carry):
            if hb == heads:
                rows, hs = slice(None), slice(None)
            else:
                rows = pl.ds(pl.multiple_of(grp * (hb * dh), hb * dh), hb * dh)
                hs = pl.ds(grp * hb, hb)
            qc, kc, vc = column(qc_ref, rows), column(kc_ref, rows), column(vc_ref, rows)
            kt = k_ref[i, hs]
            vt = v_ref[i, hs]
            s = jnp.where(attended, jnp.sum(kt * qc, axis=1, keepdims=True), -jnp.inf)
            s_new = jnp.sum(kc * qc, axis=1, keepdims=True)
            m = jnp.maximum(jnp.max(s, axis=-1, keepdims=True), s_new)
            p = jnp.exp(s - m)
            p_new = jnp.exp(s_new - m)
            den = jnp.sum(p, axis=-1, keepdims=True) + p_new
            o = (jnp.sum(vt * p, axis=-1, keepdims=True) + p_new * vc) / den
            lse = jnp.broadcast_to(m + jnp.log(den), (hb, dh, 1))
            oc_ref[rows, :] = jnp.where(mine, o.reshape(hb * dh, 1), oc_ref[rows, :])
            lc_ref[rows, :] = jnp.where(mine, lse.reshape(hb * dh, 1), lc_ref[rows, :])
            ko_ref[i, hs] = shifted(kt, kc)
            vo_ref[i, hs] = shifted(vt, vc)
            return carry

        if hb == heads:
            head_group(0, 0)
        else:
            lax.fori_loop(0, heads // hb, head_group, 0)

    @pl.when(step == pl.num_programs(0) - 1)
    def _():
        o_ref[...] = _exact_transpose(oc_ref[...])
        lse_ref[...] = _exact_transpose(lc_ref[...])


SAMPLE_PAD = 128
SAMPLE_BB = {128: 8, 512: 4, 2048: 1}
SAMPLE_HB = {128: 8, 512: 8, 2048: 1}


def _attn_sample_group(q, k_new, v_new, k_cache, v_cache, dil):
    n, w = k_cache.shape[:2]
    assert w % dil == 0 and w // dil == ATT_BLK
    bb = SAMPLE_BB[w]
    pad = lambda x: jnp.pad(x, ((0, SAMPLE_PAD - n), (0, 0)))
    to_lanes = lambda c: jnp.transpose(c, (0, 2, 3, 1))
    cache_spec = pl.BlockSpec((bb, HEADS, HEAD_DIM, w), lambda i: (i, 0, 0, 0))
    vec_spec = _full_spec((SAMPLE_PAD, ATT_W))
    cache_shape = jax.ShapeDtypeStruct((n, HEADS, HEAD_DIM, w), F32)
    vec_shape = jax.ShapeDtypeStruct((SAMPLE_PAD, ATT_W), F32)
    o, lse, k_out, v_out = pl.pallas_call(
        functools.partial(_attn_sample_kernel, dil=dil, hb=SAMPLE_HB[w]),
        grid=(n // bb,),
        in_specs=[vec_spec] * 3 + [cache_spec] * 2,
        out_specs=[vec_spec] * 2 + [cache_spec] * 2,
        out_shape=[vec_shape] * 2 + [cache_shape] * 2,
        scratch_shapes=[pltpu.VMEM((ATT_W, SAMPLE_PAD), F32)] * 5,
        compiler_params=pltpu.CompilerParams(
            dimension_semantics=("arbitrary",), vmem_limit_bytes=VMEM_LIMIT),
        name=f"attn_sample_d{dil}",
    )(pad(q), pad(k_new), pad(v_new), to_lanes(k_cache), to_lanes(v_cache))
    to_rows = lambda c: jnp.transpose(c, (0, 3, 1, 2))
    return o[:n], lse[:n], to_rows(k_out), to_rows(v_out)


def _merge_kernel(x_ref, o0_ref, o1_ref, o2_ref, l0_ref, l1_ref, l2_ref, wo_ref, y_ref):
    l0, l1, l2 = l0_ref[...], l1_ref[...], l2_ref[...]
    m = jnp.maximum(jnp.maximum(l0, l1), l2)
    e0, e1, e2 = jnp.exp(l0 - m), jnp.exp(l1 - m), jnp.exp(l2 - m)
    den = e0 + e1 + e2
    o = (e0 / den) * o0_ref[...] + (e1 / den) * o1_ref[...] + (e2 / den) * o2_ref[...]
    y_ref[...] = x_ref[...] + _bdot(o, wo_ref[...])


def _merge(x, outs, lses, wo, tm):
    m, d = x.shape
    w = wo.shape[0]
    row = lambda i: (i, 0)
    part = pl.BlockSpec((tm, w), row)
    return pl.pallas_call(
        _merge_kernel,
        grid=(m // tm,),
        in_specs=[pl.BlockSpec((tm, d), row)] + [part] * 6 + [_weight_spec(wo)],
        out_specs=pl.BlockSpec((tm, d), row),
        out_shape=jax.ShapeDtypeStruct((m, d), F32),
        compiler_params=pltpu.CompilerParams(
            dimension_semantics=("arbitrary",), vmem_limit_bytes=VMEM_LIMIT),
        name="attn_merge",
    )(x, *outs, *lses, wo)


def kernel(x_prompt, x_sample, state_lru_h, state_conv_a, state_ffn_conv, cache_k0, cache_v0, cache_k1, cache_v1, cache_k2, cache_v2, norm_mix, a_w_in, a_conv_w, a_conv_b, a_gate_a_w, a_gate_a_b, a_gate_x_w, a_gate_x_b, a_lambda, a_w_out, kv_norm, w_kv, b_w_q, b_w_o, norm_ffn, ffn_w_up, ffn_conv_w, ffn_conv_b, ffn_w_down, final_norm):
    b, t, d = x_prompt.shape
    n = x_sample.shape[0]
    assert x_sample.shape[1] == 1 and norm_mix.shape[0] == 2
    caches = (cache_k0, cache_v0, cache_k1, cache_v1, cache_k2, cache_v2)
    row = lambda v: v.reshape(1, -1)

    mixer_w = (row(norm_mix[0]), a_w_in[0].astype(BF16), a_conv_w[0], row(a_conv_b[0]),
               a_gate_a_w[0].astype(BF16), row(a_gate_a_b[0]),
               a_gate_x_w[0].astype(BF16), row(a_gate_x_b[0]),
               row(a_lambda[0]), a_w_out[0].astype(BF16))
    ffn_w = [(row(norm_ffn[l]), ffn_w_up[l].astype(BF16), ffn_conv_w[l],
              row(ffn_conv_b[l]), ffn_w_down[l].astype(BF16)) for l in range(2)]
    ffn_w[1] = ffn_w[1] + (row(final_norm),)
    gq, gkv = row(norm_mix[1]), row(kv_norm)
    wq, wkv, wo = b_w_q[0].astype(BF16), w_kv.astype(BF16), b_w_o[0].astype(BF16)

    h, p_h, p_ca = _mixer_prompt(x_prompt, mixer_w)
    h, p_f0 = _ffn_prompt(h, ffn_w[0], final_norm=False)
    col = lambda w_, j: w_[:, j * ATT_W:(j + 1) * ATT_W]
    wq_groups = jnp.stack([col(wq, g) for g in range(N_GROUPS)])
    wkv_groups = jnp.stack([jnp.concatenate([col(wkv, g), col(wkv, N_GROUPS + g)], axis=1)
                            for g in range(N_GROUPS)])
    qkv = _qkv_prompt(h, gq, gkv, wq_groups, wkv_groups)
    outs, lses, p_kv = [], [], []
    for g, (win, dil) in enumerate(GROUPS):
        o, lse = _attn_prompt_group(*qkv[g], g)
        outs.append(o)
        lses.append(lse)
        for tail in _kv_tail(h, gkv, col(wkv, g).T, col(wkv, N_GROUPS + g).T, min(win, t)):
            p_kv.append(jnp.transpose(tail.reshape(b, HEADS, HEAD_DIM, -1), (0, 3, 1, 2)))
    h = _merge_prompt(h, outs, lses, wo)
    y_prompt, p_f1 = _ffn_prompt(h, ffn_w[1], final_norm=True)

    xs = x_sample.reshape(n, d)
    conv_buf = jnp.swapaxes(state_conv_a[0], 0, 1)
    hs, s_h, s_ur = _mixer_sample(xs, state_lru_h[0], conv_buf, mixer_w)
    s_ca = jnp.concatenate([state_conv_a[0][:, 1:], s_ur[:, None]], axis=1)
    hs, s_ug0 = _ffn_sample(hs, jnp.swapaxes(state_ffn_conv[0], 0, 1), ffn_w[0], final_norm=False)
    qs, kvs = _qkv(hs, gq, gkv, wq, wkv, n)
    outs_s, lses_s, s_kv = [], [], []
    for g, (_, dil) in enumerate(GROUPS):
        col = lambda a, j: a[:, j * ATT_W:(j + 1) * ATT_W]
        o, lse, k_out, v_out = _attn_sample_group(
            col(qs, g), col(kvs, g), col(kvs, N_GROUPS + g), caches[2 * g], caches[2 * g + 1], dil)
        outs_s.append(o)
        lses_s.append(lse)
        s_kv += [k_out, v_out]
    hs = _merge(hs, outs_s, lses_s, wo, n)
    y_sample, s_ug1 = _ffn_sample(hs, jnp.swapaxes(state_ffn_conv[1], 0, 1), ffn_w[1], final_norm=True)
    s_ffn = jnp.stack([
        jnp.concatenate([state_ffn_conv[l][:, 1:], ug[:, None]], axis=1)
        for l, ug in enumerate((s_ug0, s_ug1))])
    return (y_prompt, y_sample.reshape(n, 1, d),
            p_h.reshape(1, b, d), s_h.reshape(1, n, d),
            p_ca.reshape(1, b, -1, d), s_ca.reshape(1, n, -1, d),
            jnp.stack([p_f0, p_f1]), s_ffn,
            p_kv[0], p_kv[1], s_kv[0], s_kv[1],
            p_kv[2], p_kv[3], s_kv[2], s_kv[3],
            p_kv[4], p_kv[5], s_kv[4], s_kv[5])
```

```python
import functools

import jax
import jax.numpy as jnp
from jax import lax
from jax.experimental import pallas as pl
from jax.experimental.pallas import tpu as pltpu

F32 = jnp.float32
BF16 = jnp.bfloat16

EPS = 1e-6
LRU_C = 8.0
N_LRU_BLOCKS = 4
HEAD_DIM = 64
HEADS = 8
GROUPS = ((128, 1), (512, 4), (2048, 16))
N_GROUPS = len(GROUPS)
ATT_W = HEADS * HEAD_DIM
ATT_BLK = 128
LANES = 128
CARRY_ROWS = 8

TQ_MIXER = 256
TM_FFN = 512
TM_FFN_WITH_CACHE = 256
FC_FFN = 512
TM_TOKEN = 512
VMEM_LIMIT = 52 * 1024 * 1024


def _rmsnorm(x, g):
    return x * lax.rsqrt(jnp.mean(x * x, axis=-1, keepdims=True) + EPS) * g


def _bdot(a, w):
    return jnp.dot(a.astype(BF16), w, preferred_element_type=F32)


def _shift_rows(cur, prev, s):
    rolled = pltpu.roll(cur, s, axis=0)
    prev_rolled = pltpu.roll(prev, s, axis=0)
    row = lax.broadcasted_iota(jnp.int32, prev.shape, 0)
    head = jnp.where(row < s, prev_rolled, rolled[:CARRY_ROWS])
    return jnp.concatenate([head, rolled[CARRY_ROWS:]], axis=0)


def _causal_conv_rows(cur, prev, w_ref, b):
    k = w_ref.shape[0]
    y = b + w_ref[k - 1:k, :] * cur
    for j in range(k - 1):
        y = y + w_ref[j:j + 1, :] * _shift_rows(cur, prev, k - 1 - j)
    return y


def _log_sigmoid(x):
    return jnp.minimum(x, 0.0) - jnp.log1p(jnp.exp(-jnp.abs(x)))


def _lru_coeffs(xc, wa_ref, ba, wx_ref, bx, lam):
    xb = xc.astype(BF16)
    blk = xc.shape[-1] // N_LRU_BLOCKS
    ra, ri = [], []
    for n in range(N_LRU_BLOCKS):
        xs = xb[:, n * blk:(n + 1) * blk]
        ra.append(jnp.dot(xs, wa_ref[n], preferred_element_type=F32))
        ri.append(jnp.dot(xs, wx_ref[n], preferred_element_type=F32))
    r = jax.nn.sigmoid(jnp.concatenate(ra, axis=-1) + ba)
    i = jax.nn.sigmoid(jnp.concatenate(ri, axis=-1) + bx)
    log_a = LRU_C * r * _log_sigmoid(lam)
    a = jnp.exp(log_a)
    one_minus_a2 = (1.0 + a * a) * jnp.tanh(-log_a)
    bt = jnp.sqrt(one_minus_a2) * (i * xc)
    return a, bt


def _scan_rows(a, b, h_init):
    n, d = a.shape
    groups = n // CARRY_ROWS
    a = a.reshape(groups, CARRY_ROWS, d)
    b = b.reshape(groups, CARRY_ROWS, d)
    row = lax.broadcasted_iota(jnp.int32, a.shape, 1)
    s = 1
    while s < CARRY_ROWS:
        keep = row >= s
        a_sh = jnp.where(keep, pltpu.roll(a, s, axis=1), 1.0)
        b_sh = jnp.where(keep, pltpu.roll(b, s, axis=1), 0.0)
        b = a * b_sh + b
        a = a * a_sh
        s *= 2
    h = h_init
    out = []
    for k in range(groups):
        hk = a[k] * h + b[k]
        out.append(hk)
        h = hk[CARRY_ROWS - 1:, :]
    return jnp.concatenate(out, axis=0)


def _mixer_prompt_kernel(x_ref, g_ref, win_ref, cw_ref, cb_ref, wa_ref, ba_ref,
                         wx_ref, bx_ref, lam_ref, wout_ref,
                         y_ref, hlast_ref, cbuf_ref, ucarry, hcarry):
    t = pl.program_id(1)
    tq, d = x_ref.shape

    @pl.when(t == 0)
    def _():
        ucarry[...] = jnp.zeros_like(ucarry)
        hcarry[...] = jnp.zeros_like(hcarry)

    x = x_ref[...]
    u = _bdot(_rmsnorm(x, g_ref[...]), win_ref[...])
    gate = jax.nn.gelu(u[:, :d])
    ur = u[:, d:]
    xc = _causal_conv_rows(ur, ucarry[...], cw_ref, cb_ref[...])
    ucarry[...] = ur[tq - CARRY_ROWS:, :]
    a, bt = _lru_coeffs(xc, wa_ref, ba_ref[...], wx_ref, bx_ref[...], lam_ref[...])
    h = _scan_rows(a, bt, hcarry[...])
    hcarry[...] = h[tq - 1:tq, :]
    y_ref[...] = x + _bdot(h * gate, wout_ref[...])

    @pl.when(t == pl.num_programs(1) - 1)
    def _():
        hlast_ref[...] = h[tq - 1:tq, :]
        k = cw_ref.shape[0]
        cbuf_ref[...] = ur[tq - (k - 1):, :]


def _mixer_sample_kernel(x_ref, h0_ref, buf_ref, g_ref, win_ref, cw_ref, cb_ref,
                         wa_ref, ba_ref, wx_ref, bx_ref, lam_ref, wout_ref,
                         y_ref, hnew_ref, ur_ref):
    d = x_ref.shape[-1]
    k = cw_ref.shape[0]
    x = x_ref[...]
    u = _bdot(_rmsnorm(x, g_ref[...]), win_ref[...])
    gate = jax.nn.gelu(u[:, :d])
    ur = u[:, d:]
    xc = cb_ref[...] + cw_ref[k - 1:k, :] * ur
    for j in range(k - 1):
        xc = xc + cw_ref[j:j + 1, :] * buf_ref[j]
    a, bt = _lru_coeffs(xc, wa_ref, ba_ref[...], wx_ref, bx_ref[...], lam_ref[...])
    h = a * h0_ref[...] + bt
    hnew_ref[...] = h
    ur_ref[...] = ur
    y_ref[...] = x + _bdot(h * gate, wout_ref[...])


def _full_spec(shape):
    zeros = (0,) * len(shape)
    return pl.BlockSpec(shape, lambda *_: zeros)


class _Layer:
    def __init__(self, stacked, index):
        self.stacked, self.index = stacked, index

    @property
    def shape(self):
        return self.stacked.shape[1:]


def _weight_spec(w):
    zeros = (0,) * len(w.shape)
    if isinstance(w, _Layer):
        return pl.BlockSpec((None,) + w.shape, lambda *_: (w.index,) + zeros,
                            pipeline_mode=pl.Buffered(1))
    return pl.BlockSpec(w.shape, lambda *_: zeros, pipeline_mode=pl.Buffered(1))


def _weight_specs(weights):
    return [_weight_spec(w) for w in weights]


def _weight_arrays(weights):
    return [w.stacked if isinstance(w, _Layer) else w for w in weights]


def _mixer_prompt(x, weights):
    b, t, d = x.shape
    tq = TQ_MIXER
    k = weights[2].shape[0]
    row = lambda bi, ti: (bi, ti, 0)
    per_seq = lambda bi, ti: (bi, 0, 0)
    return pl.pallas_call(
        _mixer_prompt_kernel,
        grid=(b, t // tq),
        in_specs=[pl.BlockSpec((None, tq, d), row)] + _weight_specs(weights),
        out_specs=[pl.BlockSpec((None, tq, d), row),
                   pl.BlockSpec((None, 1, d), per_seq),
                   pl.BlockSpec((None, k - 1, d), per_seq)],
        out_shape=[jax.ShapeDtypeStruct((b, t, d), F32),
                   jax.ShapeDtypeStruct((b, 1, d), F32),
                   jax.ShapeDtypeStruct((b, k - 1, d), F32)],
        scratch_shapes=[pltpu.VMEM((CARRY_ROWS, d), F32), pltpu.VMEM((1, d), F32)],
        compiler_params=pltpu.CompilerParams(
            dimension_semantics=("arbitrary", "arbitrary"),
            vmem_limit_bytes=VMEM_LIMIT),
        name="mixer_prompt",
    )(x, *weights)


def _mixer_sample(x, h0, buf, weights):
    n, d = x.shape
    k = weights[2].shape[0]
    return pl.pallas_call(
        _mixer_sample_kernel,
        grid=(1,),
        in_specs=[_full_spec((n, d)), _full_spec((n, d)), _full_spec((k - 1, n, d))]
        + _weight_specs(weights),
        out_specs=[_full_spec((n, d))] * 3,
        out_shape=[jax.ShapeDtypeStruct((n, d), F32)] * 3,
        compiler_params=pltpu.CompilerParams(
            dimension_semantics=("arbitrary",), vmem_limit_bytes=VMEM_LIMIT),
        name="mixer_sample",
    )(x, h0, buf, *weights)


def _ffn_chunks(xb, x, wup_ref, cw_ref, cb_ref, wdown_ref, conv_gate, act_ref=None, fc=FC_FFN):
    f = wdown_ref.shape[0]
    acc = x
    for lo in range(0, f, fc):
        hi = lo + fc
        ug = jnp.dot(xb, wup_ref[:, lo:hi], preferred_element_type=F32)
        uv = jnp.dot(xb, wup_ref[:, f + lo:f + hi], preferred_element_type=F32)
        gc = conv_gate(ug, lo, hi)
        act = (jax.nn.gelu(gc) * uv).astype(BF16)
        if act_ref is None:
            acc = acc + jnp.dot(act, wdown_ref[lo:hi, :], preferred_element_type=F32)
        else:
            act_ref[:, lo:hi] = act
    if act_ref is not None:
        acc = acc + jnp.dot(act_ref[...], wdown_ref[...], preferred_element_type=F32)
    return acc


def _ffn_prompt_kernel(*refs, final_norm, fc, cache_dil):
    refs = list(refs)
    x_ref, g_ref, wup_ref, cw_ref, cb_ref, wdown_ref = refs[:6]
    del refs[:6]
    fn_ref = refs.pop(0) if final_norm else None
    if cache_dil:
        cols_ref, kc_ref, vc_ref = refs[:3]
        del refs[:3]
    y_ref, fbuf_ref = refs[:2]
    del refs[:2]
    if cache_dil:
        oc_ref, lc_ref, ko_ref, vo_ref = refs[:4]
        del refs[:4]
    gcarry, act_ref = refs
    t = pl.program_id(1)
    tm = x_ref.shape[0]
    k = cw_ref.shape[0]

    @pl.when(t == 0)
    def _():
        gcarry[...] = jnp.zeros_like(gcarry)

    if cache_dil:
        step = pl.program_id(0) * pl.num_programs(1) + t
        nh, dh = kc_ref.shape[:2]
        per_seq = HEADS // nh

        @pl.when(step == 0)
        def _():
            oc_ref[...] = jnp.zeros_like(oc_ref)
            lc_ref[...] = jnp.zeros_like(lc_ref)

        _attend_and_roll(cols_ref, kc_ref, vc_ref, ko_ref, vo_ref, oc_ref, lc_ref,
                         step // per_seq, (step % per_seq) * (nh * dh), cache_dil, 1)

    def conv_gate(ug, lo, hi):
        gc = _causal_conv_rows(ug, gcarry[:, lo:hi], cw_ref.at[:, lo:hi], cb_ref[:, lo:hi])
        gcarry[:, lo:hi] = ug[tm - CARRY_ROWS:, :]
        return gc

    x = x_ref[...]
    xb = _rmsnorm(x, g_ref[...]).astype(BF16)
    y = _ffn_chunks(xb, x, wup_ref, cw_ref, cb_ref, wdown_ref, conv_gate, act_ref, fc)
    if final_norm:
        y = _rmsnorm(y, fn_ref[...])
    y_ref[...] = y

    @pl.when(t == pl.num_programs(1) - 1)
    def _():
        fbuf_ref[...] = gcarry[CARRY_ROWS - (k - 1):, :]


def _ffn_sample_kernel(*refs, final_norm):
    if final_norm:
        (x_ref, buf_ref, g_ref, wup_ref, cw_ref, cb_ref, wdown_ref, fn_ref,
         y_ref, ug_ref) = refs
    else:
        (x_ref, buf_ref, g_ref, wup_ref, cw_ref, cb_ref, wdown_ref,
         y_ref, ug_ref) = refs
    k = cw_ref.shape[0]

    def conv_gate(ug, lo, hi):
        ug_ref[:, lo:hi] = ug
        gc = cb_ref[:, lo:hi] + cw_ref[k - 1:k, lo:hi] * ug
        for j in range(k - 1):
            gc = gc + cw_ref[j:j + 1, lo:hi] * buf_ref[j, :, lo:hi]
        return gc

    x = x_ref[...]
    xb = _rmsnorm(x, g_ref[...]).astype(BF16)
    y = _ffn_chunks(xb, x, wup_ref, cw_ref, cb_ref, wdown_ref, conv_gate)
    if final_norm:
        y = _rmsnorm(y, fn_ref[...])
    y_ref[...] = y


def _ffn_prompt(x, weights, final_norm, tm, cache=None):
    b, t, d = x.shape
    f = weights[4].shape[0]
    k = weights[2].shape[0]
    nt = t // tm
    row = lambda bi, ti: (bi, ti, 0)
    per_seq = lambda bi, ti: (bi, 0, 0)
    in_specs = [pl.BlockSpec((None, tm, d), row)] + _weight_specs(weights)
    out_specs = [pl.BlockSpec((None, tm, d), row), pl.BlockSpec((None, k - 1, f), per_seq)]
    out_shape = [jax.ShapeDtypeStruct((b, t, d), F32), jax.ShapeDtypeStruct((b, k - 1, f), F32)]
    operands = [x, *_weight_arrays(weights)]
    cache_dil = None
    if cache is not None:
        cols, kc, vc, cache_dil = cache
        n, heads, dh, w = kc.shape
        steps_per_seq = (b * nt) // n
        nh = heads // steps_per_seq
        assert steps_per_seq * n == b * nt and nh * steps_per_seq == heads
        blk = pl.BlockSpec((None, nh, dh, w),
                           lambda bi, ti: ((bi * nt + ti) // steps_per_seq,
                                           (bi * nt + ti) % steps_per_seq, 0, 0))
        col = _full_spec(cols.shape[1:])
        in_specs += [_full_spec(cols.shape), blk, blk]
        out_specs += [col, col, blk, blk]
        out_shape += [jax.ShapeDtypeStruct(cols.shape[1:], F32)] * 2
        out_shape += [jax.ShapeDtypeStruct(kc.shape, F32)] * 2
        operands += [cols, kc, vc]
    return pl.pallas_call(
        functools.partial(_ffn_prompt_kernel, final_norm=final_norm, fc=FC_FFN,
                          cache_dil=cache_dil),
        grid=(b, nt),
        in_specs=in_specs,
        out_specs=out_specs,
        out_shape=out_shape,
        scratch_shapes=[pltpu.VMEM((CARRY_ROWS, f), F32), pltpu.VMEM((tm, f), BF16)],
        compiler_params=pltpu.CompilerParams(
            dimension_semantics=("arbitrary", "arbitrary"),
            vmem_limit_bytes=VMEM_LIMIT),
        name="ffn_prompt_final" if final_norm else "ffn_prompt",
    )(*operands)


def _ffn_sample(x, buf, weights, final_norm):
    n, d = x.shape
    f = weights[4].shape[0]
    k = weights[2].shape[0]
    return pl.pallas_call(
        functools.partial(_ffn_sample_kernel, final_norm=final_norm),
        grid=(1,),
        in_specs=[_full_spec((n, d)), _full_spec((k - 1, n, f))]
        + _weight_specs(weights),
        out_specs=[_full_spec((n, d)), _full_spec((n, f))],
        out_shape=[jax.ShapeDtypeStruct((n, d), F32), jax.ShapeDtypeStruct((n, f), F32)],
        compiler_params=pltpu.CompilerParams(
            dimension_semantics=("arbitrary",), vmem_limit_bytes=VMEM_LIMIT),
        name="ffn_sample_final" if final_norm else "ffn_sample",
    )(x, buf, *_weight_arrays(weights))


def _qkv_kernel(x_ref, gq_ref, gkv_ref, wq_ref, wkv_ref, q_ref, kv_ref):
    x = x_ref[...]
    inv = lax.rsqrt(jnp.mean(x * x, axis=-1, keepdims=True) + EPS)
    y = x * inv
    q_ref[...] = _bdot(y * gq_ref[...], wq_ref[...])
    kv_ref[...] = _bdot(y * gkv_ref[...], wkv_ref[...])


def _qkv(x, gq, gkv, wq, wkv, tm):
    m, d = x.shape
    nq, nkv = wq.shape[1], wkv.shape[1]
    row = lambda i: (i, 0)
    return pl.pallas_call(
        _qkv_kernel,
        grid=(m // tm,),
        in_specs=[pl.BlockSpec((tm, d), row)] + _weight_specs((gq, gkv, wq, wkv)),
        out_specs=[pl.BlockSpec((tm, nq), row), pl.BlockSpec((tm, nkv), row)],
        out_shape=[jax.ShapeDtypeStruct((m, nq), F32), jax.ShapeDtypeStruct((m, nkv), F32)],
        compiler_params=pltpu.CompilerParams(
            dimension_semantics=("arbitrary",), vmem_limit_bytes=VMEM_LIMIT),
        name="qkv_proj",
    )(x, gq, gkv, wq, wkv)


LSE_REP = ATT_BLK // HEADS


_NT = (((1,), (1,)), ((), ()))


def _qkv_prompt_kernel(x_ref, gq_ref, gkv_ref, wq_ref, wkv_ref, *out_and_scratch):
    out_refs, ybuf = out_and_scratch[:-1], out_and_scratch[-1]
    x = x_ref[...]
    y = x * lax.rsqrt(jnp.mean(x * x, axis=-1, keepdims=True) + EPS)
    chunks = ybuf.shape[0]
    for c in range(chunks):
        ybuf[c] = y[:, c * LANES:(c + 1) * LANES]
    for g in range(N_GROUPS):
        q_ref, k_ref, v_ref = out_refs[3 * g:3 * g + 3]
        dil, per = q_ref.shape[0], q_ref.shape[1]
        yg = y
        if dil > 1:
            yg = jnp.concatenate(
                [jnp.concatenate([ybuf[c, pl.ds(r, per, stride=dil), :] for c in range(chunks)], axis=1)
                 for r in range(dil)], axis=0)
        xq = (yg * gq_ref[...]).astype(BF16)
        xkv = (yg * gkv_ref[...]).astype(BF16)
        part = lambda ref, j: ref[:, j * ATT_W:(j + 1) * ATT_W]
        q = jnp.dot(xq, part(wq_ref, g), preferred_element_type=F32) * (HEAD_DIM ** -0.5)
        k = jnp.dot(xkv, part(wkv_ref, g), preferred_element_type=F32)
        v = jnp.dot(xkv, part(wkv_ref, N_GROUPS + g), preferred_element_type=F32)
        q_ref[...] = q.astype(BF16).reshape(dil, per, ATT_W)
        k_ref[...] = k.astype(BF16).reshape(dil, per, ATT_W)
        v_ref[...] = v.astype(BF16).reshape(dil, per, ATT_W)


def _qkv_prompt(x, gq, gkv, wq, wkv):
    b, t, d = x.shape
    tm = TM_TOKEN
    specs, shapes = [], []
    for _, dil in GROUPS:
        assert tm % (dil * 16) == 0
        specs += [pl.BlockSpec((None, dil, tm // dil, ATT_W), lambda bi, ti: (bi, 0, ti, 0))] * 3
        shapes += [jax.ShapeDtypeStruct((b, dil, t // dil, ATT_W), BF16)] * 3
    weights = (gq, gkv, wq, wkv)
    outs = pl.pallas_call(
        _qkv_prompt_kernel,
        grid=(b, t // tm),
        in_specs=[pl.BlockSpec((None, tm, d), lambda bi, ti: (bi, ti, 0))] + _weight_specs(weights),
        out_specs=specs,
        out_shape=shapes,
        scratch_shapes=[pltpu.VMEM((d // LANES, tm, LANES), F32)],
        compiler_params=pltpu.CompilerParams(
            dimension_semantics=("arbitrary", "arbitrary"), vmem_limit_bytes=VMEM_LIMIT),
        name="qkv_prompt",
    )(x, *weights)
    return [outs[3 * g:3 * g + 3] for g in range(N_GROUPS)]


def _kv_tail_kernel(x_ref, gkv_ref, wkv_ref, kt_ref, vt_ref, *, group):
    x = x_ref[...]
    y = x * lax.rsqrt(jnp.mean(x * x, axis=-1, keepdims=True) + EPS)
    xkv = (y * gkv_ref[...]).astype(BF16)
    part = lambda j: wkv_ref[:, j * ATT_W:(j + 1) * ATT_W]
    kt_ref[...] = jnp.dot(xkv, part(group), preferred_element_type=F32).T
    vt_ref[...] = jnp.dot(xkv, part(N_GROUPS + group), preferred_element_type=F32).T


def _kv_tail(x, gkv, wkv, group, win):
    b, t, d = x.shape
    assert t % win == 0
    out = pl.BlockSpec((None, ATT_W, win), lambda bi: (bi, 0, 0))
    weights = (gkv, wkv)
    return pl.pallas_call(
        functools.partial(_kv_tail_kernel, group=group),
        grid=(b,),
        in_specs=[pl.BlockSpec((None, win, d), lambda bi: (bi, t // win - 1, 0))]
        + _weight_specs(weights),
        out_specs=[out, out],
        out_shape=[jax.ShapeDtypeStruct((b, ATT_W, win), F32)] * 2,
        compiler_params=pltpu.CompilerParams(
            dimension_semantics=("arbitrary",), vmem_limit_bytes=VMEM_LIMIT),
        name=f"kv_tail_w{win}",
    )(x, *weights)


def _attn_prompt_kernel(q_ref, kp_ref, k_ref, vp_ref, v_ref, o_ref, lse_ref):
    blk = kp_ref.shape[0]
    nblk = q_ref.shape[0] // blk
    qi = lax.broadcasted_iota(jnp.int32, (blk, 2 * blk), 0)
    kj = lax.broadcasted_iota(jnp.int32, (blk, 2 * blk), 1)
    band = jnp.logical_or(jnp.logical_and(kj < blk, kj >= qi),
                          jnp.logical_and(kj >= blk, kj - blk <= qi))
    band_first = jnp.logical_and(band, jnp.logical_or(kj >= blk, pl.program_id(2) > 0))
    lane = lax.broadcasted_iota(jnp.int32, (blk, blk), 1)
    low_half = lane < HEAD_DIM
    head_of_lane = lane // LSE_REP
    zero = jnp.zeros((), BF16)
    for j in range(nblk):
        rows = slice(j * blk, (j + 1) * blk)
        q = q_ref[rows, :]
        if j == 0:
            keep = band_first
            k = jnp.concatenate([kp_ref[...], k_ref[:blk, :]], axis=0)
            v = jnp.concatenate([vp_ref[...], v_ref[:blk, :]], axis=0)
        else:
            keep = band
            k = k_ref[(j - 1) * blk:(j + 1) * blk, :]
            v = v_ref[(j - 1) * blk:(j + 1) * blk, :]
        lse_tile = jnp.zeros((blk, blk), F32)
        for pair in range(HEADS // 2):
            cols = slice(pair * blk, (pair + 1) * blk)
            q2, k2, v2 = q[:, cols], k[:, cols], v[:, cols]
            outs = []
            for half in range(2):
                mine = low_half if half == 0 else jnp.logical_not(low_half)
                s = lax.dot_general(jnp.where(mine, q2, zero), k2, _NT, preferred_element_type=F32)
                s = jnp.where(keep, s, -jnp.inf)
                m = jnp.max(s, axis=-1, keepdims=True)
                p = jnp.exp(s - m)
                den = jnp.sum(p, axis=-1, keepdims=True)
                outs.append(jnp.dot(p.astype(BF16), v2, preferred_element_type=F32) / den)
                lse_tile = jnp.where(head_of_lane == 2 * pair + half, m + jnp.log(den), lse_tile)
            o_ref[rows, cols] = jnp.where(low_half, outs[0], outs[1])
        lse_ref[rows, :] = lse_tile


ATT_NBLK = 8


def _attn_prompt_group(q, k, v, g):
    b, dil, n, _ = q.shape
    nblk = min(ATT_NBLK, n // ATT_BLK)
    tq = nblk * ATT_BLK
    assert n % tq == 0
    prev = lambda i: jnp.maximum(i * nblk - 1, 0)
    rows = lambda size, width: pl.BlockSpec((None, None, size, width), lambda bi, r, i: (bi, r, i, 0))
    before = pl.BlockSpec((None, None, ATT_BLK, ATT_W), lambda bi, r, i: (bi, r, prev(i), 0))
    return pl.pallas_call(
        _attn_prompt_kernel,
        grid=(b, dil, n // tq),
        in_specs=[rows(tq, ATT_W), before, rows(tq, ATT_W), before, rows(tq, ATT_W)],
        out_specs=[rows(tq, ATT_W), rows(tq, ATT_BLK)],
        out_shape=[jax.ShapeDtypeStruct((b, dil, n, ATT_W), F32),
                   jax.ShapeDtypeStruct((b, dil, n, ATT_BLK), F32)],
        compiler_params=pltpu.CompilerParams(
            dimension_semantics=("arbitrary", "arbitrary", "arbitrary"),
            vmem_limit_bytes=VMEM_LIMIT),
        name=f"attn_prompt_g{g}",
    )(q, k, k, v, v)


def _merge_prompt_kernel(x_ref, o0_ref, o1_ref, o2_ref, l0_ref, l1_ref, l2_ref, sel_ref, wo_ref,
                         y_ref, obuf, lbuf):
    def token_order(ref, buf):
        dil, per, width = ref.shape
        if dil == 1:
            return ref[0]
        chunks = width // LANES
        for r in range(dil):
            rows = ref[r]
            for c in range(chunks):
                buf[c, pl.ds(r, per, stride=dil), :] = rows[:, c * LANES:(c + 1) * LANES]
        return jnp.concatenate([buf[c] for c in range(chunks)], axis=1)

    lses = []
    for l_ref in (l0_ref, l1_ref, l2_ref):
        lses.append(token_order(l_ref, lbuf))
    m = jnp.maximum(jnp.maximum(lses[0], lses[1]), lses[2])
    es = [jnp.exp(l - m) for l in lses]
    den = es[0] + es[1] + es[2]
    comb = None
    for e, o_ref in zip(es, (o0_ref, o1_ref, o2_ref)):
        w = e / den
        hi = w.astype(BF16)
        lo = (w - hi.astype(F32)).astype(BF16)
        wide = (jnp.dot(hi, sel_ref[...], preferred_element_type=F32)
                + jnp.dot(lo, sel_ref[...], preferred_element_type=F32))
        term = wide * token_order(o_ref, obuf)
        comb = term if comb is None else comb + term
    y_ref[...] = x_ref[...] + _bdot(comb, wo_ref[...])


def _merge_prompt(x, outs, lses, wo):
    b, t, d = x.shape
    tm = TM_TOKEN
    head_of_lane = jnp.arange(ATT_BLK) // LSE_REP
    first_copy = jnp.arange(ATT_BLK) % LSE_REP == 0
    sel = ((head_of_lane[:, None] == (jnp.arange(ATT_W) // HEAD_DIM)[None, :])
           & first_copy[:, None]).astype(BF16)
    part_specs = []
    for width, parts in ((ATT_W, outs), (ATT_BLK, lses)):
        for p in parts:
            dil = p.shape[1]
            part_specs.append(pl.BlockSpec((None, dil, tm // dil, width),
                                           lambda bi, ti: (bi, 0, ti, 0)))
    row = lambda bi, ti: (bi, ti, 0)
    return pl.pallas_call(
        _merge_prompt_kernel,
        grid=(b, t // tm),
        in_specs=[pl.BlockSpec((None, tm, d), row)] + part_specs + _weight_specs((sel, wo)),
        out_specs=pl.BlockSpec((None, tm, d), row),
        out_shape=jax.ShapeDtypeStruct((b, t, d), F32),
        scratch_shapes=[pltpu.VMEM((ATT_W // LANES, tm, LANES), F32),
                        pltpu.VMEM((ATT_BLK // LANES, tm, LANES), F32)],
        compiler_params=pltpu.CompilerParams(
            dimension_semantics=("arbitrary", "arbitrary"), vmem_limit_bytes=VMEM_LIMIT),
        name="attn_merge_prompt",
    )(x, *outs, *lses, sel, wo)


def _exact_transpose(x):
    c = x.shape[1]
    eye = (lax.broadcasted_iota(jnp.int32, (c, c), 0)
           == lax.broadcasted_iota(jnp.int32, (c, c), 1)).astype(F32)
    return lax.dot_general(eye, x, (((1,), (1,)), ((), ())),
                           precision=lax.Precision.HIGHEST, preferred_element_type=F32)


SAMPLE_PAD = 128
SAMPLE_BB = {128: 8, 512: 4}


def _sample_cols_kernel(q_ref, kv_ref, cols_ref):
    for g in range(N_GROUPS):
        part = lambda ref, j: ref[:, j * ATT_W:(j + 1) * ATT_W]
        cols_ref[3 * g] = _exact_transpose(part(q_ref, g)) * (HEAD_DIM ** -0.5)
        cols_ref[3 * g + 1] = _exact_transpose(part(kv_ref, g))
        cols_ref[3 * g + 2] = _exact_transpose(part(kv_ref, N_GROUPS + g))


def _sample_rows_kernel(oc_ref, lc_ref, o_ref, lse_ref):
    for g in range(N_GROUPS):
        o_ref[:, g * ATT_W:(g + 1) * ATT_W] = _exact_transpose(oc_ref[g])
        lse_ref[:, g * ATT_W:(g + 1) * ATT_W] = _exact_transpose(lc_ref[g])


def _sample_cols(q, kv):
    pad = lambda x: jnp.pad(x, ((0, SAMPLE_PAD - x.shape[0]), (0, 0)))
    q, kv = pad(q), pad(kv)
    shape = (3 * N_GROUPS, ATT_W, SAMPLE_PAD)
    return pl.pallas_call(
        _sample_cols_kernel,
        grid=(1,),
        in_specs=[_full_spec(q.shape), _full_spec(kv.shape)],
        out_specs=_full_spec(shape),
        out_shape=jax.ShapeDtypeStruct(shape, F32),
        compiler_params=pltpu.CompilerParams(
            dimension_semantics=("arbitrary",), vmem_limit_bytes=VMEM_LIMIT),
        name="sample_cols",
    )(q, kv)


def _sample_rows(oc, lc):
    shape = (SAMPLE_PAD, N_GROUPS * ATT_W)
    return pl.pallas_call(
        _sample_rows_kernel,
        grid=(1,),
        in_specs=[_full_spec(oc.shape), _full_spec(lc.shape)],
        out_specs=[_full_spec(shape)] * 2,
        out_shape=[jax.ShapeDtypeStruct(shape, F32)] * 2,
        compiler_params=pltpu.CompilerParams(
            dimension_semantics=("arbitrary",), vmem_limit_bytes=VMEM_LIMIT),
        name="sample_rows",
    )(oc, lc)


def _attend_and_roll(cols_ref, k_ref, v_ref, ko_ref, vo_ref, oc_ref, lc_ref, seq, row0, dil, hb):
    nh, dh, w = k_ref.shape
    npad = oc_ref.shape[1]
    pos = lax.broadcasted_iota(jnp.int32, (1, w), 1)
    attended = (pos & (dil - 1)) == 0
    newest = pos == w - 1
    mine = lax.broadcasted_iota(jnp.int32, (1, npad), 1) == seq

    for grp in range(nh // hb):
        first = row0 + grp * hb * dh
        rows = (slice(first, first + hb * dh) if isinstance(first, int)
                else pl.ds(pl.multiple_of(first, dh), hb * dh))
        hs = slice(grp * hb, (grp + 1) * hb)

        def column(j):
            col = jnp.sum(jnp.where(mine, cols_ref[j, rows, :], 0.0), axis=-1, keepdims=True)
            return col.reshape(hb, dh, 1)

        def shifted(tile, new_col):
            flat = pltpu.roll(tile.reshape(hb * dh, w), w - 1, axis=1)
            return jnp.where(newest, new_col, flat.reshape(hb, dh, w))

        qc, kc, vc = column(0), column(1), column(2)
        kt = k_ref[hs]
        vt = v_ref[hs]
        s = jnp.where(attended, jnp.sum(kt * qc, axis=1, keepdims=True), -jnp.inf)
        s_new = jnp.sum(kc * qc, axis=1, keepdims=True)
        m = jnp.maximum(jnp.max(s, axis=-1, keepdims=True), s_new)
        p = jnp.exp(s - m)
        p_new = jnp.exp(s_new - m)
        den = jnp.sum(p, axis=-1, keepdims=True) + p_new
        o = (jnp.sum(vt * p, axis=-1, keepdims=True) + p_new * vc) / den
        lse = jnp.broadcast_to(m + jnp.log(den), (hb, dh, 1))
        oc_ref[rows, :] = jnp.where(mine, o.reshape(hb * dh, 1), oc_ref[rows, :])
        lc_ref[rows, :] = jnp.where(mine, lse.reshape(hb * dh, 1), lc_ref[rows, :])
        ko_ref[hs] = shifted(kt, kc)
        vo_ref[hs] = shifted(vt, vc)


def _attn_sample_kernel(cols_ref, k_ref, v_ref, oc_ref, lc_ref, ko_ref, vo_ref, *, dil):
    step = pl.program_id(0)
    bb, heads = k_ref.shape[:2]

    @pl.when(step == 0)
    def _():
        oc_ref[...] = jnp.zeros_like(oc_ref)
        lc_ref[...] = jnp.zeros_like(lc_ref)

    for i in range(bb):
        _attend_and_roll(cols_ref, k_ref.at[i], v_ref.at[i], ko_ref.at[i], vo_ref.at[i],
                         oc_ref, lc_ref, step * bb + i, 0, dil, heads)


def _to_lanes(cache):
    return jnp.transpose(cache, (0, 2, 3, 1))


def _to_rows(cache):
    return jnp.transpose(cache, (0, 3, 1, 2))


def _attn_sample_group(cols, k_cache, v_cache, dil):
    n, w = k_cache.shape[:2]
    assert w % dil == 0 and w // dil == ATT_BLK
    bb = SAMPLE_BB[w]
    cache_spec = pl.BlockSpec((bb, HEADS, HEAD_DIM, w), lambda i: (i, 0, 0, 0))
    col_spec = _full_spec((ATT_W, SAMPLE_PAD))
    cache_shape = jax.ShapeDtypeStruct((n, HEADS, HEAD_DIM, w), F32)
    col_shape = jax.ShapeDtypeStruct((ATT_W, SAMPLE_PAD), F32)
    oc, lc, k_out, v_out = pl.pallas_call(
        functools.partial(_attn_sample_kernel, dil=dil),
        grid=(n // bb,),
        in_specs=[_full_spec(cols.shape)] + [cache_spec] * 2,
        out_specs=[col_spec] * 2 + [cache_spec] * 2,
        out_shape=[col_shape] * 2 + [cache_shape] * 2,
        compiler_params=pltpu.CompilerParams(
            dimension_semantics=("arbitrary",), vmem_limit_bytes=VMEM_LIMIT),
        name=f"attn_sample_d{dil}",
    )(cols, _to_lanes(k_cache), _to_lanes(v_cache))
    return oc, lc, _to_rows(k_out), _to_rows(v_out)


def _merge_kernel(x_ref, o0_ref, o1_ref, o2_ref, l0_ref, l1_ref, l2_ref, wo_ref, y_ref):
    l0, l1, l2 = l0_ref[...], l1_ref[...], l2_ref[...]
    m = jnp.maximum(jnp.maximum(l0, l1), l2)
    e0, e1, e2 = jnp.exp(l0 - m), jnp.exp(l1 - m), jnp.exp(l2 - m)
    den = e0 + e1 + e2
    o = (e0 / den) * o0_ref[...] + (e1 / den) * o1_ref[...] + (e2 / den) * o2_ref[...]
    y_ref[...] = x_ref[...] + _bdot(o, wo_ref[...])


def _merge(x, outs, lses, wo, tm):
    m, d = x.shape
    w = wo.shape[0]
    row = lambda i: (i, 0)
    part = pl.BlockSpec((tm, w), row)
    return pl.pallas_call(
        _merge_kernel,
        grid=(m // tm,),
        in_specs=[pl.BlockSpec((tm, d), row)] + [part] * 6 + [_weight_spec(wo)],
        out_specs=pl.BlockSpec((tm, d), row),
        out_shape=jax.ShapeDtypeStruct((m, d), F32),
        compiler_params=pltpu.CompilerParams(
            dimension_semantics=("arbitrary",), vmem_limit_bytes=VMEM_LIMIT),
        name="attn_merge",
    )(x, *outs, *lses, wo)


def kernel(x_prompt, x_sample, state_lru_h, state_conv_a, state_ffn_conv, cache_k0, cache_v0, cache_k1, cache_v1, cache_k2, cache_v2, norm_mix, a_w_in, a_conv_w, a_conv_b, a_gate_a_w, a_gate_a_b, a_gate_x_w, a_gate_x_b, a_lambda, a_w_out, kv_norm, w_kv, b_w_q, b_w_o, norm_ffn, ffn_w_up, ffn_conv_w, ffn_conv_b, ffn_w_down, final_norm):
    b, t, d = x_prompt.shape
    n = x_sample.shape[0]
    assert x_sample.shape[1] == 1 and norm_mix.shape[0] == 2
    caches = (cache_k0, cache_v0, cache_k1, cache_v1, cache_k2, cache_v2)
    row = lambda v: v.reshape(1, -1)

    mixer_w = (row(norm_mix[0]), a_w_in[0].astype(BF16), a_conv_w[0], row(a_conv_b[0]),
               a_gate_a_w[0].astype(BF16), row(a_gate_a_b[0]),
               a_gate_x_w[0].astype(BF16), row(a_gate_x_b[0]),
               row(a_lambda[0]), a_w_out[0].astype(BF16))
    w_up, w_down = ffn_w_up.astype(BF16), ffn_w_down.astype(BF16)
    ffn_w = [(row(norm_ffn[l]), _Layer(w_up, l), ffn_conv_w[l],
              row(ffn_conv_b[l]), _Layer(w_down, l)) for l in range(2)]
    ffn_w[1] = ffn_w[1] + (row(final_norm),)
    gq, gkv = row(norm_mix[1]), row(kv_norm)
    wq, wkv, wo = b_w_q[0].astype(BF16), w_kv.astype(BF16), b_w_o[0].astype(BF16)

    xs = x_sample.reshape(n, d)
    conv_buf = jnp.swapaxes(state_conv_a[0], 0, 1)
    hs, s_h, s_ur = _mixer_sample(xs, state_lru_h[0], conv_buf, mixer_w)
    s_ca = jnp.concatenate([state_conv_a[0][:, 1:], s_ur[:, None]], axis=1)
    hs, s_ug0 = _ffn_sample(hs, jnp.swapaxes(state_ffn_conv[0], 0, 1), ffn_w[0], final_norm=False)
    qs, kvs = _qkv(hs, gq, gkv, wq, wkv, n)
    cols = _sample_cols(qs, kvs)

    h, p_h, p_ca = _mixer_prompt(x_prompt, mixer_w)
    wide = N_GROUPS - 1
    h, p_f0, oc_wide, lc_wide, k_wide, v_wide = _ffn_prompt(
        h, ffn_w[0], False, TM_FFN_WITH_CACHE,
        cache=(cols[3 * wide:3 * wide + 3], _to_lanes(caches[2 * wide]),
               _to_lanes(caches[2 * wide + 1]), GROUPS[wide][1]))
    qkv = _qkv_prompt(h, gq, gkv, wq, wkv)
    outs, lses, p_kv = [], [], []
    for g, (win, dil) in enumerate(GROUPS):
        o, lse = _attn_prompt_group(*qkv[g], g)
        outs.append(o)
        lses.append(lse)
        for tail in _kv_tail(h, gkv, wkv, g, min(win, t)):
            p_kv.append(_to_rows(tail.reshape(b, HEADS, HEAD_DIM, -1)))
    h = _merge_prompt(h, outs, lses, wo)
    y_prompt, p_f1 = _ffn_prompt(h, ffn_w[1], True, TM_FFN)

    ocs, lcs, s_kv = [], [], []
    for g, (_, dil) in enumerate(GROUPS[:wide]):
        oc, lc, k_out, v_out = _attn_sample_group(
            cols[3 * g:3 * g + 3], caches[2 * g], caches[2 * g + 1], dil)
        ocs.append(oc)
        lcs.append(lc)
        s_kv += [k_out, v_out]
    s_kv += [_to_rows(k_wide), _to_rows(v_wide)]
    os_, lses_s = _sample_rows(jnp.stack(ocs + [oc_wide]), jnp.stack(lcs + [lc_wide]))
    part = lambda a, j: a[:n, j * ATT_W:(j + 1) * ATT_W]
    hs = _merge(hs, [part(os_, g) for g in range(N_GROUPS)],
                [part(lses_s, g) for g in range(N_GROUPS)], wo, n)
    y_sample, s_ug1 = _ffn_sample(hs, jnp.swapaxes(state_ffn_conv[1], 0, 1), ffn_w[1], final_norm=True)
    s_ffn = jnp.stack([
        jnp.concatenate([state_ffn_conv[l][:, 1:], ug[:, None]], axis=1)
        for l, ug in enumerate((s_ug0, s_ug1))])
    return (y_prompt, y_sample.reshape(n, 1, d),
            p_h.reshape(1, b, d), s_h.reshape(1, n, d),
            p_ca.reshape(1, b, -1, d), s_ca.reshape(1, n, -1, d),
            jnp.stack([p_f0, p_f1]), s_ffn,
            p_kv[0], p_kv[1], s_kv[0], s_kv[1],
            p_kv[2], p_kv[3], s_kv[2], s_kv[3],
            p_kv[4], p_kv[5], s_kv[4], s_kv[5])
```

```python
import functools

import jax
import jax.numpy as jnp
from jax import lax
from jax.experimental import pallas as pl
from jax.experimental.pallas import tpu as pltpu

F32 = jnp.float32
BF16 = jnp.bfloat16

EPS = 1e-6
LRU_C = 8.0
N_LRU_BLOCKS = 4
HEAD_DIM = 64
HEADS = 8
GROUPS = ((128, 1), (512, 4), (2048, 16))
N_GROUPS = len(GROUPS)
ATT_W = HEADS * HEAD_DIM
ATT_BLK = 128
LANES = 128
CARRY_ROWS = 8

TQ_MIXER = 256
TM_FFN = 512
TM_FFN_WITH_CACHE = 256
FC_FFN = 512
TM_TOKEN = 512
HEADS_TOGETHER_MAX_W = 512
VMEM_LIMIT = 52 * 1024 * 1024


def _rmsnorm(x, g):
    return x * lax.rsqrt(jnp.mean(x * x, axis=-1, keepdims=True) + EPS) * g


def _bdot(a, w):
    return jnp.dot(a.astype(BF16), w, preferred_element_type=F32)


def _shift_rows(cur, prev, s):
    rolled = pltpu.roll(cur, s, axis=0)
    prev_rolled = pltpu.roll(prev, s, axis=0)
    row = lax.broadcasted_iota(jnp.int32, prev.shape, 0)
    head = jnp.where(row < s, prev_rolled, rolled[:CARRY_ROWS])
    return jnp.concatenate([head, rolled[CARRY_ROWS:]], axis=0)


def _causal_conv_rows(cur, prev, w_ref, b):
    k = w_ref.shape[0]
    y = b + w_ref[k - 1:k, :] * cur
    for j in range(k - 1):
        y = y + w_ref[j:j + 1, :] * _shift_rows(cur, prev, k - 1 - j)
    return y


def _log_sigmoid(x):
    return jnp.minimum(x, 0.0) - jnp.log1p(jnp.exp(-jnp.abs(x)))


def _lru_coeffs(xc, wa_ref, ba, wx_ref, bx, lam):
    xb = xc.astype(BF16)
    blk = xc.shape[-1] // N_LRU_BLOCKS
    ra, ri = [], []
    for n in range(N_LRU_BLOCKS):
        xs = xb[:, n * blk:(n + 1) * blk]
        ra.append(jnp.dot(xs, wa_ref[n], preferred_element_type=F32))
        ri.append(jnp.dot(xs, wx_ref[n], preferred_element_type=F32))
    r = jax.nn.sigmoid(jnp.concatenate(ra, axis=-1) + ba)
    i = jax.nn.sigmoid(jnp.concatenate(ri, axis=-1) + bx)
    log_a = LRU_C * r * _log_sigmoid(lam)
    a = jnp.exp(log_a)
    one_minus_a2 = (1.0 + a * a) * jnp.tanh(-log_a)
    bt = jnp.sqrt(one_minus_a2) * (i * xc)
    return a, bt


def _scan_rows(a, b, h_init):
    n, d = a.shape
    groups = n // CARRY_ROWS
    a = a.reshape(groups, CARRY_ROWS, d)
    b = b.reshape(groups, CARRY_ROWS, d)
    row = lax.broadcasted_iota(jnp.int32, a.shape, 1)
    s = 1
    while s < CARRY_ROWS:
        keep = row >= s
        a_sh = jnp.where(keep, pltpu.roll(a, s, axis=1), 1.0)
        b_sh = jnp.where(keep, pltpu.roll(b, s, axis=1), 0.0)
        b = a * b_sh + b
        a = a * a_sh
        s *= 2
    h = h_init
    out = []
    for k in range(groups):
        hk = a[k] * h + b[k]
        out.append(hk)
        h = hk[CARRY_ROWS - 1:, :]
    return jnp.concatenate(out, axis=0)


def _mixer_prompt_kernel(x_ref, g_ref, win_ref, cw_ref, cb_ref, wa_ref, ba_ref,
                         wx_ref, bx_ref, lam_ref, wout_ref,
                         y_ref, hlast_ref, cbuf_ref, ucarry, hcarry):
    t = pl.program_id(1)
    tq, d = x_ref.shape

    @pl.when(t == 0)
    def _():
        ucarry[...] = jnp.zeros_like(ucarry)
        hcarry[...] = jnp.zeros_like(hcarry)

    x = x_ref[...]
    u = _bdot(_rmsnorm(x, g_ref[...]), win_ref[...])
    gate = jax.nn.gelu(u[:, :d])
    ur = u[:, d:]
    xc = _causal_conv_rows(ur, ucarry[...], cw_ref, cb_ref[...])
    ucarry[...] = ur[tq - CARRY_ROWS:, :]
    a, bt = _lru_coeffs(xc, wa_ref, ba_ref[...], wx_ref, bx_ref[...], lam_ref[...])
    h = _scan_rows(a, bt, hcarry[...])
    hcarry[...] = h[tq - 1:tq, :]
    y_ref[...] = x + _bdot(h * gate, wout_ref[...])

    @pl.when(t == pl.num_programs(1) - 1)
    def _():
        hlast_ref[...] = h[tq - 1:tq, :]
        k = cw_ref.shape[0]
        cbuf_ref[...] = ur[tq - (k - 1):, :]


def _mixer_sample_kernel(x_ref, h0_ref, buf_ref, g_ref, win_ref, cw_ref, cb_ref,
                         wa_ref, ba_ref, wx_ref, bx_ref, lam_ref, wout_ref,
                         y_ref, hnew_ref, ur_ref):
    d = x_ref.shape[-1]
    k = cw_ref.shape[0]
    x = x_ref[...]
    u = _bdot(_rmsnorm(x, g_ref[...]), win_ref[...])
    gate = jax.nn.gelu(u[:, :d])
    ur = u[:, d:]
    xc = cb_ref[...] + cw_ref[k - 1:k, :] * ur
    for j in range(k - 1):
        xc = xc + cw_ref[j:j + 1, :] * buf_ref[j]
    a, bt = _lru_coeffs(xc, wa_ref, ba_ref[...], wx_ref, bx_ref[...], lam_ref[...])
    h = a * h0_ref[...] + bt
    hnew_ref[...] = h
    ur_ref[...] = ur
    y_ref[...] = x + _bdot(h * gate, wout_ref[...])


def _full_spec(shape):
    zeros = (0,) * len(shape)
    return pl.BlockSpec(shape, lambda *_: zeros)


class _Layer:
    def __init__(self, stacked, index):
        self.stacked, self.index = stacked, index

    @property
    def shape(self):
        return self.stacked.shape[1:]


def _weight_spec(w):
    zeros = (0,) * len(w.shape)
    if isinstance(w, _Layer):
        return pl.BlockSpec((None,) + w.shape, lambda *_: (w.index,) + zeros,
                            pipeline_mode=pl.Buffered(1))
    return pl.BlockSpec(w.shape, lambda *_: zeros, pipeline_mode=pl.Buffered(1))


def _weight_specs(weights):
    return [_weight_spec(w) for w in weights]


def _weight_arrays(weights):
    return [w.stacked if isinstance(w, _Layer) else w for w in weights]


def _mixer_prompt(x, weights):
    b, t, d = x.shape
    tq = TQ_MIXER
    k = weights[2].shape[0]
    row = lambda bi, ti: (bi, ti, 0)
    per_seq = lambda bi, ti: (bi, 0, 0)
    return pl.pallas_call(
        _mixer_prompt_kernel,
        grid=(b, t // tq),
        in_specs=[pl.BlockSpec((None, tq, d), row)] + _weight_specs(weights),
        out_specs=[pl.BlockSpec((None, tq, d), row),
                   pl.BlockSpec((None, 1, d), per_seq),
                   pl.BlockSpec((None, k - 1, d), per_seq)],
        out_shape=[jax.ShapeDtypeStruct((b, t, d), F32),
                   jax.ShapeDtypeStruct((b, 1, d), F32),
                   jax.ShapeDtypeStruct((b, k - 1, d), F32)],
        scratch_shapes=[pltpu.VMEM((CARRY_ROWS, d), F32), pltpu.VMEM((1, d), F32)],
        compiler_params=pltpu.CompilerParams(
            dimension_semantics=("arbitrary", "arbitrary"),
            vmem_limit_bytes=VMEM_LIMIT),
        name="mixer_prompt",
    )(x, *weights)


def _mixer_sample(x, h0, buf, weights):
    n, d = x.shape
    k = weights[2].shape[0]
    return pl.pallas_call(
        _mixer_sample_kernel,
        grid=(1,),
        in_specs=[_full_spec((n, d)), _full_spec((n, d)), _full_spec((k - 1, n, d))]
        + _weight_specs(weights),
        out_specs=[_full_spec((n, d))] * 3,
        out_shape=[jax.ShapeDtypeStruct((n, d), F32)] * 3,
        compiler_params=pltpu.CompilerParams(
            dimension_semantics=("arbitrary",), vmem_limit_bytes=VMEM_LIMIT),
        name="mixer_sample",
    )(x, h0, buf, *weights)


def _ffn_chunks(xb, x, wup_ref, cw_ref, cb_ref, wdown_ref, conv_gate, act_ref=None, fc=FC_FFN):
    f = wdown_ref.shape[0]
    acc = x
    for lo in range(0, f, fc):
        hi = lo + fc
        ug = jnp.dot(xb, wup_ref[:, lo:hi], preferred_element_type=F32)
        uv = jnp.dot(xb, wup_ref[:, f + lo:f + hi], preferred_element_type=F32)
        gc = conv_gate(ug, lo, hi)
        act = (jax.nn.gelu(gc) * uv).astype(BF16)
        if act_ref is None:
            acc = acc + jnp.dot(act, wdown_ref[lo:hi, :], preferred_element_type=F32)
        else:
            act_ref[:, lo:hi] = act
    if act_ref is not None:
        acc = acc + jnp.dot(act_ref[...], wdown_ref[...], preferred_element_type=F32)
    return acc


def _ffn_prompt_kernel(*refs, final_norm, fc, cache_dils, attention):
    refs = list(refs)
    x_ref, g_ref, wup_ref, cw_ref, cb_ref, wdown_ref = refs[:6]
    del refs[:6]
    fn_ref = refs.pop(0) if final_norm else None
    if attention:
        att_refs = refs[:2 * N_GROUPS + 2]
        del refs[:2 * N_GROUPS + 2]
    cache_in = [refs[3 * j:3 * j + 3] for j in range(len(cache_dils))]
    del refs[:3 * len(cache_dils)]
    y_ref, fbuf_ref = refs[:2]
    del refs[:2]
    cache_out = [refs[4 * j:4 * j + 4] for j in range(len(cache_dils))]
    del refs[:4 * len(cache_dils)]
    gcarry, act_ref = refs[:2]
    att_scratch = refs[2:]
    t = pl.program_id(1)
    tm = x_ref.shape[0]
    k = cw_ref.shape[0]

    @pl.when(t == 0)
    def _():
        gcarry[...] = jnp.zeros_like(gcarry)

    step = pl.program_id(0) * pl.num_programs(1) + t
    for dil, (cols_ref, kc_ref, vc_ref), (oc_ref, lc_ref, ko_ref, vo_ref) in zip(
            cache_dils, cache_in, cache_out):
        nh, dh, w = kc_ref.shape
        per_seq = HEADS // nh

        @pl.when(step == 0)
        def _():
            oc_ref[...] = jnp.zeros_like(oc_ref)
            lc_ref[...] = jnp.zeros_like(lc_ref)

        _attend_and_roll(cols_ref, kc_ref, vc_ref, ko_ref, vo_ref, oc_ref, lc_ref,
                         step // per_seq, (step % per_seq) * (nh * dh), dil,
                         1 if w > HEADS_TOGETHER_MAX_W else nh)

    def conv_gate(ug, lo, hi):
        gc = _causal_conv_rows(ug, gcarry[:, lo:hi], cw_ref.at[:, lo:hi], cb_ref[:, lo:hi])
        gcarry[:, lo:hi] = ug[tm - CARRY_ROWS:, :]
        return gc

    x = x_ref[...]
    if attention:
        x = _attention_residual(x, att_refs[:N_GROUPS], att_refs[N_GROUPS:2 * N_GROUPS],
                                att_refs[-2], att_refs[-1], *att_scratch)
    xb = _rmsnorm(x, g_ref[...]).astype(BF16)
    y = _ffn_chunks(xb, x, wup_ref, cw_ref, cb_ref, wdown_ref, conv_gate, act_ref, fc)
    if final_norm:
        y = _rmsnorm(y, fn_ref[...])
    y_ref[...] = y

    @pl.when(t == pl.num_programs(1) - 1)
    def _():
        fbuf_ref[...] = gcarry[CARRY_ROWS - (k - 1):, :]


def _ffn_sample_kernel(*refs, final_norm):
    if final_norm:
        (x_ref, buf_ref, g_ref, wup_ref, cw_ref, cb_ref, wdown_ref, fn_ref,
         y_ref, ug_ref) = refs
    else:
        (x_ref, buf_ref, g_ref, wup_ref, cw_ref, cb_ref, wdown_ref,
         y_ref, ug_ref) = refs
    k = cw_ref.shape[0]

    def conv_gate(ug, lo, hi):
        ug_ref[:, lo:hi] = ug
        gc = cb_ref[:, lo:hi] + cw_ref[k - 1:k, lo:hi] * ug
        for j in range(k - 1):
            gc = gc + cw_ref[j:j + 1, lo:hi] * buf_ref[j, :, lo:hi]
        return gc

    x = x_ref[...]
    xb = _rmsnorm(x, g_ref[...]).astype(BF16)
    y = _ffn_chunks(xb, x, wup_ref, cw_ref, cb_ref, wdown_ref, conv_gate)
    if final_norm:
        y = _rmsnorm(y, fn_ref[...])
    y_ref[...] = y


def _ffn_prompt(x, weights, final_norm, tm, caches=(), attention=None):
    b, t, d = x.shape
    f = weights[4].shape[0]
    k = weights[2].shape[0]
    nt = t // tm
    row = lambda bi, ti: (bi, ti, 0)
    per_seq = lambda bi, ti: (bi, 0, 0)
    in_specs = [pl.BlockSpec((None, tm, d), row)] + _weight_specs(weights)
    out_specs = [pl.BlockSpec((None, tm, d), row), pl.BlockSpec((None, k - 1, f), per_seq)]
    out_shape = [jax.ShapeDtypeStruct((b, t, d), F32), jax.ShapeDtypeStruct((b, k - 1, f), F32)]
    operands = [x, *_weight_arrays(weights)]
    scratch = [pltpu.VMEM((CARRY_ROWS, f), F32), pltpu.VMEM((tm, f), BF16)]
    if attention is not None:
        att_specs, att_operands, att_scratch = _attention_residual_operands(*attention, tm)
        in_specs += att_specs
        operands += att_operands
        scratch += att_scratch
    cache_in_specs, cache_out_specs, cache_out_shape, cache_operands = [], [], [], []
    for cols, kc, vc, _ in caches:
        n, heads, dh, w = kc.shape
        steps_per_seq = (b * nt) // n
        nh = heads // steps_per_seq
        assert steps_per_seq * n == b * nt and nh * steps_per_seq == heads
        blk = pl.BlockSpec((None, nh, dh, w),
                           lambda bi, ti, s=steps_per_seq: ((bi * nt + ti) // s, (bi * nt + ti) % s, 0, 0))
        col = _full_spec(cols.shape[1:])
        cache_in_specs += [_full_spec(cols.shape), blk, blk]
        cache_out_specs += [col, col, blk, blk]
        cache_out_shape += [jax.ShapeDtypeStruct(cols.shape[1:], F32)] * 2
        cache_out_shape += [jax.ShapeDtypeStruct(kc.shape, F32)] * 2
        cache_operands += [cols, kc, vc]
    in_specs += cache_in_specs
    out_specs += cache_out_specs
    out_shape += cache_out_shape
    operands += cache_operands
    return pl.pallas_call(
        functools.partial(_ffn_prompt_kernel, final_norm=final_norm, fc=FC_FFN,
                          cache_dils=tuple(c[3] for c in caches),
                          attention=attention is not None),
        grid=(b, nt),
        in_specs=in_specs,
        out_specs=out_specs,
        out_shape=out_shape,
        scratch_shapes=scratch,
        compiler_params=pltpu.CompilerParams(
            dimension_semantics=("arbitrary", "arbitrary"),
            vmem_limit_bytes=VMEM_LIMIT),
        name="ffn_prompt_final" if final_norm else "ffn_prompt",
    )(*operands)


def _ffn_sample(x, buf, weights, final_norm):
    n, d = x.shape
    f = weights[4].shape[0]
    k = weights[2].shape[0]
    return pl.pallas_call(
        functools.partial(_ffn_sample_kernel, final_norm=final_norm),
        grid=(1,),
        in_specs=[_full_spec((n, d)), _full_spec((k - 1, n, f))]
        + _weight_specs(weights),
        out_specs=[_full_spec((n, d)), _full_spec((n, f))],
        out_shape=[jax.ShapeDtypeStruct((n, d), F32), jax.ShapeDtypeStruct((n, f), F32)],
        compiler_params=pltpu.CompilerParams(
            dimension_semantics=("arbitrary",), vmem_limit_bytes=VMEM_LIMIT),
        name="ffn_sample_final" if final_norm else "ffn_sample",
    )(x, buf, *_weight_arrays(weights))


def _qkv_kernel(x_ref, gq_ref, gkv_ref, wq_ref, wkv_ref, q_ref, kv_ref):
    x = x_ref[...]
    inv = lax.rsqrt(jnp.mean(x * x, axis=-1, keepdims=True) + EPS)
    y = x * inv
    q_ref[...] = _bdot(y * gq_ref[...], wq_ref[...])
    kv_ref[...] = _bdot(y * gkv_ref[...], wkv_ref[...])


def _qkv(x, gq, gkv, wq, wkv, tm):
    m, d = x.shape
    nq, nkv = wq.shape[1], wkv.shape[1]
    row = lambda i: (i, 0)
    return pl.pallas_call(
        _qkv_kernel,
        grid=(m // tm,),
        in_specs=[pl.BlockSpec((tm, d), row)] + _weight_specs((gq, gkv, wq, wkv)),
        out_specs=[pl.BlockSpec((tm, nq), row), pl.BlockSpec((tm, nkv), row)],
        out_shape=[jax.ShapeDtypeStruct((m, nq), F32), jax.ShapeDtypeStruct((m, nkv), F32)],
        compiler_params=pltpu.CompilerParams(
            dimension_semantics=("arbitrary",), vmem_limit_bytes=VMEM_LIMIT),
        name="qkv_proj",
    )(x, gq, gkv, wq, wkv)


LSE_REP = ATT_BLK // HEADS


_NT = (((1,), (1,)), ((), ()))


def _qkv_prompt_kernel(x_ref, gq_ref, gkv_ref, wq_ref, wkv_ref, *out_and_scratch):
    out_refs, ybuf = out_and_scratch[:-1], out_and_scratch[-1]
    x = x_ref[...]
    y = x * lax.rsqrt(jnp.mean(x * x, axis=-1, keepdims=True) + EPS)
    chunks = ybuf.shape[0]
    for c in range(chunks):
        ybuf[c] = y[:, c * LANES:(c + 1) * LANES]
    for g in range(N_GROUPS):
        q_ref, k_ref, v_ref = out_refs[3 * g:3 * g + 3]
        dil, per = q_ref.shape[0], q_ref.shape[1]
        yg = y
        if dil > 1:
            yg = jnp.concatenate(
                [jnp.concatenate([ybuf[c, pl.ds(r, per, stride=dil), :] for c in range(chunks)], axis=1)
                 for r in range(dil)], axis=0)
        xq = (yg * gq_ref[...]).astype(BF16)
        xkv = (yg * gkv_ref[...]).astype(BF16)
        part = lambda ref, j: ref[:, j * ATT_W:(j + 1) * ATT_W]
        q = jnp.dot(xq, part(wq_ref, g), preferred_element_type=F32) * (HEAD_DIM ** -0.5)
        k = jnp.dot(xkv, part(wkv_ref, g), preferred_element_type=F32)
        v = jnp.dot(xkv, part(wkv_ref, N_GROUPS + g), preferred_element_type=F32)
        q_ref[...] = q.astype(BF16).reshape(dil, per, ATT_W)
        k_ref[...] = k.astype(BF16).reshape(dil, per, ATT_W)
        v_ref[...] = v.astype(BF16).reshape(dil, per, ATT_W)


def _qkv_prompt(x, gq, gkv, wq, wkv):
    b, t, d = x.shape
    tm = TM_TOKEN
    specs, shapes = [], []
    for _, dil in GROUPS:
        assert tm % (dil * 16) == 0
        specs += [pl.BlockSpec((None, dil, tm // dil, ATT_W), lambda bi, ti: (bi, 0, ti, 0))] * 3
        shapes += [jax.ShapeDtypeStruct((b, dil, t // dil, ATT_W), BF16)] * 3
    weights = (gq, gkv, wq, wkv)
    outs = pl.pallas_call(
        _qkv_prompt_kernel,
        grid=(b, t // tm),
        in_specs=[pl.BlockSpec((None, tm, d), lambda bi, ti: (bi, ti, 0))] + _weight_specs(weights),
        out_specs=specs,
        out_shape=shapes,
        scratch_shapes=[pltpu.VMEM((d // LANES, tm, LANES), F32)],
        compiler_params=pltpu.CompilerParams(
            dimension_semantics=("arbitrary", "arbitrary"), vmem_limit_bytes=VMEM_LIMIT),
        name="qkv_prompt",
    )(x, *weights)
    return [outs[3 * g:3 * g + 3] for g in range(N_GROUPS)]


def _kv_tail_kernel(x_ref, gkv_ref, wkv_ref, kt_ref, vt_ref, *, group):
    x = x_ref[...]
    y = x * lax.rsqrt(jnp.mean(x * x, axis=-1, keepdims=True) + EPS)
    xkv = (y * gkv_ref[...]).astype(BF16)
    part = lambda j: wkv_ref[:, j * ATT_W:(j + 1) * ATT_W]
    kt_ref[...] = jnp.dot(xkv, part(group), preferred_element_type=F32).T
    vt_ref[...] = jnp.dot(xkv, part(N_GROUPS + group), preferred_element_type=F32).T


def _kv_tail(x, gkv, wkv, group, win):
    b, t, d = x.shape
    assert t % win == 0
    out = pl.BlockSpec((None, ATT_W, win), lambda bi: (bi, 0, 0))
    weights = (gkv, wkv)
    return pl.pallas_call(
        functools.partial(_kv_tail_kernel, group=group),
        grid=(b,),
        in_specs=[pl.BlockSpec((None, win, d), lambda bi: (bi, t // win - 1, 0))]
        + _weight_specs(weights),
        out_specs=[out, out],
        out_shape=[jax.ShapeDtypeStruct((b, ATT_W, win), F32)] * 2,
        compiler_params=pltpu.CompilerParams(
            dimension_semantics=("arbitrary",), vmem_limit_bytes=VMEM_LIMIT),
        name=f"kv_tail_w{win}",
    )(x, *weights)


def _attn_prompt_kernel(q_ref, kp_ref, k_ref, vp_ref, v_ref, o_ref, lse_ref):
    nres, blk = kp_ref.shape[:2]
    nblk = q_ref.shape[1] // blk
    qi = lax.broadcasted_iota(jnp.int32, (blk, 2 * blk), 0)
    kj = lax.broadcasted_iota(jnp.int32, (blk, 2 * blk), 1)
    band = jnp.logical_or(jnp.logical_and(kj < blk, kj >= qi),
                          jnp.logical_and(kj >= blk, kj - blk <= qi))
    band_first = jnp.logical_and(band, jnp.logical_or(kj >= blk, pl.program_id(2) > 0))
    lane = lax.broadcasted_iota(jnp.int32, (blk, blk), 1)
    low_half = lane < HEAD_DIM
    head_of_lane = lane // LSE_REP
    zero = jnp.zeros((), BF16)
    for r, j in [(r, j) for r in range(nres) for j in range(nblk)]:
        rows = slice(j * blk, (j + 1) * blk)
        q = q_ref[r, rows, :]
        if j == 0:
            keep = band_first
            k = jnp.concatenate([kp_ref[r], k_ref[r, :blk, :]], axis=0)
            v = jnp.concatenate([vp_ref[r], v_ref[r, :blk, :]], axis=0)
        else:
            keep = band
            k = k_ref[r, (j - 1) * blk:(j + 1) * blk, :]
            v = v_ref[r, (j - 1) * blk:(j + 1) * blk, :]
        lse_tile = jnp.zeros((blk, blk), F32)
        for pair in range(HEADS // 2):
            cols = slice(pair * blk, (pair + 1) * blk)
            q2, k2, v2 = q[:, cols], k[:, cols], v[:, cols]
            outs = []
            for half in range(2):
                mine = low_half if half == 0 else jnp.logical_not(low_half)
                s = lax.dot_general(jnp.where(mine, q2, zero), k2, _NT, preferred_element_type=F32)
                s = jnp.where(keep, s, -jnp.inf)
                m = jnp.max(s, axis=-1, keepdims=True)
                p = jnp.exp(s - m)
                den = jnp.sum(p, axis=-1, keepdims=True)
                outs.append(jnp.dot(p.astype(BF16), v2, preferred_element_type=F32) / den)
                lse_tile = jnp.where(head_of_lane == 2 * pair + half, m + jnp.log(den), lse_tile)
            o_ref[r, rows, cols] = jnp.where(low_half, outs[0], outs[1])
        lse_ref[r, rows, :] = lse_tile


ATT_NBLK = 8


def _attn_prompt_group(q, k, v, g):
    b, dil, n, _ = q.shape
    nblk = min(ATT_NBLK, n // ATT_BLK)
    nres = min(dil, ATT_NBLK // nblk)
    tq = nblk * ATT_BLK
    assert n % tq == 0 and dil % nres == 0
    prev = lambda i: jnp.maximum(i * nblk - 1, 0)
    rows = lambda size, width: pl.BlockSpec((None, nres, size, width), lambda bi, r, i: (bi, r, i, 0))
    before = pl.BlockSpec((None, nres, ATT_BLK, ATT_W), lambda bi, r, i: (bi, r, prev(i), 0))
    return pl.pallas_call(
        _attn_prompt_kernel,
        grid=(b, dil // nres, n // tq),
        in_specs=[rows(tq, ATT_W), before, rows(tq, ATT_W), before, rows(tq, ATT_W)],
        out_specs=[rows(tq, ATT_W), rows(tq, ATT_BLK)],
        out_shape=[jax.ShapeDtypeStruct((b, dil, n, ATT_W), F32),
                   jax.ShapeDtypeStruct((b, dil, n, ATT_BLK), F32)],
        compiler_params=pltpu.CompilerParams(
            dimension_semantics=("arbitrary", "arbitrary", "arbitrary"),
            vmem_limit_bytes=VMEM_LIMIT),
        name=f"attn_prompt_g{g}",
    )(q, k, k, v, v)


def _attention_residual(x, o_refs, l_refs, sel_ref, wo_ref, obuf, lbuf):
    def token_order(ref, buf):
        dil, per, width = ref.shape
        if dil == 1:
            return ref[0]
        chunks = width // LANES
        for r in range(dil):
            rows = ref[r]
            for c in range(chunks):
                buf[c, pl.ds(r, per, stride=dil), :] = rows[:, c * LANES:(c + 1) * LANES]
        return jnp.concatenate([buf[c] for c in range(chunks)], axis=1)

    lses = [token_order(l_ref, lbuf) for l_ref in l_refs]
    m = jnp.maximum(jnp.maximum(lses[0], lses[1]), lses[2])
    es = [jnp.exp(l - m) for l in lses]
    den = es[0] + es[1] + es[2]
    comb = None
    for e, o_ref in zip(es, o_refs):
        w = e / den
        hi = w.astype(BF16)
        lo = (w - hi.astype(F32)).astype(BF16)
        wide = (jnp.dot(hi, sel_ref[...], preferred_element_type=F32)
                + jnp.dot(lo, sel_ref[...], preferred_element_type=F32))
        term = wide * token_order(o_ref, obuf)
        comb = term if comb is None else comb + term
    return x + _bdot(comb, wo_ref[...])


def _attention_residual_operands(outs, lses, wo, tm):
    head_of_lane = jnp.arange(ATT_BLK) // LSE_REP
    first_copy = jnp.arange(ATT_BLK) % LSE_REP == 0
    sel = ((head_of_lane[:, None] == (jnp.arange(ATT_W) // HEAD_DIM)[None, :])
           & first_copy[:, None]).astype(BF16)
    specs = []
    for width, parts in ((ATT_W, outs), (ATT_BLK, lses)):
        for p in parts:
            dil = p.shape[1]
            specs.append(pl.BlockSpec((None, dil, tm // dil, width), lambda bi, ti: (bi, 0, ti, 0)))
    specs += _weight_specs((sel, wo))
    scratch = [pltpu.VMEM((ATT_W // LANES, tm, LANES), F32),
               pltpu.VMEM((ATT_BLK // LANES, tm, LANES), F32)]
    return specs, [*outs, *lses, sel, wo], scratch


def _exact_transpose(x):
    c = x.shape[1]
    eye = (lax.broadcasted_iota(jnp.int32, (c, c), 0)
           == lax.broadcasted_iota(jnp.int32, (c, c), 1)).astype(F32)
    return lax.dot_general(eye, x, (((1,), (1,)), ((), ())),
                           precision=lax.Precision.HIGHEST, preferred_element_type=F32)


SAMPLE_PAD = 128
SAMPLE_BB = {128: 8, 512: 4}


def _sample_cols_kernel(q_ref, kv_ref, cols_ref):
    for g in range(N_GROUPS):
        part = lambda ref, j: ref[:, j * ATT_W:(j + 1) * ATT_W]
        cols_ref[3 * g] = _exact_transpose(part(q_ref, g)) * (HEAD_DIM ** -0.5)
        cols_ref[3 * g + 1] = _exact_transpose(part(kv_ref, g))
        cols_ref[3 * g + 2] = _exact_transpose(part(kv_ref, N_GROUPS + g))


def _sample_rows_kernel(oc_ref, lc_ref, o_ref, lse_ref):
    for g in range(N_GROUPS):
        o_ref[:, g * ATT_W:(g + 1) * ATT_W] = _exact_transpose(oc_ref[g])
        lse_ref[:, g * ATT_W:(g + 1) * ATT_W] = _exact_transpose(lc_ref[g])


def _sample_cols(q, kv):
    pad = lambda x: jnp.pad(x, ((0, SAMPLE_PAD - x.shape[0]), (0, 0)))
    q, kv = pad(q), pad(kv)
    shape = (3 * N_GROUPS, ATT_W, SAMPLE_PAD)
    return pl.pallas_call(
        _sample_cols_kernel,
        grid=(1,),
        in_specs=[_full_spec(q.shape), _full_spec(kv.shape)],
        out_specs=_full_spec(shape),
        out_shape=jax.ShapeDtypeStruct(shape, F32),
        compiler_params=pltpu.CompilerParams(
            dimension_semantics=("arbitrary",), vmem_limit_bytes=VMEM_LIMIT),
        name="sample_cols",
    )(q, kv)


def _sample_rows(oc, lc):
    shape = (SAMPLE_PAD, N_GROUPS * ATT_W)
    return pl.pallas_call(
        _sample_rows_kernel,
        grid=(1,),
        in_specs=[_full_spec(oc.shape), _full_spec(lc.shape)],
        out_specs=[_full_spec(shape)] * 2,
        out_shape=[jax.ShapeDtypeStruct(shape, F32)] * 2,
        compiler_params=pltpu.CompilerParams(
            dimension_semantics=("arbitrary",), vmem_limit_bytes=VMEM_LIMIT),
        name="sample_rows",
    )(oc, lc)


def _attend_and_roll(cols_ref, k_ref, v_ref, ko_ref, vo_ref, oc_ref, lc_ref, seq, row0, dil, hb):
    nh, dh, w = k_ref.shape
    npad = oc_ref.shape[1]
    pos = lax.broadcasted_iota(jnp.int32, (1, w), 1)
    attended = (pos & (dil - 1)) == 0
    newest = pos == w - 1
    mine = lax.broadcasted_iota(jnp.int32, (1, npad), 1) == seq

    for grp in range(nh // hb):
        first = row0 + grp * hb * dh
        rows = (slice(first, first + hb * dh) if isinstance(first, int)
                else pl.ds(pl.multiple_of(first, dh), hb * dh))
        hs = slice(grp * hb, (grp + 1) * hb)

        def column(j):
            col = jnp.sum(jnp.where(mine, cols_ref[j, rows, :], 0.0), axis=-1, keepdims=True)
            return col.reshape(hb, dh, 1)

        def shifted(tile, new_col):
            flat = pltpu.roll(tile.reshape(hb * dh, w), w - 1, axis=1)
            return jnp.where(newest, new_col, flat.reshape(hb, dh, w))

        qc, kc, vc = column(0), column(1), column(2)
        kt = k_ref[hs]
        vt = v_ref[hs]
        s = jnp.where(attended, jnp.sum(kt * qc, axis=1, keepdims=True), -jnp.inf)
        s_new = jnp.sum(kc * qc, axis=1, keepdims=True)
        m = jnp.maximum(jnp.max(s, axis=-1, keepdims=True), s_new)
        p = jnp.exp(s - m)
        p_new = jnp.exp(s_new - m)
        den = jnp.sum(p, axis=-1, keepdims=True) + p_new
        o = (jnp.sum(vt * p, axis=-1, keepdims=True) + p_new * vc) / den
        lse = jnp.broadcast_to(m + jnp.log(den), (hb, dh, 1))
        oc_ref[rows, :] = jnp.where(mine, o.reshape(hb * dh, 1), oc_ref[rows, :])
        lc_ref[rows, :] = jnp.where(mine, lse.reshape(hb * dh, 1), lc_ref[rows, :])
        ko_ref[hs] = shifted(kt, kc)
        vo_ref[hs] = shifted(vt, vc)


def _attn_sample_kernel(cols_ref, k_ref, v_ref, oc_ref, lc_ref, ko_ref, vo_ref, *, dil):
    step = pl.program_id(0)
    bb, heads = k_ref.shape[:2]

    @pl.when(step == 0)
    def _():
        oc_ref[...] = jnp.zeros_like(oc_ref)
        lc_ref[...] = jnp.zeros_like(lc_ref)

    for i in range(bb):
        _attend_and_roll(cols_ref, k_ref.at[i], v_ref.at[i], ko_ref.at[i], vo_ref.at[i],
                         oc_ref, lc_ref, step * bb + i, 0, dil, heads)


def _to_lanes(cache):
    return jnp.transpose(cache, (0, 2, 3, 1))


def _to_rows(cache):
    return jnp.transpose(cache, (0, 3, 1, 2))


def _attn_sample_group(cols, k_cache, v_cache, dil):
    n, heads, dh, w = k_cache.shape
    assert w % dil == 0 and w // dil == ATT_BLK
    bb = SAMPLE_BB[w]
    cache_spec = pl.BlockSpec((bb, heads, dh, w), lambda i: (i, 0, 0, 0))
    col_spec = _full_spec((ATT_W, SAMPLE_PAD))
    col_shape = jax.ShapeDtypeStruct((ATT_W, SAMPLE_PAD), F32)
    return pl.pallas_call(
        functools.partial(_attn_sample_kernel, dil=dil),
        grid=(n // bb,),
        in_specs=[_full_spec(cols.shape)] + [cache_spec] * 2,
        out_specs=[col_spec] * 2 + [cache_spec] * 2,
        out_shape=[col_shape] * 2 + [jax.ShapeDtypeStruct(k_cache.shape, F32)] * 2,
        compiler_params=pltpu.CompilerParams(
            dimension_semantics=("arbitrary",), vmem_limit_bytes=VMEM_LIMIT),
        name=f"attn_sample_d{dil}",
    )(cols, k_cache, v_cache)


def _merge_kernel(x_ref, o0_ref, o1_ref, o2_ref, l0_ref, l1_ref, l2_ref, wo_ref, y_ref):
    l0, l1, l2 = l0_ref[...], l1_ref[...], l2_ref[...]
    m = jnp.maximum(jnp.maximum(l0, l1), l2)
    e0, e1, e2 = jnp.exp(l0 - m), jnp.exp(l1 - m), jnp.exp(l2 - m)
    den = e0 + e1 + e2
    o = (e0 / den) * o0_ref[...] + (e1 / den) * o1_ref[...] + (e2 / den) * o2_ref[...]
    y_ref[...] = x_ref[...] + _bdot(o, wo_ref[...])


def _merge(x, outs, lses, wo, tm):
    m, d = x.shape
    w = wo.shape[0]
    row = lambda i: (i, 0)
    part = pl.BlockSpec((tm, w), row)
    return pl.pallas_call(
        _merge_kernel,
        grid=(m // tm,),
        in_specs=[pl.BlockSpec((tm, d), row)] + [part] * 6 + [_weight_spec(wo)],
        out_specs=pl.BlockSpec((tm, d), row),
        out_shape=jax.ShapeDtypeStruct((m, d), F32),
        compiler_params=pltpu.CompilerParams(
            dimension_semantics=("arbitrary",), vmem_limit_bytes=VMEM_LIMIT),
        name="attn_merge",
    )(x, *outs, *lses, wo)


def kernel(x_prompt, x_sample, state_lru_h, state_conv_a, state_ffn_conv, cache_k0, cache_v0, cache_k1, cache_v1, cache_k2, cache_v2, norm_mix, a_w_in, a_conv_w, a_conv_b, a_gate_a_w, a_gate_a_b, a_gate_x_w, a_gate_x_b, a_lambda, a_w_out, kv_norm, w_kv, b_w_q, b_w_o, norm_ffn, ffn_w_up, ffn_conv_w, ffn_conv_b, ffn_w_down, final_norm):
    b, t, d = x_prompt.shape
    n = x_sample.shape[0]
    assert x_sample.shape[1] == 1 and norm_mix.shape[0] == 2
    caches = (cache_k0, cache_v0, cache_k1, cache_v1, cache_k2, cache_v2)
    row = lambda v: v.reshape(1, -1)

    mixer_w = (row(norm_mix[0]), a_w_in[0].astype(BF16), a_conv_w[0], row(a_conv_b[0]),
               a_gate_a_w[0].astype(BF16), row(a_gate_a_b[0]),
               a_gate_x_w[0].astype(BF16), row(a_gate_x_b[0]),
               row(a_lambda[0]), a_w_out[0].astype(BF16))
    w_up, w_down = ffn_w_up.astype(BF16), ffn_w_down.astype(BF16)
    ffn_w = [(row(norm_ffn[l]), _Layer(w_up, l), ffn_conv_w[l],
              row(ffn_conv_b[l]), _Layer(w_down, l)) for l in range(2)]
    ffn_w[1] = ffn_w[1] + (row(final_norm),)
    gq, gkv = row(norm_mix[1]), row(kv_norm)
    wq, wkv, wo = b_w_q[0].astype(BF16), w_kv.astype(BF16), b_w_o[0].astype(BF16)

    xs = x_sample.reshape(n, d)
    conv_buf = jnp.swapaxes(state_conv_a[0], 0, 1)
    hs, s_h, s_ur = _mixer_sample(xs, state_lru_h[0], conv_buf, mixer_w)
    s_ca = jnp.concatenate([state_conv_a[0][:, 1:], s_ur[:, None]], axis=1)
    hs, s_ug0 = _ffn_sample(hs, jnp.swapaxes(state_ffn_conv[0], 0, 1), ffn_w[0], final_norm=False)
    qs, kvs = _qkv(hs, gq, gkv, wq, wkv, n)
    cols = _sample_cols(qs, kvs)

    sample_groups = [(cols[3 * g:3 * g + 3], _to_lanes(caches[2 * g]), _to_lanes(caches[2 * g + 1]), dil)
                     for g, (_, dil) in enumerate(GROUPS)]
    h, p_h, p_ca = _mixer_prompt(x_prompt, mixer_w)
    h, p_f0, *rolled_wide = _ffn_prompt(h, ffn_w[0], False, TM_FFN_WITH_CACHE,
                                        caches=sample_groups[-1:])
    qkv = _qkv_prompt(h, gq, gkv, wq, wkv)
    outs, lses, p_kv = [], [], []
    for g, (win, dil) in enumerate(GROUPS):
        o, lse = _attn_prompt_group(*qkv[g], g)
        outs.append(o)
        lses.append(lse)
        for tail in _kv_tail(h, gkv, wkv, g, min(win, t)):
            p_kv.append(_to_rows(tail.reshape(b, HEADS, HEAD_DIM, -1)))
    y_prompt, p_f1 = _ffn_prompt(h, ffn_w[1], True, TM_FFN, attention=(outs, lses, wo))

    rolled = [_attn_sample_group(*grp) for grp in sample_groups[:-1]] + [rolled_wide]
    s_kv = [_to_rows(c) for r in rolled for c in r[2:]]
    os_, lses_s = _sample_rows(jnp.stack([r[0] for r in rolled]), jnp.stack([r[1] for r in rolled]))
    part = lambda a, j: a[:n, j * ATT_W:(j + 1) * ATT_W]
    hs = _merge(hs, [part(os_, g) for g in range(N_GROUPS)],
                [part(lses_s, g) for g in range(N_GROUPS)], wo, n)
    y_sample, s_ug1 = _ffn_sample(hs, jnp.swapaxes(state_ffn_conv[1], 0, 1), ffn_w[1], final_norm=True)
    s_ffn = jnp.stack([
        jnp.concatenate([state_ffn_conv[l][:, 1:], ug[:, None]], axis=1)
        for l, ug in enumerate((s_ug0, s_ug1))])
    return (y_prompt, y_sample.reshape(n, 1, d),
            p_h.reshape(1, b, d), s_h.reshape(1, n, d),
            p_ca.reshape(1, b, -1, d), s_ca.reshape(1, n, -1, d),
            jnp.stack([p_f0, p_f1]), s_ffn,
            p_kv[0], p_kv[1], s_kv[0], s_kv[1],
            p_kv[2], p_kv[3], s_kv[2], s_kv[3],
            p_kv[4], p_kv[5], s_kv[4], s_kv[5])
```

```python
import functools

import jax
import jax.numpy as jnp
from jax import lax
from jax.experimental import pallas as pl
from jax.experimental.pallas import tpu as pltpu

F32 = jnp.float32
BF16 = jnp.bfloat16

EPS = 1e-6
LRU_C = 8.0
N_LRU_BLOCKS = 4
HEAD_DIM = 64
HEADS = 8
GROUPS = ((128, 1), (512, 4), (2048, 16))
N_GROUPS = len(GROUPS)
ATT_W = HEADS * HEAD_DIM
ATT_BLK = 128
LANES = 128
CARRY_ROWS = 8

TQ_MIXER = 256
STAGE_SKEW_ROWS = 8
TM_FFN = 512
TM_FFN_WITH_CACHE = 256
FC_FFN = 512
TM_TOKEN = 512
HEADS_TOGETHER_MAX_W = 512
VMEM_LIMIT = 52 * 1024 * 1024


def _rmsnorm(x, g):
    return x * lax.rsqrt(jnp.mean(x * x, axis=-1, keepdims=True) + EPS) * g


def _bdot(a, w):
    return jnp.dot(a.astype(BF16), w, preferred_element_type=F32)


def _shift_rows(cur, prev, s):
    rolled = pltpu.roll(cur, s, axis=0)
    prev_rolled = pltpu.roll(prev, s, axis=0)
    row = lax.broadcasted_iota(jnp.int32, prev.shape, 0)
    head = jnp.where(row < s, prev_rolled, rolled[:CARRY_ROWS])
    return jnp.concatenate([head, rolled[CARRY_ROWS:]], axis=0)


def _causal_conv_rows(cur, prev, w_ref, b):
    k = w_ref.shape[0]
    y = b + w_ref[k - 1:k, :] * cur
    for j in range(k - 1):
        y = y + w_ref[j:j + 1, :] * _shift_rows(cur, prev, k - 1 - j)
    return y


_GELU_C = 0.7978845608028654
_GELU_C3 = _GELU_C * 0.044715


def _gelu(x):
    half = 0.5 * x
    return half + half * jnp.tanh(x * (_GELU_C + _GELU_C3 * (x * x)))


def _sigmoid(x):
    return 0.5 + 0.5 * jnp.tanh(0.5 * x)


def _sqrt_nonneg(v):
    return jnp.where(v > 0.0, v * lax.rsqrt(v), 0.0)


def _log_sigmoid(x):
    return jnp.minimum(x, 0.0) - jnp.log1p(jnp.exp(-jnp.abs(x)))


def _lru_coeffs(xc, wa_ref, ba, wx_ref, bx, lam):
    xb = xc.astype(BF16)
    blk = xc.shape[-1] // N_LRU_BLOCKS
    ra, ri = [], []
    for n in range(N_LRU_BLOCKS):
        xs = xb[:, n * blk:(n + 1) * blk]
        ra.append(jnp.dot(xs, wa_ref[n], preferred_element_type=F32))
        ri.append(jnp.dot(xs, wx_ref[n], preferred_element_type=F32))
    r = _sigmoid(jnp.concatenate(ra, axis=-1) + ba)
    i = _sigmoid(jnp.concatenate(ri, axis=-1) + bx)
    log_a = LRU_C * r * _log_sigmoid(lam)
    a = jnp.exp(log_a)
    one_minus_a2 = (1.0 + a * a) * jnp.tanh(-log_a)
    bt = _sqrt_nonneg(one_minus_a2) * (i * xc)
    return a, bt


def _interleave_rows(x, stage):
    n, d = x.shape
    run = n // CARRY_ROWS
    pitch = stage.shape[1] // CARRY_ROWS
    for c in range(d // LANES):
        for j in range(CARRY_ROWS):
            stage[c, j * pitch:j * pitch + run, :] = x[j * run:(j + 1) * run, c * LANES:(c + 1) * LANES]
    slabs = [jnp.concatenate([stage[c, pl.ds(i, CARRY_ROWS, stride=pitch), :]
                              for c in range(d // LANES)], axis=1) for i in range(run)]
    return jnp.stack(slabs)


def _deinterleave_rows(x, stage):
    run, _, d = x.shape
    pitch = stage.shape[1] // CARRY_ROWS
    for i in range(run):
        for c in range(d // LANES):
            stage[c, pl.ds(i, CARRY_ROWS, stride=pitch), :] = x[i, :, c * LANES:(c + 1) * LANES]
    return jnp.concatenate(
        [jnp.concatenate([stage[c, j * pitch:j * pitch + run, :] for j in range(CARRY_ROWS)], axis=0)
         for c in range(d // LANES)], axis=1)


def _delayed(cur, prev_tail, s):
    run = cur.shape[0]
    sub = lax.broadcasted_iota(jnp.int32, (s,) + cur.shape[1:], 1)
    edge = jnp.where(sub == 0,
                     pltpu.roll(prev_tail[prev_tail.shape[0] - s:], 1, axis=1),
                     pltpu.roll(cur[run - s:], 1, axis=1))
    return jnp.concatenate([edge, cur[:run - s]], axis=0)


def _scan_interleaved(a, b, h_init):
    run = a.shape[0]
    h, acc = b[0], a[0]
    hs, accs = [h], [acc]
    for i in range(1, run):
        h = a[i] * h + b[i]
        acc = a[i] * acc
        hs.append(h)
        accs.append(acc)
    tot_a, tot_b = acc, h
    sub = lax.broadcasted_iota(jnp.int32, tot_a.shape, 0)
    s = 1
    while s < CARRY_ROWS:
        keep = sub >= s
        a_sh = jnp.where(keep, pltpu.roll(tot_a, s, axis=0), 1.0)
        b_sh = jnp.where(keep, pltpu.roll(tot_b, s, axis=0), 0.0)
        tot_b = tot_a * b_sh + tot_b
        tot_a = tot_a * a_sh
        s *= 2
    ends = tot_a * h_init + tot_b
    starts = jnp.where(sub == 0, h_init, pltpu.roll(ends, 1, axis=0))
    out = jnp.stack([hs[i] + accs[i] * starts for i in range(run)])
    return out, ends[CARRY_ROWS - 1:, :]


def _mixer_prompt_kernel(x_ref, g_ref, win_ref, cw_ref, cb_ref, wa_ref, ba_ref,
                         wx_ref, bx_ref, lam_ref, wout_ref,
                         y_ref, hlast_ref, cbuf_ref, stage, utail, hcarry):
    t = pl.program_id(1)
    tq, d = x_ref.shape
    run = tq // CARRY_ROWS
    taps = cw_ref.shape[0]

    @pl.when(t == 0)
    def _():
        utail[...] = jnp.zeros_like(utail)
        hcarry[...] = jnp.zeros_like(hcarry)

    x = _interleave_rows(x_ref[...], stage).reshape(tq, d)
    u = _bdot(_rmsnorm(x, g_ref[...]), win_ref[...])
    gate = _gelu(u[:, :d])
    ur = u[:, d:].reshape(run, CARRY_ROWS, d)
    prev_tail = utail[...]
    xc = cb_ref[...] + cw_ref[taps - 1:taps, :] * ur
    for j in range(taps - 1):
        xc = xc + cw_ref[j:j + 1, :] * _delayed(ur, prev_tail, taps - 1 - j)
    utail[...] = ur[run - (taps - 1):]
    a, bt = _lru_coeffs(xc.reshape(tq, d), wa_ref, ba_ref[...], wx_ref, bx_ref[...], lam_ref[...])
    h, h_end = _scan_interleaved(a.reshape(run, CARRY_ROWS, d), bt.reshape(run, CARRY_ROWS, d),
                                 hcarry[...])
    hcarry[...] = h_end
    y = x + _bdot(h.reshape(tq, d) * gate, wout_ref[...])
    y_ref[...] = _deinterleave_rows(y.reshape(run, CARRY_ROWS, d), stage)

    @pl.when(t == pl.num_programs(1) - 1)
    def _():
        hlast_ref[...] = h_end
        cbuf_ref[...] = ur[run - (taps - 1):, CARRY_ROWS - 1, :]


def _mixer_sample_kernel(x_ref, h0_ref, buf_ref, g_ref, win_ref, cw_ref, cb_ref,
                         wa_ref, ba_ref, wx_ref, bx_ref, lam_ref, wout_ref,
                         y_ref, hnew_ref, ur_ref):
    d = x_ref.shape[-1]
    k = cw_ref.shape[0]
    x = x_ref[...]
    u = _bdot(_rmsnorm(x, g_ref[...]), win_ref[...])
    gate = _gelu(u[:, :d])
    ur = u[:, d:]
    xc = cb_ref[...] + cw_ref[k - 1:k, :] * ur
    for j in range(k - 1):
        xc = xc + cw_ref[j:j + 1, :] * buf_ref[j]
    a, bt = _lru_coeffs(xc, wa_ref, ba_ref[...], wx_ref, bx_ref[...], lam_ref[...])
    h = a * h0_ref[...] + bt
    hnew_ref[...] = h
    ur_ref[...] = ur
    y_ref[...] = x + _bdot(h * gate, wout_ref[...])


def _full_spec(shape):
    zeros = (0,) * len(shape)
    return pl.BlockSpec(shape, lambda *_: zeros)


class _Layer:
    def __init__(self, stacked, index):
        self.stacked, self.index = stacked, index

    @property
    def shape(self):
        return self.stacked.shape[1:]


def _weight_spec(w):
    zeros = (0,) * len(w.shape)
    if isinstance(w, _Layer):
        return pl.BlockSpec((None,) + w.shape, lambda *_: (w.index,) + zeros,
                            pipeline_mode=pl.Buffered(1))
    return pl.BlockSpec(w.shape, lambda *_: zeros, pipeline_mode=pl.Buffered(1))


def _weight_specs(weights):
    return [_weight_spec(w) for w in weights]


def _weight_arrays(weights):
    return [w.stacked if isinstance(w, _Layer) else w for w in weights]


def _mixer_prompt(x, weights):
    b, t, d = x.shape
    tq = TQ_MIXER
    k = weights[2].shape[0]
    row = lambda bi, ti: (bi, ti, 0)
    per_seq = lambda bi, ti: (bi, 0, 0)
    return pl.pallas_call(
        _mixer_prompt_kernel,
        grid=(b, t // tq),
        in_specs=[pl.BlockSpec((None, tq, d), row)] + _weight_specs(weights),
        out_specs=[pl.BlockSpec((None, tq, d), row),
                   pl.BlockSpec((None, 1, d), per_seq),
                   pl.BlockSpec((None, k - 1, d), per_seq)],
        out_shape=[jax.ShapeDtypeStruct((b, t, d), F32),
                   jax.ShapeDtypeStruct((b, 1, d), F32),
                   jax.ShapeDtypeStruct((b, k - 1, d), F32)],
        scratch_shapes=[pltpu.VMEM((d // LANES, tq + CARRY_ROWS * STAGE_SKEW_ROWS, LANES), F32),
                        pltpu.VMEM((k - 1, CARRY_ROWS, d), F32),
                        pltpu.VMEM((1, d), F32)],
        compiler_params=pltpu.CompilerParams(
            dimension_semantics=("arbitrary", "arbitrary"),
            vmem_limit_bytes=VMEM_LIMIT),
        name="mixer_prompt",
    )(x, *weights)


def _mixer_sample(x, h0, buf, weights):
    n, d = x.shape
    k = weights[2].shape[0]
    return pl.pallas_call(
        _mixer_sample_kernel,
        grid=(1,),
        in_specs=[_full_spec((n, d)), _full_spec((n, d)), _full_spec((k - 1, n, d))]
        + _weight_specs(weights),
        out_specs=[_full_spec((n, d))] * 3,
        out_shape=[jax.ShapeDtypeStruct((n, d), F32)] * 3,
        compiler_params=pltpu.CompilerParams(
            dimension_semantics=("arbitrary",), vmem_limit_bytes=VMEM_LIMIT),
        name="mixer_sample",
    )(x, h0, buf, *weights)


def _ffn_chunks(xb, x, wup_ref, cw_ref, cb_ref, wdown_ref, conv_gate, act_ref=None, fc=FC_FFN):
    f = wdown_ref.shape[0]
    acc = x
    for lo in range(0, f, fc):
        hi = lo + fc
        ug = jnp.dot(xb, wup_ref[:, lo:hi], preferred_element_type=F32)
        uv = jnp.dot(xb, wup_ref[:, f + lo:f + hi], preferred_element_type=F32)
        gc = conv_gate(ug, lo, hi)
        act = (_gelu(gc) * uv).astype(BF16)
        if act_ref is None:
            acc = acc + jnp.dot(act, wdown_ref[lo:hi, :], preferred_element_type=F32)
        else:
            act_ref[:, lo:hi] = act
    if act_ref is not None:
        acc = acc + jnp.dot(act_ref[...], wdown_ref[...], preferred_element_type=F32)
    return acc


def _ffn_prompt_kernel(*refs, final_norm, fc, cache_dils, attention):
    refs = list(refs)
    x_ref, g_ref, wup_ref, cw_ref, cb_ref, wdown_ref = refs[:6]
    del refs[:6]
    fn_ref = refs.pop(0) if final_norm else None
    if attention:
        att_refs = refs[:2 * N_GROUPS + 2]
        del refs[:2 * N_GROUPS + 2]
    cache_in = [refs[3 * j:3 * j + 3] for j in range(len(cache_dils))]
    del refs[:3 * len(cache_dils)]
    y_ref, fbuf_ref = refs[:2]
    del refs[:2]
    cache_out = [refs[4 * j:4 * j + 4] for j in range(len(cache_dils))]
    del refs[:4 * len(cache_dils)]
    gcarry, act_ref = refs[:2]
    att_scratch = refs[2:]
    t = pl.program_id(1)
    tm = x_ref.shape[0]
    k = cw_ref.shape[0]

    @pl.when(t == 0)
    def _():
        gcarry[...] = jnp.zeros_like(gcarry)

    step = pl.program_id(0) * pl.num_programs(1) + t
    for dil, (cols_ref, kc_ref, vc_ref), (oc_ref, lc_ref, ko_ref, vo_ref) in zip(
            cache_dils, cache_in, cache_out):
        nh, dh, w = kc_ref.shape
        per_seq = HEADS // nh

        @pl.when(step == 0)
        def _():
            oc_ref[...] = jnp.zeros_like(oc_ref)
            lc_ref[...] = jnp.zeros_like(lc_ref)

        _attend_and_roll(cols_ref, kc_ref, vc_ref, ko_ref, vo_ref, oc_ref, lc_ref,
                         step // per_seq, (step % per_seq) * (nh * dh), dil,
                         1 if w > HEADS_TOGETHER_MAX_W else nh)

    def conv_gate(ug, lo, hi):
        gc = _causal_conv_rows(ug, gcarry[:, lo:hi], cw_ref.at[:, lo:hi], cb_ref[:, lo:hi])
        gcarry[:, lo:hi] = ug[tm - CARRY_ROWS:, :]
        return gc

    x = x_ref[...]
    if attention:
        x = _attention_residual(x, att_refs[:N_GROUPS], att_refs[N_GROUPS:2 * N_GROUPS],
                                att_refs[-2], att_refs[-1], *att_scratch)
    xb = _rmsnorm(x, g_ref[...]).astype(BF16)
    y = _ffn_chunks(xb, x, wup_ref, cw_ref, cb_ref, wdown_ref, conv_gate, act_ref, fc)
    if final_norm:
        y = _rmsnorm(y, fn_ref[...])
    y_ref[...] = y

    @pl.when(t == pl.num_programs(1) - 1)
    def _():
        fbuf_ref[...] = gcarry[CARRY_ROWS - (k - 1):, :]


def _ffn_sample_kernel(*refs, final_norm):
    if final_norm:
        (x_ref, buf_ref, g_ref, wup_ref, cw_ref, cb_ref, wdown_ref, fn_ref,
         y_ref, ug_ref) = refs
    else:
        (x_ref, buf_ref, g_ref, wup_ref, cw_ref, cb_ref, wdown_ref,
         y_ref, ug_ref) = refs
    k = cw_ref.shape[0]

    def conv_gate(ug, lo, hi):
        ug_ref[:, lo:hi] = ug
        gc = cb_ref[:, lo:hi] + cw_ref[k - 1:k, lo:hi] * ug
        for j in range(k - 1):
            gc = gc + cw_ref[j:j + 1, lo:hi] * buf_ref[j, :, lo:hi]
        return gc

    x = x_ref[...]
    xb = _rmsnorm(x, g_ref[...]).astype(BF16)
    y = _ffn_chunks(xb, x, wup_ref, cw_ref, cb_ref, wdown_ref, conv_gate)
    if final_norm:
        y = _rmsnorm(y, fn_ref[...])
    y_ref[...] = y


def _ffn_prompt(x, weights, final_norm, tm, caches=(), attention=None):
    b, t, d = x.shape
    f = weights[4].shape[0]
    k = weights[2].shape[0]
    nt = t // tm
    row = lambda bi, ti: (bi, ti, 0)
    per_seq = lambda bi, ti: (bi, 0, 0)
    in_specs = [pl.BlockSpec((None, tm, d), row)] + _weight_specs(weights)
    out_specs = [pl.BlockSpec((None, tm, d), row), pl.BlockSpec((None, k - 1, f), per_seq)]
    out_shape = [jax.ShapeDtypeStruct((b, t, d), F32), jax.ShapeDtypeStruct((b, k - 1, f), F32)]
    operands = [x, *_weight_arrays(weights)]
    scratch = [pltpu.VMEM((CARRY_ROWS, f), F32), pltpu.VMEM((tm, f), BF16)]
    if attention is not None:
        att_specs, att_operands, att_scratch = _attention_residual_operands(*attention, tm)
        in_specs += att_specs
        operands += att_operands
        scratch += att_scratch
    cache_in_specs, cache_out_specs, cache_out_shape, cache_operands = [], [], [], []
    for cols, kc, vc, _ in caches:
        n, heads, dh, w = kc.shape
        steps_per_seq = (b * nt) // n
        nh = heads // steps_per_seq
        assert steps_per_seq * n == b * nt and nh * steps_per_seq == heads
        blk = pl.BlockSpec((None, nh, dh, w),
                           lambda bi, ti, s=steps_per_seq: ((bi * nt + ti) // s, (bi * nt + ti) % s, 0, 0))
        col = _full_spec(cols.shape[1:])
        cache_in_specs += [_full_spec(cols.shape), blk, blk]
        cache_out_specs += [col, col, blk, blk]
        cache_out_shape += [jax.ShapeDtypeStruct(cols.shape[1:], F32)] * 2
        cache_out_shape += [jax.ShapeDtypeStruct(kc.shape, F32)] * 2
        cache_operands += [cols, kc, vc]
    in_specs += cache_in_specs
    out_specs += cache_out_specs
    out_shape += cache_out_shape
    operands += cache_operands
    return pl.pallas_call(
        functools.partial(_ffn_prompt_kernel, final_norm=final_norm, fc=FC_FFN,
                          cache_dils=tuple(c[3] for c in caches),
                          attention=attention is not None),
        grid=(b, nt),
        in_specs=in_specs,
        out_specs=out_specs,
        out_shape=out_shape,
        scratch_shapes=scratch,
        compiler_params=pltpu.CompilerParams(
            dimension_semantics=("arbitrary", "arbitrary"),
            vmem_limit_bytes=VMEM_LIMIT),
        name="ffn_prompt_final" if final_norm else "ffn_prompt",
    )(*operands)


def _ffn_sample(x, buf, weights, final_norm):
    n, d = x.shape
    f = weights[4].shape[0]
    k = weights[2].shape[0]
    return pl.pallas_call(
        functools.partial(_ffn_sample_kernel, final_norm=final_norm),
        grid=(1,),
        in_specs=[_full_spec((n, d)), _full_spec((k - 1, n, f))]
        + _weight_specs(weights),
        out_specs=[_full_spec((n, d)), _full_spec((n, f))],
        out_shape=[jax.ShapeDtypeStruct((n, d), F32), jax.ShapeDtypeStruct((n, f), F32)],
        compiler_params=pltpu.CompilerParams(
            dimension_semantics=("arbitrary",), vmem_limit_bytes=VMEM_LIMIT),
        name="ffn_sample_final" if final_norm else "ffn_sample",
    )(x, buf, *_weight_arrays(weights))


def _qkv_kernel(x_ref, gq_ref, gkv_ref, wq_ref, wkv_ref, q_ref, kv_ref):
    x = x_ref[...]
    inv = lax.rsqrt(jnp.mean(x * x, axis=-1, keepdims=True) + EPS)
    y = x * inv
    q_ref[...] = _bdot(y * gq_ref[...], wq_ref[...])
    kv_ref[...] = _bdot(y * gkv_ref[...], wkv_ref[...])


def _qkv(x, gq, gkv, wq, wkv, tm):
    m, d = x.shape
    nq, nkv = wq.shape[1], wkv.shape[1]
    row = lambda i: (i, 0)
    return pl.pallas_call(
        _qkv_kernel,
        grid=(m // tm,),
        in_specs=[pl.BlockSpec((tm, d), row)] + _weight_specs((gq, gkv, wq, wkv)),
        out_specs=[pl.BlockSpec((tm, nq), row), pl.BlockSpec((tm, nkv), row)],
        out_shape=[jax.ShapeDtypeStruct((m, nq), F32), jax.ShapeDtypeStruct((m, nkv), F32)],
        compiler_params=pltpu.CompilerParams(
            dimension_semantics=("arbitrary",), vmem_limit_bytes=VMEM_LIMIT),
        name="qkv_proj",
    )(x, gq, gkv, wq, wkv)


LSE_REP = ATT_BLK // HEADS


_NT = (((1,), (1,)), ((), ()))


def _qkv_prompt_kernel(x_ref, gq_ref, gkv_ref, wq_ref, wkv_ref, *out_and_scratch, win):
    out_refs, ybuf = out_and_scratch[:-1], out_and_scratch[-1]
    tile, n_tiles = pl.program_id(1), pl.num_programs(1)
    tm = x_ref.shape[0]
    x = x_ref[...]
    y = x * lax.rsqrt(jnp.mean(x * x, axis=-1, keepdims=True) + EPS)
    xkv_token_order = (y * gkv_ref[...]).astype(BF16)
    chunks = ybuf.shape[0]
    for c in range(chunks):
        ybuf[c] = y[:, c * LANES:(c + 1) * LANES]
    for g in range(N_GROUPS):
        q_ref, k_ref, v_ref, kt_tail_ref, vt_tail_ref = out_refs[5 * g:5 * g + 5]
        part = lambda ref, j: ref[:, j * ATT_W:(j + 1) * ATT_W]

        tail_rows = kt_tail_ref.shape[1]
        tail_tiles = win[g] // tail_rows

        @pl.when(tile >= n_tiles - tail_tiles)
        def _():
            rows = xkv_token_order[tm - tail_rows:, :]
            kt_tail_ref[...] = jnp.dot(rows, part(wkv_ref, g), preferred_element_type=F32).T
            vt_tail_ref[...] = jnp.dot(rows, part(wkv_ref, N_GROUPS + g),
                                       preferred_element_type=F32).T

        dil, per = q_ref.shape[0], q_ref.shape[1]
        yg = y
        if dil > 1:
            yg = jnp.concatenate(
                [jnp.concatenate([ybuf[c, pl.ds(r, per, stride=dil), :] for c in range(chunks)], axis=1)
                 for r in range(dil)], axis=0)
        xq = (yg * gq_ref[...]).astype(BF16)
        xkv = (yg * gkv_ref[...]).astype(BF16)
        q = jnp.dot(xq, part(wq_ref, g), preferred_element_type=F32) * (HEAD_DIM ** -0.5)
        k = jnp.dot(xkv, part(wkv_ref, g), preferred_element_type=F32)
        v = jnp.dot(xkv, part(wkv_ref, N_GROUPS + g), preferred_element_type=F32)
        q_ref[...] = q.astype(BF16).reshape(dil, per, ATT_W)
        k_ref[...] = k.astype(BF16).reshape(dil, per, ATT_W)
        v_ref[...] = v.astype(BF16).reshape(dil, per, ATT_W)


def _qkv_prompt(x, gq, gkv, wq, wkv):
    b, t, d = x.shape
    tm = TM_TOKEN
    nt = t // tm
    specs, shapes, wins = [], [], []
    for win, dil in GROUPS:
        assert tm % (dil * 16) == 0
        win = min(win, t)
        tail_rows = min(win, tm)
        assert win % tail_rows == 0
        first_tail = nt - win // tail_rows
        specs += [pl.BlockSpec((None, dil, tm // dil, ATT_W), lambda bi, ti: (bi, 0, ti, 0))] * 3
        specs += [pl.BlockSpec((None, ATT_W, tail_rows),
                               lambda bi, ti, f=first_tail: (bi, 0, jnp.maximum(ti - f, 0)))] * 2
        shapes += [jax.ShapeDtypeStruct((b, dil, t // dil, ATT_W), BF16)] * 3
        shapes += [jax.ShapeDtypeStruct((b, ATT_W, win), F32)] * 2
        wins.append(win)
    weights = (gq, gkv, wq, wkv)
    outs = pl.pallas_call(
        functools.partial(_qkv_prompt_kernel, win=tuple(wins)),
        grid=(b, nt),
        in_specs=[pl.BlockSpec((None, tm, d), lambda bi, ti: (bi, ti, 0))] + _weight_specs(weights),
        out_specs=specs,
        out_shape=shapes,
        scratch_shapes=[pltpu.VMEM((d // LANES, tm, LANES), F32)],
        compiler_params=pltpu.CompilerParams(
            dimension_semantics=("arbitrary", "arbitrary"), vmem_limit_bytes=VMEM_LIMIT),
        name="qkv_prompt",
    )(x, *weights)
    return [outs[5 * g:5 * g + 5] for g in range(N_GROUPS)]


def _attn_prompt_kernel(q_ref, kp_ref, k_ref, vp_ref, v_ref, o_ref, lse_ref):
    nres, blk = kp_ref.shape[:2]
    nblk = q_ref.shape[1] // blk
    qi = lax.broadcasted_iota(jnp.int32, (blk, 2 * blk), 0)
    kj = lax.broadcasted_iota(jnp.int32, (blk, 2 * blk), 1)
    band = jnp.logical_or(jnp.logical_and(kj < blk, kj >= qi),
                          jnp.logical_and(kj >= blk, kj - blk <= qi))
    band_first = jnp.logical_and(band, jnp.logical_or(kj >= blk, pl.program_id(2) > 0))
    lane = lax.broadcasted_iota(jnp.int32, (blk, blk), 1)
    low_half = lane < HEAD_DIM
    head_of_lane = lane // LSE_REP
    zero = jnp.zeros((), BF16)
    for r, j in [(r, j) for r in range(nres) for j in range(nblk)]:
        rows = slice(j * blk, (j + 1) * blk)
        q = q_ref[r, rows, :]
        if j == 0:
            keep = band_first
            k = jnp.concatenate([kp_ref[r], k_ref[r, :blk, :]], axis=0)
            v = jnp.concatenate([vp_ref[r], v_ref[r, :blk, :]], axis=0)
        else:
            keep = band
            k = k_ref[r, (j - 1) * blk:(j + 1) * blk, :]
            v = v_ref[r, (j - 1) * blk:(j + 1) * blk, :]
        lse_tile = jnp.zeros((blk, blk), F32)
        for pair in range(HEADS // 2):
            cols = slice(pair * blk, (pair + 1) * blk)
            q2, k2, v2 = q[:, cols], k[:, cols], v[:, cols]
            outs = []
            for half in range(2):
                mine = low_half if half == 0 else jnp.logical_not(low_half)
                s = lax.dot_general(jnp.where(mine, q2, zero), k2, _NT, preferred_element_type=F32)
                s = jnp.where(keep, s, -jnp.inf)
                m = jnp.max(s, axis=-1, keepdims=True)
                p = jnp.exp(s - m)
                den = jnp.sum(p, axis=-1, keepdims=True)
                outs.append(jnp.dot(p.astype(BF16), v2, preferred_element_type=F32) / den)
                lse_tile = jnp.where(head_of_lane == 2 * pair + half, m + jnp.log(den), lse_tile)
            o_ref[r, rows, cols] = jnp.where(low_half, outs[0], outs[1])
        lse_ref[r, rows, :] = lse_tile


ATT_NBLK = 8


def _attn_prompt_group(q, k, v, g):
    b, dil, n, _ = q.shape
    nblk = min(ATT_NBLK, n // ATT_BLK)
    nres = min(dil, ATT_NBLK // nblk)
    tq = nblk * ATT_BLK
    assert n % tq == 0 and dil % nres == 0
    prev = lambda i: jnp.maximum(i * nblk - 1, 0)
    rows = lambda size, width: pl.BlockSpec((None, nres, size, width), lambda bi, r, i: (bi, r, i, 0))
    before = pl.BlockSpec((None, nres, ATT_BLK, ATT_W), lambda bi, r, i: (bi, r, prev(i), 0))
    return pl.pallas_call(
        _attn_prompt_kernel,
        grid=(b, dil // nres, n // tq),
        in_specs=[rows(tq, ATT_W), before, rows(tq, ATT_W), before, rows(tq, ATT_W)],
        out_specs=[rows(tq, ATT_W), rows(tq, ATT_BLK)],
        out_shape=[jax.ShapeDtypeStruct((b, dil, n, ATT_W), F32),
                   jax.ShapeDtypeStruct((b, dil, n, ATT_BLK), F32)],
        compiler_params=pltpu.CompilerParams(
            dimension_semantics=("arbitrary", "arbitrary", "arbitrary"),
            vmem_limit_bytes=VMEM_LIMIT),
        name=f"attn_prompt_g{g}",
    )(q, k, k, v, v)


def _attention_residual(x, o_refs, l_refs, sel_ref, wo_ref, obuf, lbuf):
    def token_order(ref, buf):
        dil, per, width = ref.shape
        if dil == 1:
            return ref[0]
        chunks = width // LANES
        for r in range(dil):
            rows = ref[r]
            for c in range(chunks):
                buf[c, pl.ds(r, per, stride=dil), :] = rows[:, c * LANES:(c + 1) * LANES]
        return jnp.concatenate([buf[c] for c in range(chunks)], axis=1)

    lses = [token_order(l_ref, lbuf) for l_ref in l_refs]
    m = jnp.maximum(jnp.maximum(lses[0], lses[1]), lses[2])
    es = [jnp.exp(l - m) for l in lses]
    den = es[0] + es[1] + es[2]
    comb = None
    for e, o_ref in zip(es, o_refs):
        w = e / den
        hi = w.astype(BF16)
        lo = (w - hi.astype(F32)).astype(BF16)
        wide = (jnp.dot(hi, sel_ref[...], preferred_element_type=F32)
                + jnp.dot(lo, sel_ref[...], preferred_element_type=F32))
        term = wide * token_order(o_ref, obuf)
        comb = term if comb is None else comb + term
    return x + _bdot(comb, wo_ref[...])


def _attention_residual_operands(outs, lses, wo, tm):
    head_of_lane = jnp.arange(ATT_BLK) // LSE_REP
    first_copy = jnp.arange(ATT_BLK) % LSE_REP == 0
    sel = ((head_of_lane[:, None] == (jnp.arange(ATT_W) // HEAD_DIM)[None, :])
           & first_copy[:, None]).astype(BF16)
    specs = []
    for width, parts in ((ATT_W, outs), (ATT_BLK, lses)):
        for p in parts:
            dil = p.shape[1]
            specs.append(pl.BlockSpec((None, dil, tm // dil, width), lambda bi, ti: (bi, 0, ti, 0)))
    specs += _weight_specs((sel, wo))
    scratch = [pltpu.VMEM((ATT_W // LANES, tm, LANES), F32),
               pltpu.VMEM((ATT_BLK // LANES, tm, LANES), F32)]
    return specs, [*outs, *lses, sel, wo], scratch


def _exact_transpose(x):
    c = x.shape[1]
    eye = (lax.broadcasted_iota(jnp.int32, (c, c), 0)
           == lax.broadcasted_iota(jnp.int32, (c, c), 1)).astype(F32)
    return lax.dot_general(eye, x, (((1,), (1,)), ((), ())),
                           precision=lax.Precision.HIGHEST, preferred_element_type=F32)


SAMPLE_PAD = 128
SAMPLE_BB = {128: 8, 512: 4}


def _sample_cols_kernel(q_ref, kv_ref, cols_ref):
    for g in range(N_GROUPS):
        part = lambda ref, j: ref[:, j * ATT_W:(j + 1) * ATT_W]
        cols_ref[3 * g] = _exact_transpose(part(q_ref, g)) * (HEAD_DIM ** -0.5)
        cols_ref[3 * g + 1] = _exact_transpose(part(kv_ref, g))
        cols_ref[3 * g + 2] = _exact_transpose(part(kv_ref, N_GROUPS + g))


def _sample_rows_kernel(oc_ref, lc_ref, o_ref, lse_ref):
    for g in range(N_GROUPS):
        o_ref[:, g * ATT_W:(g + 1) * ATT_W] = _exact_transpose(oc_ref[g])
        lse_ref[:, g * ATT_W:(g + 1) * ATT_W] = _exact_transpose(lc_ref[g])


def _sample_cols(q, kv):
    pad = lambda x: jnp.pad(x, ((0, SAMPLE_PAD - x.shape[0]), (0, 0)))
    q, kv = pad(q), pad(kv)
    shape = (3 * N_GROUPS, ATT_W, SAMPLE_PAD)
    return pl.pallas_call(
        _sample_cols_kernel,
        grid=(1,),
        in_specs=[_full_spec(q.shape), _full_spec(kv.shape)],
        out_specs=_full_spec(shape),
        out_shape=jax.ShapeDtypeStruct(shape, F32),
        compiler_params=pltpu.CompilerParams(
            dimension_semantics=("arbitrary",), vmem_limit_bytes=VMEM_LIMIT),
        name="sample_cols",
    )(q, kv)


def _sample_rows(oc, lc):
    shape = (SAMPLE_PAD, N_GROUPS * ATT_W)
    return pl.pallas_call(
        _sample_rows_kernel,
        grid=(1,),
        in_specs=[_full_spec(oc.shape), _full_spec(lc.shape)],
        out_specs=[_full_spec(shape)] * 2,
        out_shape=[jax.ShapeDtypeStruct(shape, F32)] * 2,
        compiler_params=pltpu.CompilerParams(
            dimension_semantics=("arbitrary",), vmem_limit_bytes=VMEM_LIMIT),
        name="sample_rows",
    )(oc, lc)


def _attend_and_roll(cols_ref, k_ref, v_ref, ko_ref, vo_ref, oc_ref, lc_ref, seq, row0, dil, hb):
    nh, dh, w = k_ref.shape
    npad = oc_ref.shape[1]
    pos = lax.broadcasted_iota(jnp.int32, (1, w), 1)
    attended = (pos & (dil - 1)) == 0
    newest = pos == w - 1
    mine = lax.broadcasted_iota(jnp.int32, (1, npad), 1) == seq

    for grp in range(nh // hb):
        first = row0 + grp * hb * dh
        rows = (slice(first, first + hb * dh) if isinstance(first, int)
                else pl.ds(pl.multiple_of(first, dh), hb * dh))
        hs = slice(grp * hb, (grp + 1) * hb)

        def column(j):
            col = jnp.sum(jnp.where(mine, cols_ref[j, rows, :], 0.0), axis=-1, keepdims=True)
            return col.reshape(hb, dh, 1)

        def shifted(tile, new_col):
            flat = pltpu.roll(tile.reshape(hb * dh, w), w - 1, axis=1)
            return jnp.where(newest, new_col, flat.reshape(hb, dh, w))

        qc, kc, vc = column(0), column(1), column(2)
        kt = k_ref[hs]
        vt = v_ref[hs]
        s = jnp.where(attended, jnp.sum(kt * qc, axis=1, keepdims=True), -jnp.inf)
        s_new = jnp.sum(kc * qc, axis=1, keepdims=True)
        m = jnp.maximum(jnp.max(s, axis=-1, keepdims=True), s_new)
        p = jnp.exp(s - m)
        p_new = jnp.exp(s_new - m)
        den = jnp.sum(p, axis=-1, keepdims=True) + p_new
        o = (jnp.sum(vt * p, axis=-1, keepdims=True) + p_new * vc) / den
        lse = jnp.broadcast_to(m + jnp.log(den), (hb, dh, 1))
        oc_ref[rows, :] = jnp.where(mine, o.reshape(hb * dh, 1), oc_ref[rows, :])
        lc_ref[rows, :] = jnp.where(mine, lse.reshape(hb * dh, 1), lc_ref[rows, :])
        ko_ref[hs] = shifted(kt, kc)
        vo_ref[hs] = shifted(vt, vc)


def _attn_sample_kernel(cols_ref, k_ref, v_ref, oc_ref, lc_ref, ko_ref, vo_ref, *, dil):
    step = pl.program_id(0)
    bb, heads = k_ref.shape[:2]

    @pl.when(step == 0)
    def _():
        oc_ref[...] = jnp.zeros_like(oc_ref)
        lc_ref[...] = jnp.zeros_like(lc_ref)

    for i in range(bb):
        _attend_and_roll(cols_ref, k_ref.at[i], v_ref.at[i], ko_ref.at[i], vo_ref.at[i],
                         oc_ref, lc_ref, step * bb + i, 0, dil, heads)


def _to_lanes(cache):
    return jnp.transpose(cache, (0, 2, 3, 1))


def _to_rows(cache):
    return jnp.transpose(cache, (0, 3, 1, 2))


def _attn_sample_group(cols, k_cache, v_cache, dil):
    n, heads, dh, w = k_cache.shape
    assert w % dil == 0 and w // dil == ATT_BLK
    bb = SAMPLE_BB[w]
    cache_spec = pl.BlockSpec((bb, heads, dh, w), lambda i: (i, 0, 0, 0))
    col_spec = _full_spec((ATT_W, SAMPLE_PAD))
    col_shape = jax.ShapeDtypeStruct((ATT_W, SAMPLE_PAD), F32)
    return pl.pallas_call(
        functools.partial(_attn_sample_kernel, dil=dil),
        grid=(n // bb,),
        in_specs=[_full_spec(cols.shape)] + [cache_spec] * 2,
        out_specs=[col_spec] * 2 + [cache_spec] * 2,
        out_shape=[col_shape] * 2 + [jax.ShapeDtypeStruct(k_cache.shape, F32)] * 2,
        compiler_params=pltpu.CompilerParams(
            dimension_semantics=("arbitrary",), vmem_limit_bytes=VMEM_LIMIT),
        name=f"attn_sample_d{dil}",
    )(cols, k_cache, v_cache)


def _merge_kernel(x_ref, o0_ref, o1_ref, o2_ref, l0_ref, l1_ref, l2_ref, wo_ref, y_ref):
    l0, l1, l2 = l0_ref[...], l1_ref[...], l2_ref[...]
    m = jnp.maximum(jnp.maximum(l0, l1), l2)
    e0, e1, e2 = jnp.exp(l0 - m), jnp.exp(l1 - m), jnp.exp(l2 - m)
    den = e0 + e1 + e2
    o = (e0 / den) * o0_ref[...] + (e1 / den) * o1_ref[...] + (e2 / den) * o2_ref[...]
    y_ref[...] = x_ref[...] + _bdot(o, wo_ref[...])


def _merge(x, outs, lses, wo, tm):
    m, d = x.shape
    w = wo.shape[0]
    row = lambda i: (i, 0)
    part = pl.BlockSpec((tm, w), row)
    return pl.pallas_call(
        _merge_kernel,
        grid=(m // tm,),
        in_specs=[pl.BlockSpec((tm, d), row)] + [part] * 6 + [_weight_spec(wo)],
        out_specs=pl.BlockSpec((tm, d), row),
        out_shape=jax.ShapeDtypeStruct((m, d), F32),
        compiler_params=pltpu.CompilerParams(
            dimension_semantics=("arbitrary",), vmem_limit_bytes=VMEM_LIMIT),
        name="attn_merge",
    )(x, *outs, *lses, wo)


def kernel(x_prompt, x_sample, state_lru_h, state_conv_a, state_ffn_conv, cache_k0, cache_v0, cache_k1, cache_v1, cache_k2, cache_v2, norm_mix, a_w_in, a_conv_w, a_conv_b, a_gate_a_w, a_gate_a_b, a_gate_x_w, a_gate_x_b, a_lambda, a_w_out, kv_norm, w_kv, b_w_q, b_w_o, norm_ffn, ffn_w_up, ffn_conv_w, ffn_conv_b, ffn_w_down, final_norm):
    b, t, d = x_prompt.shape
    n = x_sample.shape[0]
    assert x_sample.shape[1] == 1 and norm_mix.shape[0] == 2
    caches = (cache_k0, cache_v0, cache_k1, cache_v1, cache_k2, cache_v2)
    row = lambda v: v.reshape(1, -1)

    mixer_w = (row(norm_mix[0]), a_w_in[0].astype(BF16), a_conv_w[0], row(a_conv_b[0]),
               a_gate_a_w[0].astype(BF16), row(a_gate_a_b[0]),
               a_gate_x_w[0].astype(BF16), row(a_gate_x_b[0]),
               row(a_lambda[0]), a_w_out[0].astype(BF16))
    w_up, w_down = ffn_w_up.astype(BF16), ffn_w_down.astype(BF16)
    ffn_w = [(row(norm_ffn[l]), _Layer(w_up, l), ffn_conv_w[l],
              row(ffn_conv_b[l]), _Layer(w_down, l)) for l in range(2)]
    ffn_w[1] = ffn_w[1] + (row(final_norm),)
    gq, gkv = row(norm_mix[1]), row(kv_norm)
    wq, wkv, wo = b_w_q[0].astype(BF16), w_kv.astype(BF16), b_w_o[0].astype(BF16)

    xs = x_sample.reshape(n, d)
    conv_buf = jnp.swapaxes(state_conv_a[0], 0, 1)
    hs, s_h, s_ur = _mixer_sample(xs, state_lru_h[0], conv_buf, mixer_w)
    s_ca = jnp.concatenate([state_conv_a[0][:, 1:], s_ur[:, None]], axis=1)
    hs, s_ug0 = _ffn_sample(hs, jnp.swapaxes(state_ffn_conv[0], 0, 1), ffn_w[0], final_norm=False)
    qs, kvs = _qkv(hs, gq, gkv, wq, wkv, n)
    cols = _sample_cols(qs, kvs)

    sample_groups = [(cols[3 * g:3 * g + 3], _to_lanes(caches[2 * g]), _to_lanes(caches[2 * g + 1]), dil)
                     for g, (_, dil) in enumerate(GROUPS)]
    h, p_h, p_ca = _mixer_prompt(x_prompt, mixer_w)
    h, p_f0, *rolled_wide = _ffn_prompt(h, ffn_w[0], False, TM_FFN_WITH_CACHE,
                                        caches=sample_groups[-1:])
    qkv = _qkv_prompt(h, gq, gkv, wq, wkv)
    outs, lses, p_kv = [], [], []
    for g in range(N_GROUPS):
        o, lse = _attn_prompt_group(*qkv[g][:3], g)
        outs.append(o)
        lses.append(lse)
        for tail in qkv[g][3:]:
            p_kv.append(_to_rows(tail.reshape(b, HEADS, HEAD_DIM, -1)))
    y_prompt, p_f1 = _ffn_prompt(h, ffn_w[1], True, TM_FFN, attention=(outs, lses, wo))

    rolled = [_attn_sample_group(*grp) for grp in sample_groups[:-1]] + [rolled_wide]
    s_kv = [_to_rows(c) for r in rolled for c in r[2:]]
    os_, lses_s = _sample_rows(jnp.stack([r[0] for r in rolled]), jnp.stack([r[1] for r in rolled]))
    part = lambda a, j: a[:n, j * ATT_W:(j + 1) * ATT_W]
    hs = _merge(hs, [part(os_, g) for g in range(N_GROUPS)],
                [part(lses_s, g) for g in range(N_GROUPS)], wo, n)
    y_sample, s_ug1 = _ffn_sample(hs, jnp.swapaxes(state_ffn_conv[1], 0, 1), ffn_w[1], final_norm=True)
    s_ffn = jnp.stack([
        jnp.concatenate([state_ffn_conv[l][:, 1:], ug[:, None]], axis=1)
        for l, ug in enumerate((s_ug0, s_ug1))])
    return (y_prompt, y_sample.reshape(n, 1, d),
            p_h.reshape(1, b, d), s_h.reshape(1, n, d),
            p_ca.reshape(1, b, -1, d), s_ca.reshape(1, n, -1, d),
            jnp.stack([p_f0, p_f1]), s_ffn,
            p_kv[0], p_kv[1], s_kv[0], s_kv[1],
            p_kv[2], p_kv[3], s_kv[2], s_kv[3],
            p_kv[4], p_kv[5], s_kv[4], s_kv[5])
```

```python
import functools

import jax
import jax.numpy as jnp
from jax import lax
from jax.experimental import pallas as pl
from jax.experimental.pallas import tpu as pltpu

F32 = jnp.float32
BF16 = jnp.bfloat16

EPS = 1e-6
LRU_C = 8.0
N_LRU_BLOCKS = 4
HEAD_DIM = 64
HEADS = 8
GROUPS = ((128, 1), (512, 4), (2048, 16))
N_GROUPS = len(GROUPS)
ATT_W = HEADS * HEAD_DIM
ATT_BLK = 128
LANES = 128
CARRY_ROWS = 8

TQ_MIXER = 512
STAGE_SKEW_ROWS = 8
TM_FFN = 512
TM_FFN_WITH_CACHE = 256
FC_FFN = 512
TM_TOKEN = 512
HEADS_TOGETHER_MAX_W = 512
VMEM_LIMIT = 52 * 1024 * 1024


def _rmsnorm(x, g):
    return x * lax.rsqrt(jnp.mean(x * x, axis=-1, keepdims=True) + EPS) * g


def _bdot(a, w):
    return jnp.dot(a.astype(BF16), w, preferred_element_type=F32)


def _shift_rows(cur, prev, s):
    rolled = pltpu.roll(cur, s, axis=0)
    prev_rolled = pltpu.roll(prev, s, axis=0)
    row = lax.broadcasted_iota(jnp.int32, prev.shape, 0)
    head = jnp.where(row < s, prev_rolled, rolled[:CARRY_ROWS])
    return jnp.concatenate([head, rolled[CARRY_ROWS:]], axis=0)


def _causal_conv_rows(cur, prev, w_ref, b):
    k = w_ref.shape[0]
    y = b + w_ref[k - 1:k, :] * cur
    for j in range(k - 1):
        y = y + w_ref[j:j + 1, :] * _shift_rows(cur, prev, k - 1 - j)
    return y


_GELU_C = 0.7978845608028654
_GELU_C3 = _GELU_C * 0.044715


def _gelu(x):
    half = 0.5 * x
    return half + half * jnp.tanh(x * (_GELU_C + _GELU_C3 * (x * x)))


def _sigmoid(x):
    return 0.5 + 0.5 * jnp.tanh(0.5 * x)


def _sqrt_nonneg(v):
    return jnp.where(v > 0.0, v * lax.rsqrt(v), 0.0)


def _log_sigmoid(x):
    return jnp.minimum(x, 0.0) - jnp.log1p(jnp.exp(-jnp.abs(x)))


def _lru_coeffs(xc, wa_ref, ba, wx_ref, bx, lam):
    xb = xc.astype(BF16)
    blk = xc.shape[-1] // N_LRU_BLOCKS
    ra, ri = [], []
    for n in range(N_LRU_BLOCKS):
        xs = xb[:, n * blk:(n + 1) * blk]
        ra.append(jnp.dot(xs, wa_ref[n], preferred_element_type=F32))
        ri.append(jnp.dot(xs, wx_ref[n], preferred_element_type=F32))
    r = _sigmoid(jnp.concatenate(ra, axis=-1) + ba)
    i = _sigmoid(jnp.concatenate(ri, axis=-1) + bx)
    log_a = LRU_C * r * _log_sigmoid(lam)
    a = jnp.exp(log_a)
    one_minus_a2 = (1.0 + a * a) * jnp.tanh(-log_a)
    bt = _sqrt_nonneg(one_minus_a2) * (i * xc)
    return a, bt


def _interleave_rows(x, stage):
    n, d = x.shape
    run = n // CARRY_ROWS
    pitch = stage.shape[1] // CARRY_ROWS
    for c in range(d // LANES):
        for j in range(CARRY_ROWS):
            stage[c, j * pitch:j * pitch + run, :] = x[j * run:(j + 1) * run, c * LANES:(c + 1) * LANES]
    slabs = [jnp.concatenate([stage[c, pl.ds(i, CARRY_ROWS, stride=pitch), :]
                              for c in range(d // LANES)], axis=1) for i in range(run)]
    return jnp.stack(slabs)


def _deinterleave_rows(x, stage):
    run, _, d = x.shape
    pitch = stage.shape[1] // CARRY_ROWS
    for i in range(run):
        for c in range(d // LANES):
            stage[c, pl.ds(i, CARRY_ROWS, stride=pitch), :] = x[i, :, c * LANES:(c + 1) * LANES]
    return jnp.concatenate(
        [jnp.concatenate([stage[c, j * pitch:j * pitch + run, :] for j in range(CARRY_ROWS)], axis=0)
         for c in range(d // LANES)], axis=1)


def _delayed(cur, prev_tail, s):
    run = cur.shape[0]
    sub = lax.broadcasted_iota(jnp.int32, (s,) + cur.shape[1:], 1)
    edge = jnp.where(sub == 0,
                     pltpu.roll(prev_tail[prev_tail.shape[0] - s:], 1, axis=1),
                     pltpu.roll(cur[run - s:], 1, axis=1))
    return jnp.concatenate([edge, cur[:run - s]], axis=0)


def _scan_interleaved(a, b, h_init):
    run = a.shape[0]
    h, acc = b[0], a[0]
    hs, accs = [h], [acc]
    for i in range(1, run):
        h = a[i] * h + b[i]
        acc = a[i] * acc
        hs.append(h)
        accs.append(acc)
    tot_a, tot_b = acc, h
    sub = lax.broadcasted_iota(jnp.int32, tot_a.shape, 0)
    s = 1
    while s < CARRY_ROWS:
        keep = sub >= s
        a_sh = jnp.where(keep, pltpu.roll(tot_a, s, axis=0), 1.0)
        b_sh = jnp.where(keep, pltpu.roll(tot_b, s, axis=0), 0.0)
        tot_b = tot_a * b_sh + tot_b
        tot_a = tot_a * a_sh
        s *= 2
    ends = tot_a * h_init + tot_b
    starts = jnp.where(sub == 0, h_init, pltpu.roll(ends, 1, axis=0))
    out = jnp.stack([hs[i] + accs[i] * starts for i in range(run)])
    return out, ends[CARRY_ROWS - 1:, :]


def _mixer_prompt_kernel(x_ref, g_ref, win_ref, cw_ref, cb_ref, wa_ref, ba_ref,
                         wx_ref, bx_ref, lam_ref, wout_ref,
                         y_ref, hlast_ref, cbuf_ref, stage, utail, hcarry):
    t = pl.program_id(1)
    tq, d = x_ref.shape
    run = tq // CARRY_ROWS
    taps = cw_ref.shape[0]

    @pl.when(t == 0)
    def _():
        utail[...] = jnp.zeros_like(utail)
        hcarry[...] = jnp.zeros_like(hcarry)

    x = _interleave_rows(x_ref[...], stage).reshape(tq, d)
    u = _bdot(_rmsnorm(x, g_ref[...]), win_ref[...])
    gate = _gelu(u[:, :d])
    ur = u[:, d:].reshape(run, CARRY_ROWS, d)
    prev_tail = utail[...]
    xc = cb_ref[...] + cw_ref[taps - 1:taps, :] * ur
    for j in range(taps - 1):
        xc = xc + cw_ref[j:j + 1, :] * _delayed(ur, prev_tail, taps - 1 - j)
    utail[...] = ur[run - (taps - 1):]
    a, bt = _lru_coeffs(xc.reshape(tq, d), wa_ref, ba_ref[...], wx_ref, bx_ref[...], lam_ref[...])
    h, h_end = _scan_interleaved(a.reshape(run, CARRY_ROWS, d), bt.reshape(run, CARRY_ROWS, d),
                                 hcarry[...])
    hcarry[...] = h_end
    y = x + _bdot(h.reshape(tq, d) * gate, wout_ref[...])
    y_ref[...] = _deinterleave_rows(y.reshape(run, CARRY_ROWS, d), stage)

    @pl.when(t == pl.num_programs(1) - 1)
    def _():
        hlast_ref[...] = h_end
        cbuf_ref[...] = ur[run - (taps - 1):, CARRY_ROWS - 1, :]


def _mixer_sample_kernel(x_ref, h0_ref, buf_ref, g_ref, win_ref, cw_ref, cb_ref,
                         wa_ref, ba_ref, wx_ref, bx_ref, lam_ref, wout_ref,
                         y_ref, hnew_ref, ur_ref):
    d = x_ref.shape[-1]
    k = cw_ref.shape[0]
    x = x_ref[...]
    u = _bdot(_rmsnorm(x, g_ref[...]), win_ref[...])
    gate = _gelu(u[:, :d])
    ur = u[:, d:]
    xc = cb_ref[...] + cw_ref[k - 1:k, :] * ur
    for j in range(k - 1):
        xc = xc + cw_ref[j:j + 1, :] * buf_ref[j]
    a, bt = _lru_coeffs(xc, wa_ref, ba_ref[...], wx_ref, bx_ref[...], lam_ref[...])
    h = a * h0_ref[...] + bt
    hnew_ref[...] = h
    ur_ref[...] = ur
    y_ref[...] = x + _bdot(h * gate, wout_ref[...])


def _full_spec(shape):
    zeros = (0,) * len(shape)
    return pl.BlockSpec(shape, lambda *_: zeros)


class _Layer:
    def __init__(self, stacked, index):
        self.stacked, self.index = stacked, index

    @property
    def shape(self):
        return self.stacked.shape[1:]


def _weight_spec(w):
    zeros = (0,) * len(w.shape)
    if isinstance(w, _Layer):
        return pl.BlockSpec((None,) + w.shape, lambda *_: (w.index,) + zeros,
                            pipeline_mode=pl.Buffered(1))
    return pl.BlockSpec(w.shape, lambda *_: zeros, pipeline_mode=pl.Buffered(1))


def _weight_specs(weights):
    return [_weight_spec(w) for w in weights]


def _weight_arrays(weights):
    return [w.stacked if isinstance(w, _Layer) else w for w in weights]


def _mixer_prompt(x, weights):
    b, t, d = x.shape
    tq = TQ_MIXER
    k = weights[2].shape[0]
    row = lambda bi, ti: (bi, ti, 0)
    per_seq = lambda bi, ti: (bi, 0, 0)
    return pl.pallas_call(
        _mixer_prompt_kernel,
        grid=(b, t // tq),
        in_specs=[pl.BlockSpec((None, tq, d), row)] + _weight_specs(weights),
        out_specs=[pl.BlockSpec((None, tq, d), row),
                   pl.BlockSpec((None, 1, d), per_seq),
                   pl.BlockSpec((None, k - 1, d), per_seq)],
        out_shape=[jax.ShapeDtypeStruct((b, t, d), F32),
                   jax.ShapeDtypeStruct((b, 1, d), F32),
                   jax.ShapeDtypeStruct((b, k - 1, d), F32)],
        scratch_shapes=[pltpu.VMEM((d // LANES, tq + CARRY_ROWS * STAGE_SKEW_ROWS, LANES), F32),
                        pltpu.VMEM((k - 1, CARRY_ROWS, d), F32),
                        pltpu.VMEM((1, d), F32)],
        compiler_params=pltpu.CompilerParams(
            dimension_semantics=("arbitrary", "arbitrary"),
            vmem_limit_bytes=VMEM_LIMIT),
        name="mixer_prompt",
    )(x, *weights)


def _mixer_sample(x, h0, buf, weights):
    n, d = x.shape
    k = weights[2].shape[0]
    return pl.pallas_call(
        _mixer_sample_kernel,
        grid=(1,),
        in_specs=[_full_spec((n, d)), _full_spec((n, d)), _full_spec((k - 1, n, d))]
        + _weight_specs(weights),
        out_specs=[_full_spec((n, d))] * 3,
        out_shape=[jax.ShapeDtypeStruct((n, d), F32)] * 3,
        compiler_params=pltpu.CompilerParams(
            dimension_semantics=("arbitrary",), vmem_limit_bytes=VMEM_LIMIT),
        name="mixer_sample",
    )(x, h0, buf, *weights)


def _ffn_chunks(xb, x, wup_ref, cw_ref, cb_ref, wdown_ref, conv_gate, act_ref=None, fc=FC_FFN):
    f = wdown_ref.shape[0]
    acc = x
    for lo in range(0, f, fc):
        hi = lo + fc
        ug = jnp.dot(xb, wup_ref[:, lo:hi], preferred_element_type=F32)
        uv = jnp.dot(xb, wup_ref[:, f + lo:f + hi], preferred_element_type=F32)
        gc = conv_gate(ug, lo, hi)
        act = (_gelu(gc) * uv).astype(BF16)
        if act_ref is None:
            acc = acc + jnp.dot(act, wdown_ref[lo:hi, :], preferred_element_type=F32)
        else:
            act_ref[:, lo:hi] = act
    if act_ref is not None:
        acc = acc + jnp.dot(act_ref[...], wdown_ref[...], preferred_element_type=F32)
    return acc


def _ffn_prompt_kernel(*refs, final_norm, fc, cache_dils, attention):
    refs = list(refs)
    x_ref, g_ref, wup_ref, cw_ref, cb_ref, wdown_ref = refs[:6]
    del refs[:6]
    fn_ref = refs.pop(0) if final_norm else None
    if attention:
        att_refs = refs[:2 * N_GROUPS + 2]
        del refs[:2 * N_GROUPS + 2]
    cache_in = [refs[3 * j:3 * j + 3] for j in range(len(cache_dils))]
    del refs[:3 * len(cache_dils)]
    y_ref, fbuf_ref = refs[:2]
    del refs[:2]
    cache_out = [refs[4 * j:4 * j + 4] for j in range(len(cache_dils))]
    del refs[:4 * len(cache_dils)]
    gcarry, act_ref = refs[:2]
    att_scratch = refs[2:]
    t = pl.program_id(1)
    tm = x_ref.shape[0]
    k = cw_ref.shape[0]

    @pl.when(t == 0)
    def _():
        gcarry[...] = jnp.zeros_like(gcarry)

    step = pl.program_id(0) * pl.num_programs(1) + t
    for dil, (cols_ref, kc_ref, vc_ref), (oc_ref, lc_ref, ko_ref, vo_ref) in zip(
            cache_dils, cache_in, cache_out):
        nh, dh, w = kc_ref.shape
        per_seq = HEADS // nh

        @pl.when(step == 0)
        def _():
            oc_ref[...] = jnp.zeros_like(oc_ref)
            lc_ref[...] = jnp.zeros_like(lc_ref)

        _attend_and_roll(cols_ref, kc_ref, vc_ref, ko_ref, vo_ref, oc_ref, lc_ref,
                         step // per_seq, (step % per_seq) * (nh * dh), dil,
                         1 if w > HEADS_TOGETHER_MAX_W else nh)

    def conv_gate(ug, lo, hi):
        gc = _causal_conv_rows(ug, gcarry[:, lo:hi], cw_ref.at[:, lo:hi], cb_ref[:, lo:hi])
        gcarry[:, lo:hi] = ug[tm - CARRY_ROWS:, :]
        return gc

    x = x_ref[...]
    if attention:
        x = _attention_residual(x, att_refs[:N_GROUPS], att_refs[N_GROUPS:2 * N_GROUPS],
                                att_refs[-2], att_refs[-1], *att_scratch)
    xb = _rmsnorm(x, g_ref[...]).astype(BF16)
    y = _ffn_chunks(xb, x, wup_ref, cw_ref, cb_ref, wdown_ref, conv_gate, act_ref, fc)
    if final_norm:
        y = _rmsnorm(y, fn_ref[...])
    y_ref[...] = y

    @pl.when(t == pl.num_programs(1) - 1)
    def _():
        fbuf_ref[...] = gcarry[CARRY_ROWS - (k - 1):, :]


def _ffn_sample_kernel(*refs, final_norm):
    if final_norm:
        (x_ref, buf_ref, g_ref, wup_ref, cw_ref, cb_ref, wdown_ref, fn_ref,
         y_ref, ug_ref) = refs
    else:
        (x_ref, buf_ref, g_ref, wup_ref, cw_ref, cb_ref, wdown_ref,
         y_ref, ug_ref) = refs
    k = cw_ref.shape[0]

    def conv_gate(ug, lo, hi):
        ug_ref[:, lo:hi] = ug
        gc = cb_ref[:, lo:hi] + cw_ref[k - 1:k, lo:hi] * ug
        for j in range(k - 1):
            gc = gc + cw_ref[j:j + 1, lo:hi] * buf_ref[j, :, lo:hi]
        return gc

    x = x_ref[...]
    xb = _rmsnorm(x, g_ref[...]).astype(BF16)
    y = _ffn_chunks(xb, x, wup_ref, cw_ref, cb_ref, wdown_ref, conv_gate)
    if final_norm:
        y = _rmsnorm(y, fn_ref[...])
    y_ref[...] = y


def _ffn_prompt(x, weights, final_norm, tm, caches=(), attention=None):
    b, t, d = x.shape
    f = weights[4].shape[0]
    k = weights[2].shape[0]
    nt = t // tm
    row = lambda bi, ti: (bi, ti, 0)
    per_seq = lambda bi, ti: (bi, 0, 0)
    in_specs = [pl.BlockSpec((None, tm, d), row)] + _weight_specs(weights)
    out_specs = [pl.BlockSpec((None, tm, d), row), pl.BlockSpec((None, k - 1, f), per_seq)]
    out_shape = [jax.ShapeDtypeStruct((b, t, d), F32), jax.ShapeDtypeStruct((b, k - 1, f), F32)]
    operands = [x, *_weight_arrays(weights)]
    scratch = [pltpu.VMEM((CARRY_ROWS, f), F32), pltpu.VMEM((tm, f), BF16)]
    if attention is not None:
        att_specs, att_operands, att_scratch = _attention_residual_operands(*attention, tm)
        in_specs += att_specs
        operands += att_operands
        scratch += att_scratch
    cache_in_specs, cache_out_specs, cache_out_shape, cache_operands = [], [], [], []
    for cols, kc, vc, _ in caches:
        n, heads, dh, w = kc.shape
        steps_per_seq = (b * nt) // n
        nh = heads // steps_per_seq
        assert steps_per_seq * n == b * nt and nh * steps_per_seq == heads
        blk = pl.BlockSpec((None, nh, dh, w),
                           lambda bi, ti, s=steps_per_seq: ((bi * nt + ti) // s, (bi * nt + ti) % s, 0, 0))
        col = _full_spec(cols.shape[1:])
        cache_in_specs += [_full_spec(cols.shape), blk, blk]
        cache_out_specs += [col, col, blk, blk]
        cache_out_shape += [jax.ShapeDtypeStruct(cols.shape[1:], F32)] * 2
        cache_out_shape += [jax.ShapeDtypeStruct(kc.shape, F32)] * 2
        cache_operands += [cols, kc, vc]
    in_specs += cache_in_specs
    out_specs += cache_out_specs
    out_shape += cache_out_shape
    operands += cache_operands
    return pl.pallas_call(
        functools.partial(_ffn_prompt_kernel, final_norm=final_norm, fc=FC_FFN,
                          cache_dils=tuple(c[3] for c in caches),
                          attention=attention is not None),
        grid=(b, nt),
        in_specs=in_specs,
        out_specs=out_specs,
        out_shape=out_shape,
        scratch_shapes=scratch,
        compiler_params=pltpu.CompilerParams(
            dimension_semantics=("arbitrary", "arbitrary"),
            vmem_limit_bytes=VMEM_LIMIT),
        name="ffn_prompt_final" if final_norm else "ffn_prompt",
    )(*operands)


def _ffn_sample(x, buf, weights, final_norm):
    n, d = x.shape
    f = weights[4].shape[0]
    k = weights[2].shape[0]
    return pl.pallas_call(
        functools.partial(_ffn_sample_kernel, final_norm=final_norm),
        grid=(1,),
        in_specs=[_full_spec((n, d)), _full_spec((k - 1, n, f))]
        + _weight_specs(weights),
        out_specs=[_full_spec((n, d)), _full_spec((n, f))],
        out_shape=[jax.ShapeDtypeStruct((n, d), F32), jax.ShapeDtypeStruct((n, f), F32)],
        compiler_params=pltpu.CompilerParams(
            dimension_semantics=("arbitrary",), vmem_limit_bytes=VMEM_LIMIT),
        name="ffn_sample_final" if final_norm else "ffn_sample",
    )(x, buf, *_weight_arrays(weights))


def _qkv_kernel(x_ref, gq_ref, gkv_ref, wq_ref, wkv_ref, q_ref, kv_ref):
    x = x_ref[...]
    inv = lax.rsqrt(jnp.mean(x * x, axis=-1, keepdims=True) + EPS)
    y = x * inv
    q_ref[...] = _bdot(y * gq_ref[...], wq_ref[...])
    kv_ref[...] = _bdot(y * gkv_ref[...], wkv_ref[...])


def _qkv(x, gq, gkv, wq, wkv, tm):
    m, d = x.shape
    nq, nkv = wq.shape[1], wkv.shape[1]
    row = lambda i: (i, 0)
    return pl.pallas_call(
        _qkv_kernel,
        grid=(m // tm,),
        in_specs=[pl.BlockSpec((tm, d), row)] + _weight_specs((gq, gkv, wq, wkv)),
        out_specs=[pl.BlockSpec((tm, nq), row), pl.BlockSpec((tm, nkv), row)],
        out_shape=[jax.ShapeDtypeStruct((m, nq), F32), jax.ShapeDtypeStruct((m, nkv), F32)],
        compiler_params=pltpu.CompilerParams(
            dimension_semantics=("arbitrary",), vmem_limit_bytes=VMEM_LIMIT),
        name="qkv_proj",
    )(x, gq, gkv, wq, wkv)


LSE_REP = ATT_BLK // HEADS


_NT = (((1,), (1,)), ((), ()))


def _qkv_prompt_kernel(x_ref, gq_ref, gkv_ref, wq_ref, wkv_ref, wkvt_ref, *out_and_scratch, win):
    out_refs, ybuf = out_and_scratch[:-1], out_and_scratch[-1]
    tile, n_tiles = pl.program_id(1), pl.num_programs(1)
    tm = x_ref.shape[0]
    x = x_ref[...]
    y = x * lax.rsqrt(jnp.mean(x * x, axis=-1, keepdims=True) + EPS)
    xkv_token_order = (y * gkv_ref[...]).astype(BF16)
    chunks = ybuf.shape[0]
    for c in range(chunks):
        ybuf[c] = y[:, c * LANES:(c + 1) * LANES]
    for g in range(N_GROUPS):
        q_ref, k_ref, v_ref, kt_tail_ref, vt_tail_ref = out_refs[5 * g:5 * g + 5]
        part = lambda ref, j: ref[:, j * ATT_W:(j + 1) * ATT_W]

        tail_rows = kt_tail_ref.shape[1]
        tail_tiles = win[g] // tail_rows

        @pl.when(tile >= n_tiles - tail_tiles)
        def _():
            rows = xkv_token_order[tm - tail_rows:, :]
            w_t = lambda j: wkvt_ref[j * ATT_W:(j + 1) * ATT_W, :]
            kt_tail_ref[...] = lax.dot_general(w_t(g), rows, _NT, preferred_element_type=F32)
            vt_tail_ref[...] = lax.dot_general(w_t(N_GROUPS + g), rows, _NT,
                                               preferred_element_type=F32)

        dil, per = q_ref.shape[0], q_ref.shape[1]
        yg = y
        if dil > 1:
            yg = jnp.concatenate(
                [jnp.concatenate([ybuf[c, pl.ds(r, per, stride=dil), :] for c in range(chunks)], axis=1)
                 for r in range(dil)], axis=0)
        xq = (yg * gq_ref[...]).astype(BF16)
        xkv = (yg * gkv_ref[...]).astype(BF16)
        q = jnp.dot(xq, part(wq_ref, g), preferred_element_type=F32) * (HEAD_DIM ** -0.5)
        k = jnp.dot(xkv, part(wkv_ref, g), preferred_element_type=F32)
        v = jnp.dot(xkv, part(wkv_ref, N_GROUPS + g), preferred_element_type=F32)
        q_ref[...] = q.astype(BF16).reshape(dil, per, ATT_W)
        k_ref[...] = k.astype(BF16).reshape(dil, per, ATT_W)
        v_ref[...] = v.astype(BF16).reshape(dil, per, ATT_W)


def _qkv_prompt(x, gq, gkv, wq, wkv):
    b, t, d = x.shape
    tm = TM_TOKEN
    nt = t // tm
    specs, shapes, wins = [], [], []
    for win, dil in GROUPS:
        assert tm % (dil * 16) == 0
        win = min(win, t)
        tail_rows = min(win, tm)
        assert win % tail_rows == 0
        first_tail = nt - win // tail_rows
        specs += [pl.BlockSpec((None, dil, tm // dil, ATT_W), lambda bi, ti: (bi, 0, ti, 0))] * 3
        specs += [pl.BlockSpec((None, ATT_W, tail_rows),
                               lambda bi, ti, f=first_tail: (bi, 0, jnp.maximum(ti - f, 0)))] * 2
        shapes += [jax.ShapeDtypeStruct((b, dil, t // dil, ATT_W), BF16)] * 3
        shapes += [jax.ShapeDtypeStruct((b, ATT_W, win), F32)] * 2
        wins.append(win)
    weights = (gq, gkv, wq, wkv, wkv.T)
    outs = pl.pallas_call(
        functools.partial(_qkv_prompt_kernel, win=tuple(wins)),
        grid=(b, nt),
        in_specs=[pl.BlockSpec((None, tm, d), lambda bi, ti: (bi, ti, 0))] + _weight_specs(weights),
        out_specs=specs,
        out_shape=shapes,
        scratch_shapes=[pltpu.VMEM((d // LANES, tm, LANES), F32)],
        compiler_params=pltpu.CompilerParams(
            dimension_semantics=("arbitrary", "arbitrary"), vmem_limit_bytes=VMEM_LIMIT),
        name="qkv_prompt",
    )(x, *weights)
    return [outs[5 * g:5 * g + 5] for g in range(N_GROUPS)]


def _attn_prompt_kernel(q_ref, kp_ref, k_ref, vp_ref, v_ref, o_ref, lse_ref):
    nres, blk = kp_ref.shape[:2]
    nblk = q_ref.shape[1] // blk
    qi = lax.broadcasted_iota(jnp.int32, (blk, 2 * blk), 0)
    kj = lax.broadcasted_iota(jnp.int32, (blk, 2 * blk), 1)
    band = jnp.logical_or(jnp.logical_and(kj < blk, kj >= qi),
                          jnp.logical_and(kj >= blk, kj - blk <= qi))
    band_first = jnp.logical_and(band, jnp.logical_or(kj >= blk, pl.program_id(2) > 0))
    lane = lax.broadcasted_iota(jnp.int32, (blk, blk), 1)
    low_half = lane < HEAD_DIM
    head_of_lane = lane // LSE_REP
    zero = jnp.zeros((), BF16)
    for r, j in [(r, j) for r in range(nres) for j in range(nblk)]:
        rows = slice(j * blk, (j + 1) * blk)
        q = q_ref[r, rows, :]
        if j == 0:
            keep = band_first
            k = jnp.concatenate([kp_ref[r], k_ref[r, :blk, :]], axis=0)
            v = jnp.concatenate([vp_ref[r], v_ref[r, :blk, :]], axis=0)
        else:
            keep = band
            k = k_ref[r, (j - 1) * blk:(j + 1) * blk, :]
            v = v_ref[r, (j - 1) * blk:(j + 1) * blk, :]
        lse_tile = jnp.zeros((blk, blk), F32)
        for pair in range(HEADS // 2):
            cols = slice(pair * blk, (pair + 1) * blk)
            q2, k2, v2 = q[:, cols], k[:, cols], v[:, cols]
            outs = []
            for half in range(2):
                mine = low_half if half == 0 else jnp.logical_not(low_half)
                s = lax.dot_general(jnp.where(mine, q2, zero), k2, _NT, preferred_element_type=F32)
                s = jnp.where(keep, s, -jnp.inf)
                m = jnp.max(s, axis=-1, keepdims=True)
                p = jnp.exp(s - m)
                den = jnp.sum(p, axis=-1, keepdims=True)
                outs.append(jnp.dot(p.astype(BF16), v2, preferred_element_type=F32) / den)
                lse_tile = jnp.where(head_of_lane == 2 * pair + half, m + jnp.log(den), lse_tile)
            o_ref[r, rows, cols] = jnp.where(low_half, outs[0], outs[1])
        lse_ref[r, rows, :] = lse_tile


ATT_NBLK = 8


def _attn_prompt_group(q, k, v, g):
    b, dil, n, _ = q.shape
    nblk = min(ATT_NBLK, n // ATT_BLK)
    nres = min(dil, ATT_NBLK // nblk)
    tq = nblk * ATT_BLK
    assert n % tq == 0 and dil % nres == 0
    prev = lambda i: jnp.maximum(i * nblk - 1, 0)
    rows = lambda size, width: pl.BlockSpec((None, nres, size, width), lambda bi, r, i: (bi, r, i, 0))
    before = pl.BlockSpec((None, nres, ATT_BLK, ATT_W), lambda bi, r, i: (bi, r, prev(i), 0))
    return pl.pallas_call(
        _attn_prompt_kernel,
        grid=(b, dil // nres, n // tq),
        in_specs=[rows(tq, ATT_W), before, rows(tq, ATT_W), before, rows(tq, ATT_W)],
        out_specs=[rows(tq, ATT_W), rows(tq, ATT_BLK)],
        out_shape=[jax.ShapeDtypeStruct((b, dil, n, ATT_W), F32),
                   jax.ShapeDtypeStruct((b, dil, n, ATT_BLK), F32)],
        compiler_params=pltpu.CompilerParams(
            dimension_semantics=("arbitrary", "arbitrary", "arbitrary"),
            vmem_limit_bytes=VMEM_LIMIT),
        name=f"attn_prompt_g{g}",
    )(q, k, k, v, v)


def _attention_residual(x, o_refs, l_refs, sel_ref, wo_ref, obuf, lbuf):
    def token_order(ref, buf):
        dil, per, width = ref.shape
        if dil == 1:
            return ref[0]
        chunks = width // LANES
        for r in range(dil):
            rows = ref[r]
            for c in range(chunks):
                buf[c, pl.ds(r, per, stride=dil), :] = rows[:, c * LANES:(c + 1) * LANES]
        return jnp.concatenate([buf[c] for c in range(chunks)], axis=1)

    lses = [token_order(l_ref, lbuf) for l_ref in l_refs]
    m = jnp.maximum(jnp.maximum(lses[0], lses[1]), lses[2])
    es = [jnp.exp(l - m) for l in lses]
    den = es[0] + es[1] + es[2]
    comb = None
    for e, o_ref in zip(es, o_refs):
        w = e / den
        hi = w.astype(BF16)
        lo = (w - hi.astype(F32)).astype(BF16)
        wide = (jnp.dot(hi, sel_ref[...], preferred_element_type=F32)
                + jnp.dot(lo, sel_ref[...], preferred_element_type=F32))
        term = wide * token_order(o_ref, obuf)
        comb = term if comb is None else comb + term
    return x + _bdot(comb, wo_ref[...])


def _attention_residual_operands(outs, lses, wo, tm):
    head_of_lane = jnp.arange(ATT_BLK) // LSE_REP
    first_copy = jnp.arange(ATT_BLK) % LSE_REP == 0
    sel = ((head_of_lane[:, None] == (jnp.arange(ATT_W) // HEAD_DIM)[None, :])
           & first_copy[:, None]).astype(BF16)
    specs = []
    for width, parts in ((ATT_W, outs), (ATT_BLK, lses)):
        for p in parts:
            dil = p.shape[1]
            specs.append(pl.BlockSpec((None, dil, tm // dil, width), lambda bi, ti: (bi, 0, ti, 0)))
    specs += _weight_specs((sel, wo))
    scratch = [pltpu.VMEM((ATT_W // LANES, tm, LANES), F32),
               pltpu.VMEM((ATT_BLK // LANES, tm, LANES), F32)]
    return specs, [*outs, *lses, sel, wo], scratch


SAMPLE_PAD = 128
SAMPLE_BB = {128: 8, 512: 4}


def _sample_cols_kernel(q_ref, kv_ref, cols_ref):
    for g in range(N_GROUPS):
        part = lambda ref, j: ref[:, j * ATT_W:(j + 1) * ATT_W]
        cols_ref[3 * g] = part(q_ref, g).T * (HEAD_DIM ** -0.5)
        cols_ref[3 * g + 1] = part(kv_ref, g).T
        cols_ref[3 * g + 2] = part(kv_ref, N_GROUPS + g).T


def _sample_rows_kernel(oc_ref, lc_ref, o_ref, lse_ref):
    for g in range(N_GROUPS):
        o_ref[:, g * ATT_W:(g + 1) * ATT_W] = oc_ref[g].T
        lse_ref[:, g * ATT_W:(g + 1) * ATT_W] = lc_ref[g].T


def _sample_cols(q, kv):
    pad = lambda x: jnp.pad(x, ((0, SAMPLE_PAD - x.shape[0]), (0, 0)))
    q, kv = pad(q), pad(kv)
    shape = (3 * N_GROUPS, ATT_W, SAMPLE_PAD)
    return pl.pallas_call(
        _sample_cols_kernel,
        grid=(1,),
        in_specs=[_full_spec(q.shape), _full_spec(kv.shape)],
        out_specs=_full_spec(shape),
        out_shape=jax.ShapeDtypeStruct(shape, F32),
        compiler_params=pltpu.CompilerParams(
            dimension_semantics=("arbitrary",), vmem_limit_bytes=VMEM_LIMIT),
        name="sample_cols",
    )(q, kv)


def _sample_rows(oc, lc):
    shape = (SAMPLE_PAD, N_GROUPS * ATT_W)
    return pl.pallas_call(
        _sample_rows_kernel,
        grid=(1,),
        in_specs=[_full_spec(oc.shape), _full_spec(lc.shape)],
        out_specs=[_full_spec(shape)] * 2,
        out_shape=[jax.ShapeDtypeStruct(shape, F32)] * 2,
        compiler_params=pltpu.CompilerParams(
            dimension_semantics=("arbitrary",), vmem_limit_bytes=VMEM_LIMIT),
        name="sample_rows",
    )(oc, lc)


def _attend_and_roll(cols_ref, k_ref, v_ref, ko_ref, vo_ref, oc_ref, lc_ref, seq, row0, dil, hb):
    nh, dh, w = k_ref.shape
    npad = oc_ref.shape[1]
    pos = lax.broadcasted_iota(jnp.int32, (1, w), 1)
    attended = (pos & (dil - 1)) == 0
    newest = pos == w - 1
    mine = lax.broadcasted_iota(jnp.int32, (1, npad), 1) == seq

    for grp in range(nh // hb):
        first = row0 + grp * hb * dh
        rows = (slice(first, first + hb * dh) if isinstance(first, int)
                else pl.ds(pl.multiple_of(first, dh), hb * dh))
        hs = slice(grp * hb, (grp + 1) * hb)

        def column(j):
            col = jnp.sum(jnp.where(mine, cols_ref[j, rows, :], 0.0), axis=-1, keepdims=True)
            return col.reshape(hb, dh, 1)

        def shifted(tile, new_col):
            flat = pltpu.roll(tile.reshape(hb * dh, w), w - 1, axis=1)
            return jnp.where(newest, new_col, flat.reshape(hb, dh, w))

        qc, kc, vc = column(0), column(1), column(2)
        kt = k_ref[hs]
        vt = v_ref[hs]
        s = jnp.where(attended, jnp.sum(kt * qc, axis=1, keepdims=True), -jnp.inf)
        s_new = jnp.sum(kc * qc, axis=1, keepdims=True)
        m = jnp.maximum(jnp.max(s, axis=-1, keepdims=True), s_new)
        p = jnp.exp(s - m)
        p_new = jnp.exp(s_new - m)
        den = jnp.sum(p, axis=-1, keepdims=True) + p_new
        o = (jnp.sum(vt * p, axis=-1, keepdims=True) + p_new * vc) / den
        lse = jnp.broadcast_to(m + jnp.log(den), (hb, dh, 1))
        oc_ref[rows, :] = jnp.where(mine, o.reshape(hb * dh, 1), oc_ref[rows, :])
        lc_ref[rows, :] = jnp.where(mine, lse.reshape(hb * dh, 1), lc_ref[rows, :])
        ko_ref[hs] = shifted(kt, kc)
        vo_ref[hs] = shifted(vt, vc)


def _attn_sample_kernel(cols_ref, k_ref, v_ref, oc_ref, lc_ref, ko_ref, vo_ref, *, dil):
    step = pl.program_id(0)
    bb, heads = k_ref.shape[:2]

    @pl.when(step == 0)
    def _():
        oc_ref[...] = jnp.zeros_like(oc_ref)
        lc_ref[...] = jnp.zeros_like(lc_ref)

    for i in range(bb):
        _attend_and_roll(cols_ref, k_ref.at[i], v_ref.at[i], ko_ref.at[i], vo_ref.at[i],
                         oc_ref, lc_ref, step * bb + i, 0, dil, heads)


def _to_lanes(cache):
    return jnp.transpose(cache, (0, 2, 3, 1))


def _to_rows(cache):
    return jnp.transpose(cache, (0, 3, 1, 2))


def _attn_sample_group(cols, k_cache, v_cache, dil):
    n, heads, dh, w = k_cache.shape
    assert w % dil == 0 and w // dil == ATT_BLK
    bb = SAMPLE_BB[w]
    cache_spec = pl.BlockSpec((bb, heads, dh, w), lambda i: (i, 0, 0, 0))
    col_spec = _full_spec((ATT_W, SAMPLE_PAD))
    col_shape = jax.ShapeDtypeStruct((ATT_W, SAMPLE_PAD), F32)
    return pl.pallas_call(
        functools.partial(_attn_sample_kernel, dil=dil),
        grid=(n // bb,),
        in_specs=[_full_spec(cols.shape)] + [cache_spec] * 2,
        out_specs=[col_spec] * 2 + [cache_spec] * 2,
        out_shape=[col_shape] * 2 + [jax.ShapeDtypeStruct(k_cache.shape, F32)] * 2,
        compiler_params=pltpu.CompilerParams(
            dimension_semantics=("arbitrary",), vmem_limit_bytes=VMEM_LIMIT),
        name=f"attn_sample_d{dil}",
    )(cols, k_cache, v_cache)


def _merge_kernel(x_ref, o0_ref, o1_ref, o2_ref, l0_ref, l1_ref, l2_ref, wo_ref, y_ref):
    l0, l1, l2 = l0_ref[...], l1_ref[...], l2_ref[...]
    m = jnp.maximum(jnp.maximum(l0, l1), l2)
    e0, e1, e2 = jnp.exp(l0 - m), jnp.exp(l1 - m), jnp.exp(l2 - m)
    den = e0 + e1 + e2
    o = (e0 / den) * o0_ref[...] + (e1 / den) * o1_ref[...] + (e2 / den) * o2_ref[...]
    y_ref[...] = x_ref[...] + _bdot(o, wo_ref[...])


def _merge(x, outs, lses, wo, tm):
    m, d = x.shape
    w = wo.shape[0]
    row = lambda i: (i, 0)
    part = pl.BlockSpec((tm, w), row)
    return pl.pallas_call(
        _merge_kernel,
        grid=(m // tm,),
        in_specs=[pl.BlockSpec((tm, d), row)] + [part] * 6 + [_weight_spec(wo)],
        out_specs=pl.BlockSpec((tm, d), row),
        out_shape=jax.ShapeDtypeStruct((m, d), F32),
        compiler_params=pltpu.CompilerParams(
            dimension_semantics=("arbitrary",), vmem_limit_bytes=VMEM_LIMIT),
        name="attn_merge",
    )(x, *outs, *lses, wo)


def kernel(x_prompt, x_sample, state_lru_h, state_conv_a, state_ffn_conv, cache_k0, cache_v0, cache_k1, cache_v1, cache_k2, cache_v2, norm_mix, a_w_in, a_conv_w, a_conv_b, a_gate_a_w, a_gate_a_b, a_gate_x_w, a_gate_x_b, a_lambda, a_w_out, kv_norm, w_kv, b_w_q, b_w_o, norm_ffn, ffn_w_up, ffn_conv_w, ffn_conv_b, ffn_w_down, final_norm):
    b, t, d = x_prompt.shape
    n = x_sample.shape[0]
    assert x_sample.shape[1] == 1 and norm_mix.shape[0] == 2
    caches = (cache_k0, cache_v0, cache_k1, cache_v1, cache_k2, cache_v2)
    row = lambda v: v.reshape(1, -1)

    mixer_w = (row(norm_mix[0]), a_w_in[0].astype(BF16), a_conv_w[0], row(a_conv_b[0]),
               a_gate_a_w[0].astype(BF16), row(a_gate_a_b[0]),
               a_gate_x_w[0].astype(BF16), row(a_gate_x_b[0]),
               row(a_lambda[0]), a_w_out[0].astype(BF16))
    w_up, w_down = ffn_w_up.astype(BF16), ffn_w_down.astype(BF16)
    ffn_w = [(row(norm_ffn[l]), _Layer(w_up, l), ffn_conv_w[l],
              row(ffn_conv_b[l]), _Layer(w_down, l)) for l in range(2)]
    ffn_w[1] = ffn_w[1] + (row(final_norm),)
    gq, gkv = row(norm_mix[1]), row(kv_norm)
    wq, wkv, wo = b_w_q[0].astype(BF16), w_kv.astype(BF16), b_w_o[0].astype(BF16)

    xs = x_sample.reshape(n, d)
    conv_buf = jnp.swapaxes(state_conv_a[0], 0, 1)
    hs, s_h, s_ur = _mixer_sample(xs, state_lru_h[0], conv_buf, mixer_w)
    s_ca = jnp.concatenate([state_conv_a[0][:, 1:], s_ur[:, None]], axis=1)
    hs, s_ug0 = _ffn_sample(hs, jnp.swapaxes(state_ffn_conv[0], 0, 1), ffn_w[0], final_norm=False)
    qs, kvs = _qkv(hs, gq, gkv, wq, wkv, n)
    cols = _sample_cols(qs, kvs)

    sample_groups = [(cols[3 * g:3 * g + 3], _to_lanes(caches[2 * g]), _to_lanes(caches[2 * g + 1]), dil)
                     for g, (_, dil) in enumerate(GROUPS)]
    h, p_h, p_ca = _mixer_prompt(x_prompt, mixer_w)
    h, p_f0, *rolled_wide = _ffn_prompt(h, ffn_w[0], False, TM_FFN_WITH_CACHE,
                                        caches=sample_groups[-1:])
    qkv = _qkv_prompt(h, gq, gkv, wq, wkv)
    outs, lses, p_kv = [], [], []
    for g in range(N_GROUPS):
        o, lse = _attn_prompt_group(*qkv[g][:3], g)
        outs.append(o)
        lses.append(lse)
        for tail in qkv[g][3:]:
            p_kv.append(_to_rows(tail.reshape(b, HEADS, HEAD_DIM, -1)))
    y_prompt, p_f1 = _ffn_prompt(h, ffn_w[1], True, TM_FFN, attention=(outs, lses, wo))

    rolled = [_attn_sample_group(*grp) for grp in sample_groups[:-1]] + [rolled_wide]
    s_kv = [_to_rows(c) for r in rolled for c in r[2:]]
    os_, lses_s = _sample_rows(jnp.stack([r[0] for r in rolled]), jnp.stack([r[1] for r in rolled]))
    part = lambda a, j: a[:n, j * ATT_W:(j + 1) * ATT_W]
    hs = _merge(hs, [part(os_, g) for g in range(N_GROUPS)],
                [part(lses_s, g) for g in range(N_GROUPS)], wo, n)
    y_sample, s_ug1 = _ffn_sample(hs, jnp.swapaxes(state_ffn_conv[1], 0, 1), ffn_w[1], final_norm=True)
    s_ffn = jnp.stack([
        jnp.concatenate([state_ffn_conv[l][:, 1:], ug[:, None]], axis=1)
        for l, ug in enumerate((s_ug0, s_ug1))])
    return (y_prompt, y_sample.reshape(n, 1, d),
            p_h.reshape(1, b, d), s_h.reshape(1, n, d),
            p_ca.reshape(1, b, -1, d), s_ca.reshape(1, n, -1, d),
            jnp.stack([p_f0, p_f1]), s_ffn,
            p_kv[0], p_kv[1], s_kv[0], s_kv[1],
            p_kv[2], p_kv[3], s_kv[2], s_kv[3],
            p_kv[4], p_kv[5], s_kv[4], s_kv[5])
```

```python
import functools

import jax
import jax.numpy as jnp
from jax import lax
from jax.experimental import pallas as pl
from jax.experimental.pallas import tpu as pltpu

F32 = jnp.float32
BF16 = jnp.bfloat16

EPS = 1e-6
LRU_C = 8.0
N_LRU_BLOCKS = 4
HEAD_DIM = 64
HEADS = 8
GROUPS = ((128, 1), (512, 4), (2048, 16))
N_GROUPS = len(GROUPS)
ATT_W = HEADS * HEAD_DIM
ATT_BLK = 128
LANES = 128
CARRY_ROWS = 8

TQ_MIXER = 512
STAGE_SKEW_ROWS = 8
TM_FFN = 512
TM_FFN_WITH_CACHE = 256
FC_FFN = 512
TM_TOKEN = 512
HEADS_TOGETHER_MAX_W = 512
VMEM_LIMIT = 52 * 1024 * 1024


def _rmsnorm(x, g):
    return x * lax.rsqrt(jnp.mean(x * x, axis=-1, keepdims=True) + EPS) * g


def _bdot(a, w):
    return jnp.dot(a.astype(BF16), w, preferred_element_type=F32)


def _shift_rows(cur, prev, s):
    rolled = pltpu.roll(cur, s, axis=0)
    prev_rolled = pltpu.roll(prev, s, axis=0)
    row = lax.broadcasted_iota(jnp.int32, prev.shape, 0)
    head = jnp.where(row < s, prev_rolled, rolled[:CARRY_ROWS])
    return jnp.concatenate([head, rolled[CARRY_ROWS:]], axis=0)


def _causal_conv_rows(cur, prev, w_ref, b):
    k = w_ref.shape[0]
    y = b + w_ref[k - 1:k, :] * cur
    for j in range(k - 1):
        y = y + w_ref[j:j + 1, :] * _shift_rows(cur, prev, k - 1 - j)
    return y


_GELU_C = 0.7978845608028654
_GELU_C3 = _GELU_C * 0.044715


def _gelu(x):
    half = 0.5 * x
    return half + half * jnp.tanh(x * (_GELU_C + _GELU_C3 * (x * x)))


def _sigmoid(x):
    return 0.5 + 0.5 * jnp.tanh(0.5 * x)


def _sqrt_nonneg(v):
    return jnp.where(v > 0.0, v * lax.rsqrt(v), 0.0)


def _log_sigmoid(x):
    return jnp.minimum(x, 0.0) - jnp.log1p(jnp.exp(-jnp.abs(x)))


def _lru_coeffs(xc, wa_ref, ba, wx_ref, bx, lam):
    xb = xc.astype(BF16)
    blk = xc.shape[-1] // N_LRU_BLOCKS
    ra, ri = [], []
    for n in range(N_LRU_BLOCKS):
        xs = xb[:, n * blk:(n + 1) * blk]
        ra.append(jnp.dot(xs, wa_ref[n], preferred_element_type=F32))
        ri.append(jnp.dot(xs, wx_ref[n], preferred_element_type=F32))
    r = _sigmoid(jnp.concatenate(ra, axis=-1) + ba)
    i = _sigmoid(jnp.concatenate(ri, axis=-1) + bx)
    log_a = LRU_C * r * _log_sigmoid(lam)
    a = jnp.exp(log_a)
    one_minus_a2 = (1.0 + a * a) * jnp.tanh(-log_a)
    bt = _sqrt_nonneg(one_minus_a2) * (i * xc)
    return a, bt


def _interleave_rows(x, stage):
    n, d = x.shape
    run = n // CARRY_ROWS
    pitch = stage.shape[1] // CARRY_ROWS
    for c in range(d // LANES):
        for j in range(CARRY_ROWS):
            stage[c, j * pitch:j * pitch + run, :] = x[j * run:(j + 1) * run, c * LANES:(c + 1) * LANES]
    slabs = [jnp.concatenate([stage[c, pl.ds(i, CARRY_ROWS, stride=pitch), :]
                              for c in range(d // LANES)], axis=1) for i in range(run)]
    return jnp.stack(slabs)


def _deinterleave_rows(x, stage):
    run, _, d = x.shape
    pitch = stage.shape[1] // CARRY_ROWS
    for i in range(run):
        for c in range(d // LANES):
            stage[c, pl.ds(i, CARRY_ROWS, stride=pitch), :] = x[i, :, c * LANES:(c + 1) * LANES]
    return jnp.concatenate(
        [jnp.concatenate([stage[c, j * pitch:j * pitch + run, :] for j in range(CARRY_ROWS)], axis=0)
         for c in range(d // LANES)], axis=1)


def _delayed(cur, prev_tail, s):
    run = cur.shape[0]
    sub = lax.broadcasted_iota(jnp.int32, (s,) + cur.shape[1:], 1)
    edge = jnp.where(sub == 0,
                     pltpu.roll(prev_tail[prev_tail.shape[0] - s:], 1, axis=1),
                     pltpu.roll(cur[run - s:], 1, axis=1))
    return jnp.concatenate([edge, cur[:run - s]], axis=0)


def _scan_interleaved(a, b, h_init):
    run = a.shape[0]
    h, acc = b[0], a[0]
    hs, accs = [h], [acc]
    for i in range(1, run):
        h = a[i] * h + b[i]
        acc = a[i] * acc
        hs.append(h)
        accs.append(acc)
    tot_a, tot_b = acc, h
    sub = lax.broadcasted_iota(jnp.int32, tot_a.shape, 0)
    s = 1
    while s < CARRY_ROWS:
        keep = sub >= s
        a_sh = jnp.where(keep, pltpu.roll(tot_a, s, axis=0), 1.0)
        b_sh = jnp.where(keep, pltpu.roll(tot_b, s, axis=0), 0.0)
        tot_b = tot_a * b_sh + tot_b
        tot_a = tot_a * a_sh
        s *= 2
    ends = tot_a * h_init + tot_b
    starts = jnp.where(sub == 0, h_init, pltpu.roll(ends, 1, axis=0))
    out = jnp.stack([hs[i] + accs[i] * starts for i in range(run)])
    return out, ends[CARRY_ROWS - 1:, :]


def _mixer_prompt_kernel(x_ref, g_ref, win_ref, cw_ref, cb_ref, wa_ref, ba_ref,
                         wx_ref, bx_ref, lam_ref, wout_ref,
                         y_ref, hlast_ref, cbuf_ref, stage, utail, hcarry):
    t = pl.program_id(1)
    tq, d = x_ref.shape
    run = tq // CARRY_ROWS
    taps = cw_ref.shape[0]

    @pl.when(t == 0)
    def _():
        utail[...] = jnp.zeros_like(utail)
        hcarry[...] = jnp.zeros_like(hcarry)

    x = _interleave_rows(x_ref[...], stage).reshape(tq, d)
    u = _bdot(_rmsnorm(x, g_ref[...]), win_ref[...])
    gate = _gelu(u[:, :d])
    ur = u[:, d:].reshape(run, CARRY_ROWS, d)
    prev_tail = utail[...]
    xc = cb_ref[...] + cw_ref[taps - 1:taps, :] * ur
    for j in range(taps - 1):
        xc = xc + cw_ref[j:j + 1, :] * _delayed(ur, prev_tail, taps - 1 - j)
    utail[...] = ur[run - (taps - 1):]
    a, bt = _lru_coeffs(xc.reshape(tq, d), wa_ref, ba_ref[...], wx_ref, bx_ref[...], lam_ref[...])
    h, h_end = _scan_interleaved(a.reshape(run, CARRY_ROWS, d), bt.reshape(run, CARRY_ROWS, d),
                                 hcarry[...])
    hcarry[...] = h_end
    y = x + _bdot(h.reshape(tq, d) * gate, wout_ref[...])
    y_ref[...] = _deinterleave_rows(y.reshape(run, CARRY_ROWS, d), stage)

    @pl.when(t == pl.num_programs(1) - 1)
    def _():
        hlast_ref[...] = h_end
        cbuf_ref[...] = ur[run - (taps - 1):, CARRY_ROWS - 1, :]


def _mixer_sample_kernel(x_ref, h0_ref, buf_ref, g_ref, win_ref, cw_ref, cb_ref,
                         wa_ref, ba_ref, wx_ref, bx_ref, lam_ref, wout_ref,
                         y_ref, hnew_ref, ur_ref):
    d = x_ref.shape[-1]
    k = cw_ref.shape[0]
    x = x_ref[...]
    u = _bdot(_rmsnorm(x, g_ref[...]), win_ref[...])
    gate = _gelu(u[:, :d])
    ur = u[:, d:]
    xc = cb_ref[...] + cw_ref[k - 1:k, :] * ur
    for j in range(k - 1):
        xc = xc + cw_ref[j:j + 1, :] * buf_ref[j]
    a, bt = _lru_coeffs(xc, wa_ref, ba_ref[...], wx_ref, bx_ref[...], lam_ref[...])
    h = a * h0_ref[...] + bt
    hnew_ref[...] = h
    ur_ref[...] = ur
    y_ref[...] = x + _bdot(h * gate, wout_ref[...])


def _full_spec(shape):
    zeros = (0,) * len(shape)
    return pl.BlockSpec(shape, lambda *_: zeros)


class _Layer:
    def __init__(self, stacked, index):
        self.stacked, self.index = stacked, index

    @property
    def shape(self):
        return self.stacked.shape[1:]


def _weight_spec(w):
    zeros = (0,) * len(w.shape)
    if isinstance(w, _Layer):
        return pl.BlockSpec((None,) + w.shape, lambda *_: (w.index,) + zeros,
                            pipeline_mode=pl.Buffered(1))
    return pl.BlockSpec(w.shape, lambda *_: zeros, pipeline_mode=pl.Buffered(1))


def _weight_specs(weights):
    return [_weight_spec(w) for w in weights]


def _weight_arrays(weights):
    return [w.stacked if isinstance(w, _Layer) else w for w in weights]


def _mixer_prompt(x, weights):
    b, t, d = x.shape
    tq = TQ_MIXER
    k = weights[2].shape[0]
    row = lambda bi, ti: (bi, ti, 0)
    per_seq = lambda bi, ti: (bi, 0, 0)
    return pl.pallas_call(
        _mixer_prompt_kernel,
        grid=(b, t // tq),
        in_specs=[pl.BlockSpec((None, tq, d), row)] + _weight_specs(weights),
        out_specs=[pl.BlockSpec((None, tq, d), row),
                   pl.BlockSpec((None, 1, d), per_seq),
                   pl.BlockSpec((None, k - 1, d), per_seq)],
        out_shape=[jax.ShapeDtypeStruct((b, t, d), F32),
                   jax.ShapeDtypeStruct((b, 1, d), F32),
                   jax.ShapeDtypeStruct((b, k - 1, d), F32)],
        scratch_shapes=[pltpu.VMEM((d // LANES, tq + CARRY_ROWS * STAGE_SKEW_ROWS, LANES), F32),
                        pltpu.VMEM((k - 1, CARRY_ROWS, d), F32),
                        pltpu.VMEM((1, d), F32)],
        compiler_params=pltpu.CompilerParams(
            dimension_semantics=("arbitrary", "arbitrary"),
            vmem_limit_bytes=VMEM_LIMIT),
        name="mixer_prompt",
    )(x, *weights)


def _mixer_sample(x, h0, buf, weights):
    n, d = x.shape
    k = weights[2].shape[0]
    return pl.pallas_call(
        _mixer_sample_kernel,
        grid=(1,),
        in_specs=[_full_spec((n, d)), _full_spec((n, d)), _full_spec((k - 1, n, d))]
        + _weight_specs(weights),
        out_specs=[_full_spec((n, d))] * 3,
        out_shape=[jax.ShapeDtypeStruct((n, d), F32)] * 3,
        compiler_params=pltpu.CompilerParams(
            dimension_semantics=("arbitrary",), vmem_limit_bytes=VMEM_LIMIT),
        name="mixer_sample",
    )(x, h0, buf, *weights)


def _ffn_chunks(xb, x, wup_ref, cw_ref, cb_ref, wdown_ref, conv_gate, act_ref=None, fc=FC_FFN):
    f = wdown_ref.shape[0]
    acc = x
    for lo in range(0, f, fc):
        hi = lo + fc
        ug = jnp.dot(xb, wup_ref[:, lo:hi], preferred_element_type=F32)
        uv = jnp.dot(xb, wup_ref[:, f + lo:f + hi], preferred_element_type=F32)
        gc = conv_gate(ug, lo, hi)
        act = (_gelu(gc) * uv).astype(BF16)
        if act_ref is None:
            acc = acc + jnp.dot(act, wdown_ref[lo:hi, :], preferred_element_type=F32)
        else:
            act_ref[:, lo:hi] = act
    if act_ref is not None:
        acc = acc + jnp.dot(act_ref[...], wdown_ref[...], preferred_element_type=F32)
    return acc


def _ffn_prompt_kernel(*refs, final_norm, fc, cache_dils, attention):
    refs = list(refs)
    x_ref, g_ref, wup_ref, cw_ref, cb_ref, wdown_ref = refs[:6]
    del refs[:6]
    fn_ref = refs.pop(0) if final_norm else None
    if attention:
        att_refs = refs[:2 * N_GROUPS + 2]
        del refs[:2 * N_GROUPS + 2]
    cache_in = [refs[3 * j:3 * j + 3] for j in range(len(cache_dils))]
    del refs[:3 * len(cache_dils)]
    y_ref, fbuf_ref = refs[:2]
    del refs[:2]
    cache_out = [refs[4 * j:4 * j + 4] for j in range(len(cache_dils))]
    del refs[:4 * len(cache_dils)]
    gcarry, act_ref = refs[:2]
    att_scratch = refs[2:]
    t = pl.program_id(1)
    tm = x_ref.shape[0]
    k = cw_ref.shape[0]

    @pl.when(t == 0)
    def _():
        gcarry[...] = jnp.zeros_like(gcarry)

    step = pl.program_id(0) * pl.num_programs(1) + t
    for dil, (cols_ref, kc_ref, vc_ref), (oc_ref, lc_ref, ko_ref, vo_ref) in zip(
            cache_dils, cache_in, cache_out):
        nh, dh, w = kc_ref.shape
        per_seq = HEADS // nh

        @pl.when(step == 0)
        def _():
            oc_ref[...] = jnp.zeros_like(oc_ref)
            lc_ref[...] = jnp.zeros_like(lc_ref)

        _attend_and_roll(cols_ref, kc_ref, vc_ref, ko_ref, vo_ref, oc_ref, lc_ref,
                         step // per_seq, (step % per_seq) * (nh * dh), dil,
                         1 if w > HEADS_TOGETHER_MAX_W else nh)

    def conv_gate(ug, lo, hi):
        gc = _causal_conv_rows(ug, gcarry[:, lo:hi], cw_ref.at[:, lo:hi], cb_ref[:, lo:hi])
        gcarry[:, lo:hi] = ug[tm - CARRY_ROWS:, :]
        return gc

    x = x_ref[...]
    if attention:
        x = _attention_residual(x, att_refs[:N_GROUPS], att_refs[N_GROUPS:2 * N_GROUPS],
                                att_refs[-2], att_refs[-1], *att_scratch)
    xb = _rmsnorm(x, g_ref[...]).astype(BF16)
    y = _ffn_chunks(xb, x, wup_ref, cw_ref, cb_ref, wdown_ref, conv_gate, act_ref, fc)
    if final_norm:
        y = _rmsnorm(y, fn_ref[...])
    y_ref[...] = y

    @pl.when(t == pl.num_programs(1) - 1)
    def _():
        fbuf_ref[...] = gcarry[CARRY_ROWS - (k - 1):, :]


def _ffn_sample_kernel(*refs, final_norm):
    if final_norm:
        (x_ref, buf_ref, g_ref, wup_ref, cw_ref, cb_ref, wdown_ref, fn_ref,
         y_ref, ug_ref) = refs
    else:
        (x_ref, buf_ref, g_ref, wup_ref, cw_ref, cb_ref, wdown_ref,
         y_ref, ug_ref) = refs
    k = cw_ref.shape[0]

    def conv_gate(ug, lo, hi):
        ug_ref[:, lo:hi] = ug
        gc = cb_ref[:, lo:hi] + cw_ref[k - 1:k, lo:hi] * ug
        for j in range(k - 1):
            gc = gc + cw_ref[j:j + 1, lo:hi] * buf_ref[j, :, lo:hi]
        return gc

    x = x_ref[...]
    xb = _rmsnorm(x, g_ref[...]).astype(BF16)
    y = _ffn_chunks(xb, x, wup_ref, cw_ref, cb_ref, wdown_ref, conv_gate)
    if final_norm:
        y = _rmsnorm(y, fn_ref[...])
    y_ref[...] = y


def _ffn_prompt(x, weights, final_norm, tm, caches=(), attention=None):
    b, t, d = x.shape
    f = weights[4].shape[0]
    k = weights[2].shape[0]
    nt = t // tm
    row = lambda bi, ti: (bi, ti, 0)
    per_seq = lambda bi, ti: (bi, 0, 0)
    in_specs = [pl.BlockSpec((None, tm, d), row)] + _weight_specs(weights)
    out_specs = [pl.BlockSpec((None, tm, d), row), pl.BlockSpec((None, k - 1, f), per_seq)]
    out_shape = [jax.ShapeDtypeStruct((b, t, d), F32), jax.ShapeDtypeStruct((b, k - 1, f), F32)]
    operands = [x, *_weight_arrays(weights)]
    scratch = [pltpu.VMEM((CARRY_ROWS, f), F32), pltpu.VMEM((tm, f), BF16)]
    if attention is not None:
        att_specs, att_operands, att_scratch = _attention_residual_operands(*attention, tm)
        in_specs += att_specs
        operands += att_operands
        scratch += att_scratch
    cache_in_specs, cache_out_specs, cache_out_shape, cache_operands = [], [], [], []
    for cols, kc, vc, _ in caches:
        n, heads, dh, w = kc.shape
        steps_per_seq = (b * nt) // n
        nh = heads // steps_per_seq
        assert steps_per_seq * n == b * nt and nh * steps_per_seq == heads
        blk = pl.BlockSpec((None, nh, dh, w),
                           lambda bi, ti, s=steps_per_seq: ((bi * nt + ti) // s, (bi * nt + ti) % s, 0, 0))
        col = _full_spec(cols.shape[1:])
        cache_in_specs += [_full_spec(cols.shape), blk, blk]
        cache_out_specs += [col, col, blk, blk]
        cache_out_shape += [jax.ShapeDtypeStruct(cols.shape[1:], F32)] * 2
        cache_out_shape += [jax.ShapeDtypeStruct(kc.shape, F32)] * 2
        cache_operands += [cols, kc, vc]
    in_specs += cache_in_specs
    out_specs += cache_out_specs
    out_shape += cache_out_shape
    operands += cache_operands
    return pl.pallas_call(
        functools.partial(_ffn_prompt_kernel, final_norm=final_norm, fc=FC_FFN,
                          cache_dils=tuple(c[3] for c in caches),
                          attention=attention is not None),
        grid=(b, nt),
        in_specs=in_specs,
        out_specs=out_specs,
        out_shape=out_shape,
        scratch_shapes=scratch,
        compiler_params=pltpu.CompilerParams(
            dimension_semantics=("arbitrary", "arbitrary"),
            vmem_limit_bytes=VMEM_LIMIT),
        name="ffn_prompt_final" if final_norm else "ffn_prompt",
    )(*operands)


def _ffn_sample(x, buf, weights, final_norm):
    n, d = x.shape
    f = weights[4].shape[0]
    k = weights[2].shape[0]
    return pl.pallas_call(
        functools.partial(_ffn_sample_kernel, final_norm=final_norm),
        grid=(1,),
        in_specs=[_full_spec((n, d)), _full_spec((k - 1, n, f))]
        + _weight_specs(weights),
        out_specs=[_full_spec((n, d)), _full_spec((n, f))],
        out_shape=[jax.ShapeDtypeStruct((n, d), F32), jax.ShapeDtypeStruct((n, f), F32)],
        compiler_params=pltpu.CompilerParams(
            dimension_semantics=("arbitrary",), vmem_limit_bytes=VMEM_LIMIT),
        name="ffn_sample_final" if final_norm else "ffn_sample",
    )(x, buf, *_weight_arrays(weights))


def _qkv_kernel(x_ref, gq_ref, gkv_ref, wq_ref, wkv_ref, q_ref, kv_ref):
    x = x_ref[...]
    inv = lax.rsqrt(jnp.mean(x * x, axis=-1, keepdims=True) + EPS)
    y = x * inv
    q_ref[...] = _bdot(y * gq_ref[...], wq_ref[...])
    kv_ref[...] = _bdot(y * gkv_ref[...], wkv_ref[...])


def _qkv(x, gq, gkv, wq, wkv, tm):
    m, d = x.shape
    nq, nkv = wq.shape[1], wkv.shape[1]
    row = lambda i: (i, 0)
    return pl.pallas_call(
        _qkv_kernel,
        grid=(m // tm,),
        in_specs=[pl.BlockSpec((tm, d), row)] + _weight_specs((gq, gkv, wq, wkv)),
        out_specs=[pl.BlockSpec((tm, nq), row), pl.BlockSpec((tm, nkv), row)],
        out_shape=[jax.ShapeDtypeStruct((m, nq), F32), jax.ShapeDtypeStruct((m, nkv), F32)],
        compiler_params=pltpu.CompilerParams(
            dimension_semantics=("arbitrary",), vmem_limit_bytes=VMEM_LIMIT),
        name="qkv_proj",
    )(x, gq, gkv, wq, wkv)


LSE_REP = ATT_BLK // HEADS


_NT = (((1,), (1,)), ((), ()))


def _qkv_prompt_kernel(x_ref, gq_ref, gkv_ref, wq_ref, wkv_ref, *out_and_scratch):
    out_refs, ybuf = out_and_scratch[:-1], out_and_scratch[-1]
    x = x_ref[...]
    y = x * lax.rsqrt(jnp.mean(x * x, axis=-1, keepdims=True) + EPS)
    chunks = ybuf.shape[0]
    for c in range(chunks):
        ybuf[c] = y[:, c * LANES:(c + 1) * LANES]
    for g in range(N_GROUPS):
        q_ref, k_ref, v_ref = out_refs[3 * g:3 * g + 3]
        part = lambda ref, j: ref[:, j * ATT_W:(j + 1) * ATT_W]
        dil, per = q_ref.shape[0], q_ref.shape[1]
        yg = y
        if dil > 1:
            yg = jnp.concatenate(
                [jnp.concatenate([ybuf[c, pl.ds(r, per, stride=dil), :] for c in range(chunks)], axis=1)
                 for r in range(dil)], axis=0)
        xq = (yg * gq_ref[...]).astype(BF16)
        xkv = (yg * gkv_ref[...]).astype(BF16)
        q = jnp.dot(xq, part(wq_ref, g), preferred_element_type=F32) * (HEAD_DIM ** -0.5)
        k = jnp.dot(xkv, part(wkv_ref, g), preferred_element_type=F32)
        v = jnp.dot(xkv, part(wkv_ref, N_GROUPS + g), preferred_element_type=F32)
        q_ref[...] = q.astype(BF16).reshape(dil, per, ATT_W)
        k_ref[...] = k.astype(BF16).reshape(dil, per, ATT_W)
        v_ref[...] = v.astype(BF16).reshape(dil, per, ATT_W)


def _qkv_prompt(x, gq, gkv, wq, wkv):
    b, t, d = x.shape
    tm = TM_TOKEN
    nt = t // tm
    specs, shapes = [], []
    for _, dil in GROUPS:
        assert tm % (dil * 16) == 0
        specs += [pl.BlockSpec((None, dil, tm // dil, ATT_W), lambda bi, ti: (bi, 0, ti, 0))] * 3
        shapes += [jax.ShapeDtypeStruct((b, dil, t // dil, ATT_W), BF16)] * 3
    weights = (gq, gkv, wq, wkv)
    outs = pl.pallas_call(
        _qkv_prompt_kernel,
        grid=(b, nt),
        in_specs=[pl.BlockSpec((None, tm, d), lambda bi, ti: (bi, ti, 0))] + _weight_specs(weights),
        out_specs=specs,
        out_shape=shapes,
        scratch_shapes=[pltpu.VMEM((d // LANES, tm, LANES), F32)],
        compiler_params=pltpu.CompilerParams(
            dimension_semantics=("arbitrary", "arbitrary"), vmem_limit_bytes=VMEM_LIMIT),
        name="qkv_prompt",
    )(x, *weights)
    return [outs[3 * g:3 * g + 3] for g in range(N_GROUPS)]


def _kv_tails_kernel(x_ref, gkv_ref, wkv_ref, *out_refs):
    x = x_ref[...]
    y = x * lax.rsqrt(jnp.mean(x * x, axis=-1, keepdims=True) + EPS)
    xkv = (y * gkv_ref[...]).astype(BF16)
    rows_held = x.shape[0]
    w_cols = lambda j: wkv_ref[:, j * ATT_W:(j + 1) * ATT_W]
    transposed_product = (((0,), (1,)), ((), ()))
    for g in range(N_GROUPS):
        kt_ref, vt_ref = out_refs[2 * g:2 * g + 2]
        rows = xkv[rows_held - kt_ref.shape[1]:, :]
        kt_ref[...] = lax.dot_general(w_cols(g), rows, transposed_product,
                                      preferred_element_type=F32)
        vt_ref[...] = lax.dot_general(w_cols(N_GROUPS + g), rows, transposed_product,
                                      preferred_element_type=F32)


def _kv_tails(x, gkv, wkv):
    b, t, d = x.shape
    wins = [min(win, t) for win, _ in GROUPS]
    held = max(wins)
    assert t % held == 0
    specs, shapes = [], []
    for win in wins:
        specs += [pl.BlockSpec((None, ATT_W, win), lambda bi: (bi, 0, 0))] * 2
        shapes += [jax.ShapeDtypeStruct((b, ATT_W, win), F32)] * 2
    weights = (gkv, wkv)
    outs = pl.pallas_call(
        _kv_tails_kernel,
        grid=(b,),
        in_specs=[pl.BlockSpec((None, held, d), lambda bi: (bi, t // held - 1, 0))]
        + _weight_specs(weights),
        out_specs=specs,
        out_shape=shapes,
        compiler_params=pltpu.CompilerParams(
            dimension_semantics=("arbitrary",), vmem_limit_bytes=VMEM_LIMIT),
        name="kv_tails",
    )(x, *weights)
    return [outs[2 * g:2 * g + 2] for g in range(N_GROUPS)]


def _attn_prompt_kernel(q_ref, kp_ref, k_ref, vp_ref, v_ref, o_ref, lse_ref):
    nres, blk = kp_ref.shape[:2]
    nblk = q_ref.shape[1] // blk
    qi = lax.broadcasted_iota(jnp.int32, (blk, 2 * blk), 0)
    kj = lax.broadcasted_iota(jnp.int32, (blk, 2 * blk), 1)
    band = jnp.logical_or(jnp.logical_and(kj < blk, kj >= qi),
                          jnp.logical_and(kj >= blk, kj - blk <= qi))
    band_first = jnp.logical_and(band, jnp.logical_or(kj >= blk, pl.program_id(2) > 0))
    lane = lax.broadcasted_iota(jnp.int32, (blk, blk), 1)
    low_half = lane < HEAD_DIM
    head_of_lane = lane // LSE_REP
    zero = jnp.zeros((), BF16)
    for r, j in [(r, j) for r in range(nres) for j in range(nblk)]:
        rows = slice(j * blk, (j + 1) * blk)
        q = q_ref[r, rows, :]
        if j == 0:
            keep = band_first
            k = jnp.concatenate([kp_ref[r], k_ref[r, :blk, :]], axis=0)
            v = jnp.concatenate([vp_ref[r], v_ref[r, :blk, :]], axis=0)
        else:
            keep = band
            k = k_ref[r, (j - 1) * blk:(j + 1) * blk, :]
            v = v_ref[r, (j - 1) * blk:(j + 1) * blk, :]
        lse_tile = jnp.zeros((blk, blk), F32)
        for pair in range(HEADS // 2):
            cols = slice(pair * blk, (pair + 1) * blk)
            q2, k2, v2 = q[:, cols], k[:, cols], v[:, cols]
            outs = []
            for half in range(2):
                mine = low_half if half == 0 else jnp.logical_not(low_half)
                s = lax.dot_general(jnp.where(mine, q2, zero), k2, _NT, preferred_element_type=F32)
                s = jnp.where(keep, s, -jnp.inf)
                m = jnp.max(s, axis=-1, keepdims=True)
                p = jnp.exp(s - m)
                den = jnp.sum(p, axis=-1, keepdims=True)
                outs.append(jnp.dot(p.astype(BF16), v2, preferred_element_type=F32) / den)
                lse_tile = jnp.where(head_of_lane == 2 * pair + half, m + jnp.log(den), lse_tile)
            o_ref[r, rows, cols] = jnp.where(low_half, outs[0], outs[1])
        lse_ref[r, rows, :] = lse_tile


ATT_NBLK = 8


def _attn_prompt_group(q, k, v, g):
    b, dil, n, _ = q.shape
    nblk = min(ATT_NBLK, n // ATT_BLK)
    nres = min(dil, ATT_NBLK // nblk)
    tq = nblk * ATT_BLK
    assert n % tq == 0 and dil % nres == 0
    prev = lambda i: jnp.maximum(i * nblk - 1, 0)
    rows = lambda size, width: pl.BlockSpec((None, nres, size, width), lambda bi, r, i: (bi, r, i, 0))
    before = pl.BlockSpec((None, nres, ATT_BLK, ATT_W), lambda bi, r, i: (bi, r, prev(i), 0))
    return pl.pallas_call(
        _attn_prompt_kernel,
        grid=(b, dil // nres, n // tq),
        in_specs=[rows(tq, ATT_W), before, rows(tq, ATT_W), before, rows(tq, ATT_W)],
        out_specs=[rows(tq, ATT_W), rows(tq, ATT_BLK)],
        out_shape=[jax.ShapeDtypeStruct((b, dil, n, ATT_W), F32),
                   jax.ShapeDtypeStruct((b, dil, n, ATT_BLK), F32)],
        compiler_params=pltpu.CompilerParams(
            dimension_semantics=("arbitrary", "arbitrary", "arbitrary"),
            vmem_limit_bytes=VMEM_LIMIT),
        name=f"attn_prompt_g{g}",
    )(q, k, k, v, v)


def _attention_residual(x, o_refs, l_refs, sel_ref, wo_ref, obuf, lbuf):
    def token_order(ref, buf):
        dil, per, width = ref.shape
        if dil == 1:
            return ref[0]
        chunks = width // LANES
        for r in range(dil):
            rows = ref[r]
            for c in range(chunks):
                buf[c, pl.ds(r, per, stride=dil), :] = rows[:, c * LANES:(c + 1) * LANES]
        return jnp.concatenate([buf[c] for c in range(chunks)], axis=1)

    lses = [token_order(l_ref, lbuf) for l_ref in l_refs]
    m = jnp.maximum(jnp.maximum(lses[0], lses[1]), lses[2])
    es = [jnp.exp(l - m) for l in lses]
    den = es[0] + es[1] + es[2]
    comb = None
    for e, o_ref in zip(es, o_refs):
        w = e / den
        hi = w.astype(BF16)
        lo = (w - hi.astype(F32)).astype(BF16)
        wide = (jnp.dot(hi, sel_ref[...], preferred_element_type=F32)
                + jnp.dot(lo, sel_ref[...], preferred_element_type=F32))
        term = wide * token_order(o_ref, obuf)
        comb = term if comb is None else comb + term
    return x + _bdot(comb, wo_ref[...])


def _attention_residual_operands(outs, lses, wo, tm):
    head_of_lane = jnp.arange(ATT_BLK) // LSE_REP
    first_copy = jnp.arange(ATT_BLK) % LSE_REP == 0
    sel = ((head_of_lane[:, None] == (jnp.arange(ATT_W) // HEAD_DIM)[None, :])
           & first_copy[:, None]).astype(BF16)
    specs = []
    for width, parts in ((ATT_W, outs), (ATT_BLK, lses)):
        for p in parts:
            dil = p.shape[1]
            specs.append(pl.BlockSpec((None, dil, tm // dil, width), lambda bi, ti: (bi, 0, ti, 0)))
    specs += _weight_specs((sel, wo))
    scratch = [pltpu.VMEM((ATT_W // LANES, tm, LANES), F32),
               pltpu.VMEM((ATT_BLK // LANES, tm, LANES), F32)]
    return specs, [*outs, *lses, sel, wo], scratch


SAMPLE_PAD = 128
SAMPLE_BB = {128: 8, 512: 4}


def _sample_cols_kernel(q_ref, kv_ref, cols_ref):
    for g in range(N_GROUPS):
        part = lambda ref, j: ref[:, j * ATT_W:(j + 1) * ATT_W]
        cols_ref[3 * g] = part(q_ref, g).T * (HEAD_DIM ** -0.5)
        cols_ref[3 * g + 1] = part(kv_ref, g).T
        cols_ref[3 * g + 2] = part(kv_ref, N_GROUPS + g).T


def _sample_rows_kernel(oc_ref, lc_ref, o_ref, lse_ref):
    for g in range(N_GROUPS):
        o_ref[:, g * ATT_W:(g + 1) * ATT_W] = oc_ref[g].T
        lse_ref[:, g * ATT_W:(g + 1) * ATT_W] = lc_ref[g].T


def _sample_cols(q, kv):
    pad = lambda x: jnp.pad(x, ((0, SAMPLE_PAD - x.shape[0]), (0, 0)))
    q, kv = pad(q), pad(kv)
    shape = (3 * N_GROUPS, ATT_W, SAMPLE_PAD)
    return pl.pallas_call(
        _sample_cols_kernel,
        grid=(1,),
        in_specs=[_full_spec(q.shape), _full_spec(kv.shape)],
        out_specs=_full_spec(shape),
        out_shape=jax.ShapeDtypeStruct(shape, F32),
        compiler_params=pltpu.CompilerParams(
            dimension_semantics=("arbitrary",), vmem_limit_bytes=VMEM_LIMIT),
        name="sample_cols",
    )(q, kv)


def _sample_rows(oc, lc):
    shape = (SAMPLE_PAD, N_GROUPS * ATT_W)
    return pl.pallas_call(
        _sample_rows_kernel,
        grid=(1,),
        in_specs=[_full_spec(oc.shape), _full_spec(lc.shape)],
        out_specs=[_full_spec(shape)] * 2,
        out_shape=[jax.ShapeDtypeStruct(shape, F32)] * 2,
        compiler_params=pltpu.CompilerParams(
            dimension_semantics=("arbitrary",), vmem_limit_bytes=VMEM_LIMIT),
        name="sample_rows",
    )(oc, lc)


def _attend_and_roll(cols_ref, k_ref, v_ref, ko_ref, vo_ref, oc_ref, lc_ref, seq, row0, dil, hb):
    nh, dh, w = k_ref.shape
    npad = oc_ref.shape[1]
    pos = lax.broadcasted_iota(jnp.int32, (1, w), 1)
    attended = (pos & (dil - 1)) == 0
    newest = pos == w - 1
    mine = lax.broadcasted_iota(jnp.int32, (1, npad), 1) == seq

    for grp in range(nh // hb):
        first = row0 + grp * hb * dh
        rows = (slice(first, first + hb * dh) if isinstance(first, int)
                else pl.ds(pl.multiple_of(first, dh), hb * dh))
        hs = slice(grp * hb, (grp + 1) * hb)

        def column(j):
            col = jnp.sum(jnp.where(mine, cols_ref[j, rows, :], 0.0), axis=-1, keepdims=True)
            return col.reshape(hb, dh, 1)

        def shifted(tile, new_col):
            flat = pltpu.roll(tile.reshape(hb * dh, w), w - 1, axis=1)
            return jnp.where(newest, new_col, flat.reshape(hb, dh, w))

        qc, kc, vc = column(0), column(1), column(2)
        kt = k_ref[hs]
        vt = v_ref[hs]
        s = jnp.where(attended, jnp.sum(kt * qc, axis=1, keepdims=True), -jnp.inf)
        s_new = jnp.sum(kc * qc, axis=1, keepdims=True)
        m = jnp.maximum(jnp.max(s, axis=-1, keepdims=True), s_new)
        p = jnp.exp(s - m)
        p_new = jnp.exp(s_new - m)
        den = jnp.sum(p, axis=-1, keepdims=True) + p_new
        o = (jnp.sum(vt * p, axis=-1, keepdims=True) + p_new * vc) / den
        lse = jnp.broadcast_to(m + jnp.log(den), (hb, dh, 1))
        oc_ref[rows, :] = jnp.where(mine, o.reshape(hb * dh, 1), oc_ref[rows, :])
        lc_ref[rows, :] = jnp.where(mine, lse.reshape(hb * dh, 1), lc_ref[rows, :])
        ko_ref[hs] = shifted(kt, kc)
        vo_ref[hs] = shifted(vt, vc)


def _attn_sample_kernel(cols_ref, k_ref, v_ref, oc_ref, lc_ref, ko_ref, vo_ref, *, dil):
    step = pl.program_id(0)
    bb, heads = k_ref.shape[:2]

    @pl.when(step == 0)
    def _():
        oc_ref[...] = jnp.zeros_like(oc_ref)
        lc_ref[...] = jnp.zeros_like(lc_ref)

    for i in range(bb):
        _attend_and_roll(cols_ref, k_ref.at[i], v_ref.at[i], ko_ref.at[i], vo_ref.at[i],
                         oc_ref, lc_ref, step * bb + i, 0, dil, heads)


def _to_lanes(cache):
    return jnp.transpose(cache, (0, 2, 3, 1))


def _to_rows(cache):
    return jnp.transpose(cache, (0, 3, 1, 2))


def _attn_sample_group(cols, k_cache, v_cache, dil):
    n, heads, dh, w = k_cache.shape
    assert w % dil == 0 and w // dil == ATT_BLK
    bb = SAMPLE_BB[w]
    cache_spec = pl.BlockSpec((bb, heads, dh, w), lambda i: (i, 0, 0, 0))
    col_spec = _full_spec((ATT_W, SAMPLE_PAD))
    col_shape = jax.ShapeDtypeStruct((ATT_W, SAMPLE_PAD), F32)
    return pl.pallas_call(
        functools.partial(_attn_sample_kernel, dil=dil),
        grid=(n // bb,),
        in_specs=[_full_spec(cols.shape)] + [cache_spec] * 2,
        out_specs=[col_spec] * 2 + [cache_spec] * 2,
        out_shape=[col_shape] * 2 + [jax.ShapeDtypeStruct(k_cache.shape, F32)] * 2,
        compiler_params=pltpu.CompilerParams(
            dimension_semantics=("arbitrary",), vmem_limit_bytes=VMEM_LIMIT),
        name=f"attn_sample_d{dil}",
    )(cols, k_cache, v_cache)


def _merge_kernel(x_ref, o0_ref, o1_ref, o2_ref, l0_ref, l1_ref, l2_ref, wo_ref, y_ref):
    l0, l1, l2 = l0_ref[...], l1_ref[...], l2_ref[...]
    m = jnp.maximum(jnp.maximum(l0, l1), l2)
    e0, e1, e2 = jnp.exp(l0 - m), jnp.exp(l1 - m), jnp.exp(l2 - m)
    den = e0 + e1 + e2
    o = (e0 / den) * o0_ref[...] + (e1 / den) * o1_ref[...] + (e2 / den) * o2_ref[...]
    y_ref[...] = x_ref[...] + _bdot(o, wo_ref[...])


def _merge(x, outs, lses, wo, tm):
    m, d = x.shape
    w = wo.shape[0]
    row = lambda i: (i, 0)
    part = pl.BlockSpec((tm, w), row)
    return pl.pallas_call(
        _merge_kernel,
        grid=(m // tm,),
        in_specs=[pl.BlockSpec((tm, d), row)] + [part] * 6 + [_weight_spec(wo)],
        out_specs=pl.BlockSpec((tm, d), row),
        out_shape=jax.ShapeDtypeStruct((m, d), F32),
        compiler_params=pltpu.CompilerParams(
            dimension_semantics=("arbitrary",), vmem_limit_bytes=VMEM_LIMIT),
        name="attn_merge",
    )(x, *outs, *lses, wo)


def kernel(x_prompt, x_sample, state_lru_h, state_conv_a, state_ffn_conv, cache_k0, cache_v0, cache_k1, cache_v1, cache_k2, cache_v2, norm_mix, a_w_in, a_conv_w, a_conv_b, a_gate_a_w, a_gate_a_b, a_gate_x_w, a_gate_x_b, a_lambda, a_w_out, kv_norm, w_kv, b_w_q, b_w_o, norm_ffn, ffn_w_up, ffn_conv_w, ffn_conv_b, ffn_w_down, final_norm):
    b, t, d = x_prompt.shape
    n = x_sample.shape[0]
    assert x_sample.shape[1] == 1 and norm_mix.shape[0] == 2
    caches = (cache_k0, cache_v0, cache_k1, cache_v1, cache_k2, cache_v2)
    row = lambda v: v.reshape(1, -1)

    mixer_w = (row(norm_mix[0]), a_w_in[0].astype(BF16), a_conv_w[0], row(a_conv_b[0]),
               a_gate_a_w[0].astype(BF16), row(a_gate_a_b[0]),
               a_gate_x_w[0].astype(BF16), row(a_gate_x_b[0]),
               row(a_lambda[0]), a_w_out[0].astype(BF16))
    w_up, w_down = ffn_w_up.astype(BF16), ffn_w_down.astype(BF16)
    ffn_w = [(row(norm_ffn[l]), _Layer(w_up, l), ffn_conv_w[l],
              row(ffn_conv_b[l]), _Layer(w_down, l)) for l in range(2)]
    ffn_w[1] = ffn_w[1] + (row(final_norm),)
    gq, gkv = row(norm_mix[1]), row(kv_norm)
    wq, wkv, wo = b_w_q[0].astype(BF16), w_kv.astype(BF16), b_w_o[0].astype(BF16)

    xs = x_sample.reshape(n, d)
    conv_buf = jnp.swapaxes(state_conv_a[0], 0, 1)
    hs, s_h, s_ur = _mixer_sample(xs, state_lru_h[0], conv_buf, mixer_w)
    s_ca = jnp.concatenate([state_conv_a[0][:, 1:], s_ur[:, None]], axis=1)
    hs, s_ug0 = _ffn_sample(hs, jnp.swapaxes(state_ffn_conv[0], 0, 1), ffn_w[0], final_norm=False)
    qs, kvs = _qkv(hs, gq, gkv, wq, wkv, n)
    cols = _sample_cols(qs, kvs)

    sample_groups = [(cols[3 * g:3 * g + 3], _to_lanes(caches[2 * g]), _to_lanes(caches[2 * g + 1]), dil)
                     for g, (_, dil) in enumerate(GROUPS)]
    h, p_h, p_ca = _mixer_prompt(x_prompt, mixer_w)
    h, p_f0, *rolled_wide = _ffn_prompt(h, ffn_w[0], False, TM_FFN_WITH_CACHE,
                                        caches=sample_groups[-1:])
    qkv = _qkv_prompt(h, gq, gkv, wq, wkv)
    outs, lses, p_kv = [], [], []
    tails = _kv_tails(h, gkv, wkv)
    for g in range(N_GROUPS):
        o, lse = _attn_prompt_group(*qkv[g], g)
        outs.append(o)
        lses.append(lse)
        for tail in tails[g]:
            p_kv.append(_to_rows(tail.reshape(b, HEADS, HEAD_DIM, -1)))
    y_prompt, p_f1 = _ffn_prompt(h, ffn_w[1], True, TM_FFN, attention=(outs, lses, wo))

    rolled = [_attn_sample_group(*grp) for grp in sample_groups[:-1]] + [rolled_wide]
    s_kv = [_to_rows(c) for r in rolled for c in r[2:]]
    os_, lses_s = _sample_rows(jnp.stack([r[0] for r in rolled]), jnp.stack([r[1] for r in rolled]))
    part = lambda a, j: a[:n, j * ATT_W:(j + 1) * ATT_W]
    hs = _merge(hs, [part(os_, g) for g in range(N_GROUPS)],
                [part(lses_s, g) for g in range(N_GROUPS)], wo, n)
    y_sample, s_ug1 = _ffn_sample(hs, jnp.swapaxes(state_ffn_conv[1], 0, 1), ffn_w[1], final_norm=True)
    s_ffn = jnp.stack([
        jnp.concatenate([state_ffn_conv[l][:, 1:], ug[:, None]], axis=1)
        for l, ug in enumerate((s_ug0, s_ug1))])
    return (y_prompt, y_sample.reshape(n, 1, d),
            p_h.reshape(1, b, d), s_h.reshape(1, n, d),
            p_ca.reshape(1, b, -1, d), s_ca.reshape(1, n, -1, d),
            jnp.stack([p_f0, p_f1]), s_ffn,
            p_kv[0], p_kv[1], s_kv[0], s_kv[1],
            p_kv[2], p_kv[3], s_kv[2], s_kv[3],
            p_kv[4], p_kv[5], s_kv[4], s_kv[5])
```
